```python
import jax, jax.numpy as jnp
from jax import lax
import numpy as np

D_MODEL = 1024
BATCH = 4
SEQ = 4096
DEPTH = 2
DEC_BATCH = 32
DEC_SEQ = 32
PAST_LEN = 4096

CHUNK = 64
N_MIXERS = 2
N_ATTN_LAYERS = (DEPTH + 1) // 2
N_GMLP_LAYERS = DEPTH // 2
HEAD_DIM = 64
N_HEADS = D_MODEL // HEAD_DIM
N_KV_HEADS = 4
GQA_GROUP = N_HEADS // N_KV_HEADS
WINDOW = 128
WIN_CHUNKS = WINDOW // CHUNK
QKV_DIM = (N_HEADS + 2 * N_KV_HEADS) * HEAD_DIM
GMLP_CHUNK = 128
GMLP_HALF = 3 * D_MODEL
GMLP_GROUPS = 8
GMLP_GROUP_DIM = GMLP_HALF // GMLP_GROUPS
D_FF = 4 * D_MODEL
PLE_DIM = 256
EPS = 1e-6
NEG_INF = -1e30

kernel_name = 'hybrid_swa_sink_gmlp_stream_step'


def rms_norm(x, g):
    xf = x.astype(jnp.float32)
    y = xf * lax.rsqrt(jnp.mean(xf * xf, axis=-1, keepdims=True) + EPS)
    return (y * g.astype(jnp.float32)).astype(x.dtype)


def alibi_slopes():
    h = jnp.arange(1, N_HEADS + 1, dtype=jnp.float32)
    return jnp.exp2(-8.0 * h / N_HEADS).reshape(N_KV_HEADS, GQA_GROUP)


def sink_alibi_attention(q, k, v, dist, valid, sinks):
    s = jnp.einsum('bnqkgd,bnskd->bnkgqs', q, k, preferred_element_type=jnp.float32)
    s = s * (HEAD_DIM ** -0.5)
    s = s - alibi_slopes()[:, :, None, None] * dist[None, None]
    if valid is not None:
        s = jnp.where(valid[None, :, None, None, None, :], s, NEG_INF)
    sink = sinks.astype(jnp.float32).reshape(N_KV_HEADS, GQA_GROUP)[None, None, :, :, None, None]
    m = jnp.maximum(jnp.max(s, axis=-1, keepdims=True), sink)
    e = jnp.exp(s - m)
    probs = e / (jnp.sum(e, axis=-1, keepdims=True) + jnp.exp(sink - m))
    return jnp.einsum('bnkgqs,bnskd->bnqkgd', probs.astype(v.dtype), v)


def attention_mixer(h, w_qkv, q_g, k_g, sinks, w_o, cache_k, cache_v):
    B, T, _ = h.shape
    qkv = h @ w_qkv
    q = qkv[..., :N_HEADS * HEAD_DIM].reshape(B, T, N_KV_HEADS, GQA_GROUP, HEAD_DIM)
    k = qkv[..., N_HEADS * HEAD_DIM:(N_HEADS + N_KV_HEADS) * HEAD_DIM].reshape(B, T, N_KV_HEADS, HEAD_DIM)
    v = qkv[..., (N_HEADS + N_KV_HEADS) * HEAD_DIM:].reshape(B, T, N_KV_HEADS, HEAD_DIM)
    q = rms_norm(q, q_g)
    k = rms_norm(k, k_g)
    if cache_k is None:
        nc = T // CHUNK
        pad = ((0, 0), (WINDOW, 0), (0, 0), (0, 0))
        kp = jnp.pad(k, pad).reshape(B, nc + WIN_CHUNKS, CHUNK, N_KV_HEADS, HEAD_DIM)
        vp = jnp.pad(v, pad).reshape(B, nc + WIN_CHUNKS, CHUNK, N_KV_HEADS, HEAD_DIM)
        kb = jnp.concatenate([kp[:, j:j + nc] for j in range(WIN_CHUNKS + 1)], axis=2)
        vb = jnp.concatenate([vp[:, j:j + nc] for j in range(WIN_CHUNKS + 1)], axis=2)
        qb = q.reshape(B, nc, CHUNK, N_KV_HEADS, GQA_GROUP, HEAD_DIM)
        qi = jnp.arange(CHUNK)[:, None]
        kj = jnp.arange(WINDOW + CHUNK)[None, :]
        dist = jnp.abs(qi + WINDOW - kj).astype(jnp.float32)
        key_pos = jnp.arange(nc)[:, None] * CHUNK - WINDOW + kj
        o = sink_alibi_attention(qb, kb, vb, dist, key_pos >= 0, sinks)
        new_k, new_v = k[:, T - WINDOW:], v[:, T - WINDOW:]
    else:
        W = cache_k.shape[1]
        kc = jnp.concatenate([cache_k.astype(k.dtype), k], axis=1)
        vc = jnp.concatenate([cache_v.astype(v.dtype), v], axis=1)
        qi = jnp.arange(T)[:, None]
        kj = jnp.arange(W + T)[None, :]
        dist = jnp.abs(qi + W - kj).astype(jnp.float32)
        o = sink_alibi_attention(q[:, None], kc[:, None], vc[:, None], dist, None, sinks)
        new_k, new_v = kc[:, T:], vc[:, T:]
    y = o.reshape(B, T, N_HEADS * HEAD_DIM) @ w_o
    return y, new_k, new_v


def gmlp_mixer(h, w_uv, v_g, w_s, b_s, w_out):
    B, T, _ = h.shape
    z = jax.nn.gelu(h @ w_uv)
    u, v = z[..., :GMLP_HALF], z[..., GMLP_HALF:]
    v = rms_norm(v, v_g)
    L = min(T, GMLP_CHUNK)
    nc = T // L
    ws = w_s[:, :L, :L] * jnp.tril(jnp.ones((L, L), w_s.dtype))
    vb = v.reshape(B, nc, L, GMLP_GROUPS, GMLP_GROUP_DIM)
    s = jnp.einsum('gij,bnjgc->bnigc', ws, vb) + b_s[:, :L].T[None, None, :, :, None]
    y = (u * s.reshape(B, T, GMLP_HALF)) @ w_out
    return y, v


def trunk(x, p, cache_k, cache_v, g_mix, g_ffn, g_ple,
          attn_w_qkv, attn_q_norm, attn_k_norm, attn_sinks, attn_w_o,
          gmlp_w_uv, gmlp_v_norm, gmlp_w_s, gmlp_b_s, gmlp_w_out,
          ffn_w1, ffn_w2, ple_w_proj, ple_w_gate):
    h = x
    ks, vs, vrows = [], [], []
    for i in range(DEPTH):
        n = rms_norm(h, g_mix[i])
        if i % N_MIXERS == 0:
            a = i // N_MIXERS
            ck = None if cache_k is None else cache_k[a]
            cv = None if cache_v is None else cache_v[a]
            y, nk, nv = attention_mixer(n, attn_w_qkv[a], attn_q_norm[a], attn_k_norm[a],
                                        attn_sinks[a], attn_w_o[a], ck, cv)
            ks.append(nk)
            vs.append(nv)
        else:
            a = i // N_MIXERS
            y, vr = gmlp_mixer(n, gmlp_w_uv[a], gmlp_v_norm[a], gmlp_w_s[a], gmlp_b_s[a], gmlp_w_out[a])
            vrows.append(vr)
        h = h + y
        n = rms_norm(h, g_ffn[i])
        h = h + jnp.square(jax.nn.relu(n @ ffn_w1[i])) @ ffn_w2[i]
        gate = jax.nn.sigmoid(rms_norm(h, g_ple[i]) @ ple_w_gate[i])
        h = h + gate * (p[i] @ ple_w_proj[i])
    return h, jnp.stack(ks), jnp.stack(vs), jnp.stack(vrows)


def setup_inputs(seed: int = 0) -> dict:
    key = jax.random.key(seed)
    ks = jax.random.split(key, 24)
    f32 = jnp.float32
    nrm = lambda k, shape, scale: jax.random.normal(k, shape, f32) * scale
    win_rows = min(WINDOW, PAST_LEN)
    return {
        'x_prompt': nrm(ks[0], (BATCH, SEQ, D_MODEL), 1.0),
        'x_sample': nrm(ks[1], (DEC_BATCH, DEC_SEQ, D_MODEL), 1.0),
        'p_prompt': nrm(ks[2], (DEPTH, BATCH, SEQ, PLE_DIM), 1.0),
        'p_sample': nrm(ks[3], (DEPTH, DEC_BATCH, DEC_SEQ, PLE_DIM), 1.0),
        'cache_k': nrm(ks[4], (N_ATTN_LAYERS, DEC_BATCH, win_rows, N_KV_HEADS, HEAD_DIM), 1.0),
        'cache_v': nrm(ks[5], (N_ATTN_LAYERS, DEC_BATCH, win_rows, N_KV_HEADS, HEAD_DIM), 1.0),
        'g_mix': 1.0 + nrm(ks[6], (DEPTH, D_MODEL), 0.05),
        'g_ffn': 1.0 + nrm(ks[7], (DEPTH, D_MODEL), 0.05),
        'g_ple': 1.0 + nrm(ks[8], (DEPTH, D_MODEL), 0.05),
        'attn_w_qkv': nrm(ks[9], (N_ATTN_LAYERS, D_MODEL, QKV_DIM), D_MODEL ** -0.5),
        'attn_q_norm': 1.0 + nrm(ks[10], (N_ATTN_LAYERS, HEAD_DIM), 0.05),
        'attn_k_norm': 1.0 + nrm(ks[11], (N_ATTN_LAYERS, HEAD_DIM), 0.05),
        'attn_sinks': nrm(ks[12], (N_ATTN_LAYERS, N_HEADS), 1.0),
        'attn_w_o': nrm(ks[13], (N_ATTN_LAYERS, N_HEADS * HEAD_DIM, D_MODEL), (N_HEADS * HEAD_DIM) ** -0.5),
        'gmlp_w_uv': nrm(ks[14], (N_GMLP_LAYERS, D_MODEL, 2 * GMLP_HALF), D_MODEL ** -0.5),
        'gmlp_v_norm': 1.0 + nrm(ks[15], (N_GMLP_LAYERS, GMLP_HALF), 0.05),
        'gmlp_w_s': nrm(ks[16], (N_GMLP_LAYERS, GMLP_GROUPS, GMLP_CHUNK, GMLP_CHUNK), 0.5 * GMLP_CHUNK ** -0.5),
        'gmlp_b_s': 1.0 + nrm(ks[17], (N_GMLP_LAYERS, GMLP_GROUPS, GMLP_CHUNK), 0.02),
        'gmlp_w_out': nrm(ks[18], (N_GMLP_LAYERS, GMLP_HALF, D_MODEL), GMLP_HALF ** -0.5),
        'ffn_w1': nrm(ks[19], (DEPTH, D_MODEL, D_FF), D_MODEL ** -0.5),
        'ffn_w2': nrm(ks[20], (DEPTH, D_FF, D_MODEL), D_FF ** -0.5),
        'ple_w_proj': nrm(ks[21], (DEPTH, PLE_DIM, D_MODEL), PLE_DIM ** -0.5),
        'ple_w_gate': nrm(ks[22], (DEPTH, D_MODEL, D_MODEL), D_MODEL ** -0.5),
    }


def reference(x_prompt, x_sample, p_prompt, p_sample, cache_k, cache_v,
              g_mix, g_ffn, g_ple,
              attn_w_qkv, attn_q_norm, attn_k_norm, attn_sinks, attn_w_o,
              gmlp_w_uv, gmlp_v_norm, gmlp_w_s, gmlp_b_s, gmlp_w_out,
              ffn_w1, ffn_w2, ple_w_proj, ple_w_gate):
    weights = (g_mix, g_ffn, g_ple,
               attn_w_qkv, attn_q_norm, attn_k_norm, attn_sinks, attn_w_o,
               gmlp_w_uv, gmlp_v_norm, gmlp_w_s, gmlp_b_s, gmlp_w_out,
               ffn_w1, ffn_w2, ple_w_proj, ple_w_gate)
    y_prompt, new_k_prompt, new_v_prompt, _ = trunk(x_prompt, p_prompt, None, None, *weights)
    y_sample, new_k_sample, new_v_sample, new_gmlp_v_sample = trunk(x_sample, p_sample, cache_k, cache_v, *weights)
    return (y_prompt, y_sample, new_k_prompt, new_v_prompt, new_k_sample, new_v_sample, new_gmlp_v_sample)
```

```python
import functools
import math

import jax
import jax.numpy as jnp
from jax import lax
from jax.experimental import pallas as pl
from jax.experimental.pallas import tpu as pltpu

D_MODEL = 1024
HEAD_DIM = 64
N_HEADS = 16
N_KV_HEADS = 4
GQA_GROUP = N_HEADS // N_KV_HEADS
KV_DIM = N_KV_HEADS * HEAD_DIM
QK_DIM = (N_HEADS + N_KV_HEADS) * HEAD_DIM
QKV_DIM = (N_HEADS + 2 * N_KV_HEADS) * HEAD_DIM
CHUNK = 64
WINDOW = 128
GMLP_CHUNK = 128
GMLP_HALF = 3 * D_MODEL
GMLP_GROUPS = 8
GMLP_GROUP_DIM = GMLP_HALF // GMLP_GROUPS
D_FF = 4 * D_MODEL
PLE_DIM = 256
EPS = 1e-6
NEG_INF = -1e30

MXU_DIM_V7X = 256
VMEM_LIMIT_BYTES_V7X = 56 * 1024 * 1024

TOKEN_BLOCK = 512
ATTN_UNIT = 2 * CHUNK
ATTN_KEYS = 2 * WINDOW
FF_CHUNK = 1024
U_CHUNK = 2 * GMLP_GROUP_DIM

_BF16 = jnp.bfloat16
_F32 = jnp.float32


def _resident(shape):
    return pl.BlockSpec(shape, lambda *_: (0,) * len(shape), pipeline_mode=pl.Buffered(1))


def _params(n_axes):
    return pltpu.CompilerParams(
        dimension_semantics=("arbitrary",) * n_axes,
        vmem_limit_bytes=VMEM_LIMIT_BYTES_V7X,
    )


def _rms(x, g):
    ms = jnp.mean(x * x, axis=-1, keepdims=True)
    return (x * lax.rsqrt(ms + EPS)) * g


def _dot(a, b):
    return jnp.dot(a, b, preferred_element_type=_F32)


def _qkv_kernel(x_ref, g_ref, w_ref, qkg_ref, q_ref, k_ref, v_ref):
    n = _rms(x_ref[...], g_ref[...]).astype(_BF16)
    qkv = _dot(n, w_ref[...])
    r = lax.broadcasted_iota(jnp.int32, (MXU_DIM_V7X, MXU_DIM_V7X), 0) // HEAD_DIM
    c = lax.broadcasted_iota(jnp.int32, (MXU_DIM_V7X, MXU_DIM_V7X), 1) // HEAD_DIM
    seg = jnp.where(r == c, 1.0, 0.0).astype(_BF16)
    normed = []
    for j in range(QK_DIM // MXU_DIM_V7X):
        cols = slice(j * MXU_DIM_V7X, (j + 1) * MXU_DIM_V7X)
        t = qkv[:, cols]
        ss = _dot((t * t).astype(_BF16), seg)
        inv = lax.rsqrt(ss * (1.0 / HEAD_DIM) + EPS)
        normed.append((t * inv) * qkg_ref[:, cols])
    q_ref[...] = jnp.concatenate(normed[:-1], axis=1).astype(_BF16)
    k_ref[...] = normed[-1]
    v_ref[...] = qkv[:, QK_DIM:]


def _qkv_proj(x, g, w_qkv, qk_gain):
    n_tok = x.shape[0]
    tb = TOKEN_BLOCK
    row = lambda width: pl.BlockSpec((tb, width), lambda i: (i, 0))
    return pl.pallas_call(
        _qkv_kernel,
        grid=(n_tok // tb,),
        in_specs=[row(D_MODEL), _resident((1, D_MODEL)), _resident((D_MODEL, QKV_DIM)),
                  _resident((1, QK_DIM))],
        out_specs=[row(D_MODEL), row(KV_DIM), row(KV_DIM)],
        out_shape=[jax.ShapeDtypeStruct((n_tok, D_MODEL), _BF16),
                   jax.ShapeDtypeStruct((n_tok, KV_DIM), _F32),
                   jax.ShapeDtypeStruct((n_tok, KV_DIM), _F32)],
        compiler_params=_params(1),
        name="qkv_proj",
    )(x, g, w_qkv, qk_gain)


def _alibi_slope(head):
    return 2.0 ** (-8.0 * (head + 1) / N_HEADS)


def _fill_bias(bias_ref, variant, q_rows, valid_fn):
    t = lax.broadcasted_iota(jnp.int32, (q_rows, ATTN_KEYS), 0)
    s = lax.broadcasted_iota(jnp.int32, (q_rows, ATTN_KEYS), 1)
    dist = jnp.abs(t + WINDOW - s).astype(_F32)
    valid = valid_fn(t, s)
    for kvh in range(N_KV_HEADS):
        for j in range(GQA_GROUP):
            slope = _alibi_slope(kvh * GQA_GROUP + j)
            bias_ref[variant, kvh, j * q_rows:(j + 1) * q_rows, :] = jnp.where(
                valid, -slope * dist, NEG_INF)


def _attend(q_heads, k_h, v_h, bias, sink_col):
    q4 = jnp.concatenate(q_heads, axis=0)
    s = lax.dot_general(q4, k_h, (((1,), (1,)), ((), ())), preferred_element_type=_F32) + bias
    m = jnp.maximum(jnp.max(s, axis=-1, keepdims=True), sink_col)
    e = jnp.exp(s - m)
    den = jnp.sum(e, axis=-1, keepdims=True) + jnp.exp(sink_col - m)
    o = _dot(e.astype(_BF16), v_h)
    return o / den


def _sink_col(sinks_ref, kvh, q_rows):
    return jnp.concatenate(
        [jnp.full((q_rows, 1), sinks_ref[0, kvh * GQA_GROUP + j], _F32) for j in range(GQA_GROUP)],
        axis=0)


def _attn_prompt_kernel(sinks_ref, x_ref, q_ref, k_ref, kh_ref, v_ref, vh_ref, wo_ref,
                        out_ref, bias_ref, k_scr, v_scr, o_scr):
    b, i = pl.program_id(0), pl.program_id(1)

    @pl.when((b == 0) & (i == 0))
    def _():
        in_window = lambda t, s: ((t < CHUNK) & (s < WINDOW + CHUNK)) | ((t >= CHUNK) & (s >= CHUNK))
        _fill_bias(bias_ref, 0, ATTN_UNIT, in_window)
        _fill_bias(bias_ref, 1, ATTN_UNIT, lambda t, s: in_window(t, s) & (s >= WINDOW))

    k_scr[:WINDOW, :] = kh_ref[...].astype(_BF16)
    k_scr[WINDOW:, :] = k_ref[...].astype(_BF16)
    v_scr[:WINDOW, :] = vh_ref[...].astype(_BF16)
    v_scr[WINDOW:, :] = v_ref[...].astype(_BF16)

    def unit(u, carry):
        rows = pl.ds(pl.multiple_of(u * ATTN_UNIT, ATTN_UNIT), ATTN_UNIT)
        keys = pl.ds(pl.multiple_of(u * ATTN_UNIT, ATTN_UNIT), ATTN_KEYS)
        variant = jnp.where((i == 0) & (u == 0), 1, 0)
        q_u = q_ref[rows, :]
        k_u = k_scr[keys, :]
        v_u = v_scr[keys, :]
        for kvh in range(N_KV_HEADS):
            kv_cols = slice(kvh * HEAD_DIM, (kvh + 1) * HEAD_DIM)
            q_heads = [q_u[:, (kvh * GQA_GROUP + j) * HEAD_DIM:(kvh * GQA_GROUP + j + 1) * HEAD_DIM]
                       for j in range(GQA_GROUP)]
            o = _attend(q_heads, k_u[:, kv_cols], v_u[:, kv_cols], bias_ref[variant, kvh],
                        _sink_col(sinks_ref, kvh, ATTN_UNIT))
            for j in range(GQA_GROUP):
                head = kvh * GQA_GROUP + j
                o_scr[rows, head * HEAD_DIM:(head + 1) * HEAD_DIM] = (
                    o[j * ATTN_UNIT:(j + 1) * ATTN_UNIT, :].astype(_BF16))
        return carry

    lax.fori_loop(0, TOKEN_BLOCK // ATTN_UNIT, unit, 0)
    out_ref[...] = x_ref[...] + _dot(o_scr[...], wo_ref[...])


def _attn_prompt(sinks, x, q, k, v, w_o, batch, seq):
    tb = TOKEN_BLOCK
    nt = seq // tb
    halo_per_block = tb // WINDOW
    row = lambda width: pl.BlockSpec((tb, width), lambda b, i: (b * nt + i, 0))
    halo = pl.BlockSpec(
        (WINDOW, KV_DIM), lambda b, i: (jnp.maximum((b * nt + i) * halo_per_block - 1, 0), 0))
    return pl.pallas_call(
        _attn_prompt_kernel,
        grid=(batch, nt),
        in_specs=[pl.BlockSpec(memory_space=pltpu.SMEM),
                  row(D_MODEL), row(D_MODEL), row(KV_DIM), halo, row(KV_DIM), halo,
                  _resident((D_MODEL, D_MODEL))],
        out_specs=row(D_MODEL),
        out_shape=jax.ShapeDtypeStruct(x.shape, _F32),
        scratch_shapes=[
            pltpu.VMEM((2, N_KV_HEADS, GQA_GROUP * ATTN_UNIT, ATTN_KEYS), _F32),
            pltpu.VMEM((WINDOW + tb, KV_DIM), _BF16),
            pltpu.VMEM((WINDOW + tb, KV_DIM), _BF16),
            pltpu.VMEM((tb, D_MODEL), _BF16),
        ],
        compiler_params=_params(2),
        name="attn_prompt",
    )(sinks, x, q, k, k, v, v, w_o)


def _attn_sample_kernel(sinks_ref, x_ref, q_ref, k_ref, ck_ref, v_ref, cv_ref, wo_ref,
                        out_ref, bias_ref, k_scr, v_scr, o_scr, *, dec_seq, cache_rows):
    n_keys = cache_rows + dec_seq

    @pl.when(pl.program_id(0) == 0)
    def _():
        _fill_bias(bias_ref, 0, dec_seq, lambda t, s: s < n_keys)
        k_scr[...] = jnp.zeros_like(k_scr)
        v_scr[...] = jnp.zeros_like(v_scr)

    def one_batch(bb, carry):
        rows = pl.ds(pl.multiple_of(bb * dec_seq, dec_seq), dec_seq)
        cache = pl.ds(pl.multiple_of(bb * cache_rows, cache_rows), cache_rows)
        k_scr[:cache_rows, :] = ck_ref[cache, :].astype(_BF16)
        k_scr[cache_rows:n_keys, :] = k_ref[rows, :].astype(_BF16)
        v_scr[:cache_rows, :] = cv_ref[cache, :].astype(_BF16)
        v_scr[cache_rows:n_keys, :] = v_ref[rows, :].astype(_BF16)
        q_u = q_ref[rows, :]
        k_u = k_scr[...]
        v_u = v_scr[...]
        for kvh in range(N_KV_HEADS):
            kv_cols = slice(kvh * HEAD_DIM, (kvh + 1) * HEAD_DIM)
            q_heads = [q_u[:, (kvh * GQA_GROUP + j) * HEAD_DIM:(kvh * GQA_GROUP + j + 1) * HEAD_DIM]
                       for j in range(GQA_GROUP)]
            o = _attend(q_heads, k_u[:, kv_cols], v_u[:, kv_cols], bias_ref[0, kvh],
                        _sink_col(sinks_ref, kvh, dec_seq))
            for j in range(GQA_GROUP):
                head = kvh * GQA_GROUP + j
                o_scr[rows, head * HEAD_DIM:(head + 1) * HEAD_DIM] = (
                    o[j * dec_seq:(j + 1) * dec_seq, :].astype(_BF16))
        return carry

    lax.fori_loop(0, TOKEN_BLOCK // dec_seq, one_batch, 0)
    out_ref[...] = x_ref[...] + _dot(o_scr[...], wo_ref[...])


def _attn_sample(sinks, x, q, k, v, cache_k, cache_v, w_o, dec_seq, cache_rows):
    n_tok = x.shape[0]
    tb = TOKEN_BLOCK
    batches_per_block = tb // dec_seq
    row = lambda width: pl.BlockSpec((tb, width), lambda i: (i, 0))
    cache = pl.BlockSpec((batches_per_block * cache_rows, KV_DIM), lambda i: (i, 0))
    return pl.pallas_call(
        functools.partial(_attn_sample_kernel, dec_seq=dec_seq, cache_rows=cache_rows),
        grid=(n_tok // tb,),
        in_specs=[pl.BlockSpec(memory_space=pltpu.SMEM),
                  row(D_MODEL), row(D_MODEL), row(KV_DIM), cache, row(KV_DIM), cache,
                  _resident((D_MODEL, D_MODEL))],
        out_specs=row(D_MODEL),
        out_shape=jax.ShapeDtypeStruct(x.shape, _F32),
        scratch_shapes=[
            pltpu.VMEM((1, N_KV_HEADS, GQA_GROUP * dec_seq, ATTN_KEYS), _F32),
            pltpu.VMEM((ATTN_KEYS, KV_DIM), _BF16),
            pltpu.VMEM((ATTN_KEYS, KV_DIM), _BF16),
            pltpu.VMEM((tb, D_MODEL), _BF16),
        ],
        compiler_params=_params(1),
        name="attn_sample",
    )(sinks, x, q, k, cache_k, v, cache_v, w_o)


def _ffn_ple_kernel(h_ref, p_ref, gf_ref, w1_ref, w2_ref, gp_ref, wg_ref, wp_ref, out_ref):
    h = h_ref[...]
    n = _rms(h, gf_ref[...]).astype(_BF16)
    for c in range(D_FF // FF_CHUNK):
        cols = slice(c * FF_CHUNK, (c + 1) * FF_CHUNK)
        a = jnp.maximum(_dot(n, w1_ref[:, cols]), 0.0)
        h = h + _dot((a * a).astype(_BF16), w2_ref[cols, :])
    z = _dot(_rms(h, gp_ref[...]).astype(_BF16), wg_ref[...])
    gate = 1.0 / (1.0 + jnp.exp(-z))
    out_ref[...] = h + gate * _dot(p_ref[...].astype(_BF16), wp_ref[...])


def _ffn_ple(h, p, g_ffn, w1, w2, g_ple, w_gate, w_proj):
    n_tok = h.shape[0]
    tb = TOKEN_BLOCK
    row = lambda width: pl.BlockSpec((tb, width), lambda i: (i, 0))
    return pl.pallas_call(
        _ffn_ple_kernel,
        grid=(n_tok // tb,),
        in_specs=[row(D_MODEL), row(PLE_DIM), _resident((1, D_MODEL)),
                  _resident((D_MODEL, D_FF)), _resident((D_FF, D_MODEL)),
                  _resident((1, D_MODEL)), _resident((D_MODEL, D_MODEL)),
                  _resident((PLE_DIM, D_MODEL))],
        out_specs=row(D_MODEL),
        out_shape=jax.ShapeDtypeStruct(h.shape, _F32),
        compiler_params=_params(1),
        name="ffn_ple",
    )(h, p, g_ffn, w1, w2, g_ple, w_gate, w_proj)


def _gelu_tanh(x):
    c = math.sqrt(2.0 / math.pi)
    return x * (0.5 * (1.0 + jnp.tanh(c * (x + 0.044715 * (x * x * x)))))


def _gmlp_kernel(h_ref, g_ref, wuv_ref, vg_ref, ws_ref, bs_ref, wout_ref, out_ref, *rest,
                 emit_v):
    if emit_v:
        vout_ref, v_scr, vb_scr = rest
    else:
        v_scr, vb_scr = rest
    tb = h_ref.shape[0]
    h = h_ref[...]
    n = _rms(h, g_ref[...]).astype(_BF16)

    ssq = jnp.zeros((tb, 1), _F32)
    for c in range(GMLP_HALF // FF_CHUNK):
        cols = slice(c * FF_CHUNK, (c + 1) * FF_CHUNK)
        vc = _gelu_tanh(_dot(n, wuv_ref[:, GMLP_HALF + c * FF_CHUNK:GMLP_HALF + (c + 1) * FF_CHUNK]))
        v_scr[:, cols] = vc
        ssq = ssq + jnp.sum(vc * vc, axis=-1, keepdims=True)
    inv = lax.rsqrt(ssq * (1.0 / GMLP_HALF) + EPS)
    for c in range(GMLP_HALF // FF_CHUNK):
        cols = slice(c * FF_CHUNK, (c + 1) * FF_CHUNK)
        vn = (v_scr[:, cols] * inv) * vg_ref[:, cols]
        if emit_v:
            vout_ref[:, cols] = vn
        vb_scr[:, cols] = vn.astype(_BF16)

    groups_per_step = U_CHUNK // GMLP_GROUP_DIM
    for c in range(GMLP_HALF // U_CHUNK):
        u = _gelu_tanh(_dot(n, wuv_ref[:, c * U_CHUNK:(c + 1) * U_CHUNK]))
        gated_rows = []
        for r in range(tb // GMLP_CHUNK):
            rows = slice(r * GMLP_CHUNK, (r + 1) * GMLP_CHUNK)
            parts = []
            for gg in range(groups_per_step):
                grp = c * groups_per_step + gg
                cols = slice(grp * GMLP_GROUP_DIM, (grp + 1) * GMLP_GROUP_DIM)
                s = _dot(ws_ref[grp], vb_scr[rows, cols]) + bs_ref[:, grp:grp + 1]
                parts.append(u[rows, gg * GMLP_GROUP_DIM:(gg + 1) * GMLP_GROUP_DIM] * s)
            gated_rows.append(jnp.concatenate(parts, axis=1))
        gated = jnp.concatenate(gated_rows, axis=0).astype(_BF16)
        h = h + _dot(gated, wout_ref[c * U_CHUNK:(c + 1) * U_CHUNK, :])
    out_ref[...] = h


def _gmlp(h, g, w_uv, v_gain, ws, bs_t, w_out, emit_v):
    n_tok = h.shape[0]
    tb = TOKEN_BLOCK
    row = lambda width: pl.BlockSpec((tb, width), lambda i: (i, 0))
    out_specs = [row(D_MODEL)]
    out_shape = [jax.ShapeDtypeStruct(h.shape, _F32)]
    if emit_v:
        out_specs.append(row(GMLP_HALF))
        out_shape.append(jax.ShapeDtypeStruct((n_tok, GMLP_HALF), _F32))
    return pl.pallas_call(
        functools.partial(_gmlp_kernel, emit_v=emit_v),
        grid=(n_tok // tb,),
        in_specs=[row(D_MODEL), _resident((1, D_MODEL)), _resident((D_MODEL, 2 * GMLP_HALF)),
                  _resident((1, GMLP_HALF)),
                  _resident((GMLP_GROUPS, GMLP_CHUNK, GMLP_CHUNK)),
                  _resident((GMLP_CHUNK, GMLP_GROUPS)),
                  _resident((GMLP_HALF, D_MODEL))],
        out_specs=out_specs,
        out_shape=out_shape,
        scratch_shapes=[pltpu.VMEM((tb, GMLP_HALF), _F32), pltpu.VMEM((tb, GMLP_HALF), _BF16)],
        compiler_params=_params(1),
        name="gmlp_v" if emit_v else "gmlp",
    )(h, g, w_uv, v_gain, ws, bs_t, w_out)


def _gmlp_spatial_weights(w_s, b_s, length):
    tril = jnp.tril(jnp.ones((length, length), w_s.dtype))
    ws = w_s[:, :length, :length] * tril
    reps = GMLP_CHUNK // length
    if reps > 1:
        eye = jnp.eye(reps, dtype=w_s.dtype)
        ws = jnp.einsum("ab,gij->gaibj", eye, ws).reshape(GMLP_GROUPS, GMLP_CHUNK, GMLP_CHUNK)
    bs_t = jnp.tile(b_s[:, :length].T, (reps, 1))
    return ws.astype(_BF16), bs_t


def _trunk(x, p, cache, w, batch, seq):
    q, k, v = _qkv_proj(x, w["g_mix"][0], w["w_qkv"], w["qk_gain"])
    if cache is None:
        h = _attn_prompt(w["sinks"], x, q, k, v, w["w_o"], batch, seq)
    else:
        h = _attn_sample(w["sinks"], x, q, k, v, cache[0], cache[1], w["w_o"], seq,
                         cache[0].shape[0] // batch)
    h = _ffn_ple(h, p[0], w["g_ffn"][0], w["w1"][0], w["w2"][0], w["g_ple"][0],
                 w["w_gate"][0], w["w_proj"][0])
    length = min(seq, GMLP_CHUNK)
    ws, bs_t = _gmlp_spatial_weights(w["w_s"], w["b_s"], length)
    emit_v = cache is not None
    res = _gmlp(h, w["g_mix"][1], w["w_uv"], w["v_gain"], ws, bs_t, w["w_out"], emit_v)
    h = res[0]
    v_rows = res[1] if emit_v else None
    h = _ffn_ple(h, p[1], w["g_ffn"][1], w["w1"][1], w["w2"][1], w["g_ple"][1],
                 w["w_gate"][1], w["w_proj"][1])
    return h, k, v, v_rows


def kernel(x_prompt, x_sample, p_prompt, p_sample, cache_k, cache_v, g_mix, g_ffn, g_ple,
           attn_w_qkv, attn_q_norm, attn_k_norm, attn_sinks, attn_w_o, gmlp_w_uv, gmlp_v_norm,
           gmlp_w_s, gmlp_b_s, gmlp_w_out, ffn_w1, ffn_w2, ple_w_proj, ple_w_gate):
    batch, seq, _ = x_prompt.shape
    dec_batch, dec_seq, _ = x_sample.shape
    depth = g_mix.shape[0]
    cache_rows = cache_k.shape[2]
    assert depth == 2 and attn_w_qkv.shape[0] == 1 and gmlp_w_uv.shape[0] == 1
    assert seq % TOKEN_BLOCK == 0 and (dec_batch * dec_seq) % TOKEN_BLOCK == 0
    assert TOKEN_BLOCK % dec_seq == 0 and GMLP_CHUNK % dec_seq == 0 and dec_seq % 16 == 0
    assert cache_rows == WINDOW and cache_rows + dec_seq <= ATTN_KEYS

    scale = HEAD_DIM ** -0.5
    w = {
        "g_mix": g_mix.reshape(depth, 1, D_MODEL),
        "g_ffn": g_ffn.reshape(depth, 1, D_MODEL),
        "g_ple": g_ple.reshape(depth, 1, D_MODEL),
        "w_qkv": attn_w_qkv[0].astype(_BF16),
        "qk_gain": jnp.concatenate([jnp.tile(attn_q_norm[0] * scale, N_HEADS),
                                    jnp.tile(attn_k_norm[0], N_KV_HEADS)]).reshape(1, QK_DIM),
        "sinks": attn_sinks[0].reshape(1, N_HEADS),
        "w_o": attn_w_o[0].astype(_BF16),
        "w_uv": gmlp_w_uv[0].astype(_BF16),
        "v_gain": gmlp_v_norm[0].reshape(1, GMLP_HALF),
        "w_s": gmlp_w_s[0],
        "b_s": gmlp_b_s[0],
        "w_out": gmlp_w_out[0].astype(_BF16),
        "w1": ffn_w1.astype(_BF16),
        "w2": ffn_w2.astype(_BF16),
        "w_gate": ple_w_gate.astype(_BF16),
        "w_proj": ple_w_proj.astype(_BF16),
    }

    n_prompt = batch * seq
    n_sample = dec_batch * dec_seq
    y_p, k_p, v_p, _ = _trunk(x_prompt.reshape(n_prompt, D_MODEL),
                              p_prompt.reshape(depth, n_prompt, PLE_DIM), None, w, batch, seq)
    cache = (cache_k[0].reshape(dec_batch * cache_rows, KV_DIM),
             cache_v[0].reshape(dec_batch * cache_rows, KV_DIM))
    y_s, k_s, v_s, vrows = _trunk(x_sample.reshape(n_sample, D_MODEL),
                                  p_sample.reshape(depth, n_sample, PLE_DIM), cache, w,
                                  dec_batch, dec_seq)

    heads = (N_KV_HEADS, HEAD_DIM)
    new_k_p = k_p.reshape(batch, seq, *heads)[None, :, seq - WINDOW:]
    new_v_p = v_p.reshape(batch, seq, *heads)[None, :, seq - WINDOW:]
    new_k_s = jnp.concatenate(
        [cache_k[:, :, dec_seq:], k_s.reshape(1, dec_batch, dec_seq, *heads)], axis=2)
    new_v_s = jnp.concatenate(
        [cache_v[:, :, dec_seq:], v_s.reshape(1, dec_batch, dec_seq, *heads)], axis=2)
    return (y_p.reshape(batch, seq, D_MODEL), y_s.reshape(dec_batch, dec_seq, D_MODEL),
            new_k_p, new_v_p, new_k_s, new_v_s,
            vrows.reshape(1, dec_batch, dec_seq, GMLP_HALF))
```

```python
import functools
import math

import jax
import jax.numpy as jnp
from jax import lax
from jax.experimental import pallas as pl
from jax.experimental.pallas import tpu as pltpu

D_MODEL = 1024
HEAD_DIM = 64
N_HEADS = 16
N_KV_HEADS = 4
GQA_GROUP = N_HEADS // N_KV_HEADS
KV_DIM = N_KV_HEADS * HEAD_DIM
CHUNK = 64
WINDOW = 128
GMLP_CHUNK = 128
GMLP_HALF = 3 * D_MODEL
GMLP_GROUPS = 8
GMLP_GROUP_DIM = GMLP_HALF // GMLP_GROUPS
D_FF = 4 * D_MODEL
PLE_DIM = 256
EPS = 1e-6
NEG_INF = -1e30

LANES = 128
MXU_DIM_V7X = 256
BF16_SUBLANES = 16
VMEM_LIMIT_BYTES_V7X = 56 * 1024 * 1024

KDUP_DIM = 2 * KV_DIM
QK_DIM = D_MODEL + KDUP_DIM
QKV_DIM = QK_DIM + KV_DIM

TOKEN_BLOCK = 512
ATTN_UNIT = 2 * CHUNK
ATTN_KEYS = 2 * WINDOW
PV_ROWS = HEAD_DIM + BF16_SUBLANES
FF_CHUNK = 1024
U_CHUNK = 2 * GMLP_GROUP_DIM

_BF16 = jnp.bfloat16
_F32 = jnp.float32
_CONTRACT_LAST = (((1,), (1,)), ((), ()))


def _resident(shape):
    return pl.BlockSpec(shape, lambda *_: (0,) * len(shape), pipeline_mode=pl.Buffered(1))


def _params(n_axes):
    return pltpu.CompilerParams(
        dimension_semantics=("arbitrary",) * n_axes,
        vmem_limit_bytes=VMEM_LIMIT_BYTES_V7X,
    )


def _rms(x, g):
    ms = jnp.mean(x * x, axis=-1, keepdims=True)
    return (x * lax.rsqrt(ms + EPS)) * g


def _dot(a, b):
    return jnp.dot(a, b, preferred_element_type=_F32)


def _qkv_kernel(x_ref, g_ref, w_ref, qkg_ref, q_ref, k_ref, v_ref):
    n = _rms(x_ref[...], g_ref[...]).astype(_BF16)
    qkv = _dot(n, w_ref[...])
    r = lax.broadcasted_iota(jnp.int32, (MXU_DIM_V7X, MXU_DIM_V7X), 0) // HEAD_DIM
    c = lax.broadcasted_iota(jnp.int32, (MXU_DIM_V7X, MXU_DIM_V7X), 1) // HEAD_DIM
    seg = jnp.where(r == c, 1.0, 0.0).astype(_BF16)
    normed = []
    for j in range(QK_DIM // MXU_DIM_V7X):
        cols = slice(j * MXU_DIM_V7X, (j + 1) * MXU_DIM_V7X)
        t = qkv[:, cols]
        ss = _dot((t * t).astype(_BF16), seg)
        inv = lax.rsqrt(ss * (1.0 / HEAD_DIM) + EPS)
        normed.append((t * inv) * qkg_ref[:, cols])
    n_q = D_MODEL // MXU_DIM_V7X
    q_ref[...] = jnp.concatenate(normed[:n_q], axis=1).astype(_BF16)
    k_ref[...] = jnp.concatenate(normed[n_q:], axis=1)
    v_ref[...] = qkv[:, QK_DIM:]


def _qkv_proj(x, g, w_qkv, qk_gain):
    n_tok = x.shape[0]
    tb = TOKEN_BLOCK
    row = lambda width: pl.BlockSpec((tb, width), lambda i: (i, 0))
    return pl.pallas_call(
        _qkv_kernel,
        grid=(n_tok // tb,),
        in_specs=[row(D_MODEL), _resident((1, D_MODEL)), _resident((D_MODEL, QKV_DIM)),
                  _resident((1, QK_DIM))],
        out_specs=[row(D_MODEL), row(KDUP_DIM), row(KV_DIM)],
        out_shape=[jax.ShapeDtypeStruct((n_tok, D_MODEL), _BF16),
                   jax.ShapeDtypeStruct((n_tok, KDUP_DIM), _F32),
                   jax.ShapeDtypeStruct((n_tok, KV_DIM), _F32)],
        compiler_params=_params(1),
        name="qkv_proj",
    )(x, g, w_qkv, qk_gain)


def _alibi_slope(head):
    return 2.0 ** (-8.0 * (head + 1) / N_HEADS)


def _fill_bias_t(bias_ref, variant, valid_fn):
    s = lax.broadcasted_iota(jnp.int32, (ATTN_KEYS, ATTN_UNIT), 0)
    t = lax.broadcasted_iota(jnp.int32, (ATTN_KEYS, ATTN_UNIT), 1)
    dist = jnp.abs(t + WINDOW - s).astype(_F32)
    valid = valid_fn(t, s)
    for kvh in range(N_KV_HEADS):
        for p in range(2):
            for g in range(2):
                slope = _alibi_slope(kvh * GQA_GROUP + 2 * g + p)
                bias_ref[variant, kvh, p * ATTN_KEYS:(p + 1) * ATTN_KEYS,
                         g * ATTN_UNIT:(g + 1) * ATTN_UNIT] = jnp.where(valid, -slope * dist, NEG_INF)


def _attn_prompt_kernel(sinks_ref, x_ref, q_ref, k_ref, kh_ref, v_ref, vh_ref, wo_ref,
                        out_ref, bias_ref, klo_scr, khi_scr, vt_scr, ot_scr):
    b, i = pl.program_id(0), pl.program_id(1)
    tb = TOKEN_BLOCK

    @pl.when((b == 0) & (i == 0))
    def _():
        in_window = lambda t, s: ((t < CHUNK) & (s < WINDOW + CHUNK)) | ((t >= CHUNK) & (s >= CHUNK))
        _fill_bias_t(bias_ref, 0, in_window)
        _fill_bias_t(bias_ref, 1, lambda t, s: in_window(t, s) & (s >= WINDOW))
        vt_scr[:, HEAD_DIM:, :] = jnp.ones((N_KV_HEADS, PV_ROWS - HEAD_DIM, WINDOW + tb), _BF16)

    kd = jnp.concatenate([kh_ref[...], k_ref[...]], axis=0)
    low_half = lax.broadcasted_iota(jnp.int32, (WINDOW + tb, LANES), 1) < HEAD_DIM
    for kvh in range(N_KV_HEADS):
        kk = kd[:, kvh * LANES:(kvh + 1) * LANES]
        klo_scr[kvh] = jnp.where(low_half, kk, 0.0).astype(_BF16)
        khi_scr[kvh] = jnp.where(low_half, 0.0, kk).astype(_BF16)
    vt = jnp.concatenate([vh_ref[...], v_ref[...]], axis=0).T
    for kvh in range(N_KV_HEADS):
        vt_scr[kvh, :HEAD_DIM, :] = vt[kvh * HEAD_DIM:(kvh + 1) * HEAD_DIM, :].astype(_BF16)

    first_in_seq = jnp.where(i == 0, 1, 0)
    lane2 = lax.broadcasted_iota(jnp.int32, (1, 2 * ATTN_UNIT), 1)
    for u in range(tb // ATTN_UNIT):
        rows = slice(u * ATTN_UNIT, (u + 1) * ATTN_UNIT)
        keys = slice(u * ATTN_UNIT, u * ATTN_UNIT + ATTN_KEYS)
        variant = first_in_seq if u == 0 else 0
        for kvh in range(N_KV_HEADS):
            k2 = jnp.concatenate([klo_scr[kvh, keys, :], khi_scr[kvh, keys, :]], axis=0)
            q2 = jnp.concatenate(
                [q_ref[rows, (2 * kvh + g) * LANES:(2 * kvh + g + 1) * LANES] for g in range(2)],
                axis=0)
            s_t = lax.dot_general(k2, q2, _CONTRACT_LAST, preferred_element_type=_F32)
            s_t = s_t + bias_ref[variant, kvh]
            v1 = vt_scr[kvh, :, keys]
            for p in range(2):
                head0 = kvh * GQA_GROUP + p
                sink = jnp.where(lane2 < ATTN_UNIT, sinks_ref[0, head0], sinks_ref[0, head0 + 2])
                sp = s_t[p * ATTN_KEYS:(p + 1) * ATTN_KEYS, :]
                m = jnp.maximum(jnp.max(sp, axis=0, keepdims=True), sink)
                e = jnp.exp(sp - m)
                r = _dot(v1, e.astype(_BF16))
                den = r[HEAD_DIM:HEAD_DIM + 1, :] + jnp.exp(sink - m)
                o = r[:HEAD_DIM, :] * (1.0 / den)
                for g in range(2):
                    head = head0 + 2 * g
                    ot_scr[head * HEAD_DIM:(head + 1) * HEAD_DIM, rows] = (
                        o[:, g * ATTN_UNIT:(g + 1) * ATTN_UNIT].astype(_BF16))

    out_ref[...] = x_ref[...] + _dot(ot_scr[...].T, wo_ref[...])


def _attn_prompt(sinks, x, q, kdup, v, w_o, batch, seq):
    tb = TOKEN_BLOCK
    nt = seq // tb
    halo_per_block = tb // WINDOW
    row = lambda width: pl.BlockSpec((tb, width), lambda b, i: (b * nt + i, 0))
    halo = lambda width: pl.BlockSpec(
        (WINDOW, width), lambda b, i: (jnp.maximum((b * nt + i) * halo_per_block - 1, 0), 0))
    return pl.pallas_call(
        _attn_prompt_kernel,
        grid=(batch, nt),
        in_specs=[pl.BlockSpec(memory_space=pltpu.SMEM),
                  row(D_MODEL), row(D_MODEL), row(KDUP_DIM), halo(KDUP_DIM), row(KV_DIM),
                  halo(KV_DIM), _resident((D_MODEL, D_MODEL))],
        out_specs=row(D_MODEL),
        out_shape=jax.ShapeDtypeStruct(x.shape, _F32),
        scratch_shapes=[
            pltpu.VMEM((2, N_KV_HEADS, 2 * ATTN_KEYS, 2 * ATTN_UNIT), _F32),
            pltpu.VMEM((N_KV_HEADS, WINDOW + tb, LANES), _BF16),
            pltpu.VMEM((N_KV_HEADS, WINDOW + tb, LANES), _BF16),
            pltpu.VMEM((N_KV_HEADS, PV_ROWS, WINDOW + tb), _BF16),
            pltpu.VMEM((D_MODEL, tb), _BF16),
        ],
        compiler_params=_params(2),
        name="attn_prompt",
    )(sinks, x, q, kdup, kdup, v, v, w_o)


def _fill_bias(bias_ref, q_rows, n_keys):
    t = lax.broadcasted_iota(jnp.int32, (q_rows, ATTN_KEYS), 0)
    s = lax.broadcasted_iota(jnp.int32, (q_rows, ATTN_KEYS), 1)
    dist = jnp.abs(t + WINDOW - s).astype(_F32)
    for kvh in range(N_KV_HEADS):
        for j in range(GQA_GROUP):
            slope = _alibi_slope(kvh * GQA_GROUP + j)
            bias_ref[kvh, j * q_rows:(j + 1) * q_rows, :] = jnp.where(
                s < n_keys, -slope * dist, NEG_INF)


def _attn_sample_kernel(sinks_ref, x_ref, q_ref, k_ref, ck_ref, v_ref, cv_ref, wo_ref,
                        out_ref, bias_ref, k_scr, v_scr, o_scr, *, dec_seq, cache_rows):
    n_keys = cache_rows + dec_seq

    @pl.when(pl.program_id(0) == 0)
    def _():
        _fill_bias(bias_ref, dec_seq, n_keys)
        k_scr[...] = jnp.zeros_like(k_scr)
        v_scr[...] = jnp.zeros_like(v_scr)

    def one_batch(bb, carry):
        rows = pl.ds(pl.multiple_of(bb * dec_seq, dec_seq), dec_seq)
        cache = pl.ds(pl.multiple_of(bb * cache_rows, cache_rows), cache_rows)
        k_scr[:cache_rows, :] = ck_ref[cache, :].astype(_BF16)
        k_new = k_ref[rows, :]
        for kvh in range(N_KV_HEADS):
            k_scr[cache_rows:n_keys, kvh * HEAD_DIM:(kvh + 1) * HEAD_DIM] = (
                k_new[:, kvh * LANES:kvh * LANES + HEAD_DIM].astype(_BF16))
        v_scr[:cache_rows, :] = cv_ref[cache, :].astype(_BF16)
        v_scr[cache_rows:n_keys, :] = v_ref[rows, :].astype(_BF16)
        q_u = q_ref[rows, :]
        k_u = k_scr[...]
        v_u = v_scr[...]
        for kvh in range(N_KV_HEADS):
            kv_cols = slice(kvh * HEAD_DIM, (kvh + 1) * HEAD_DIM)
            q4 = jnp.concatenate(
                [q_u[:, (kvh * GQA_GROUP + j) * HEAD_DIM:(kvh * GQA_GROUP + j + 1) * HEAD_DIM]
                 for j in range(GQA_GROUP)], axis=0)
            sink = jnp.concatenate(
                [jnp.full((dec_seq, 1), sinks_ref[0, kvh * GQA_GROUP + j], _F32)
                 for j in range(GQA_GROUP)], axis=0)
            s = lax.dot_general(q4, k_u[:, kv_cols], _CONTRACT_LAST, preferred_element_type=_F32)
            s = s + bias_ref[kvh]
            m = jnp.maximum(jnp.max(s, axis=-1, keepdims=True), sink)
            e = jnp.exp(s - m)
            den = jnp.sum(e, axis=-1, keepdims=True) + jnp.exp(sink - m)
            o = _dot(e.astype(_BF16), v_u[:, kv_cols]) / den
            for j in range(GQA_GROUP):
                head = kvh * GQA_GROUP + j
                o_scr[rows, head * HEAD_DIM:(head + 1) * HEAD_DIM] = (
                    o[j * dec_seq:(j + 1) * dec_seq, :].astype(_BF16))
        return carry

    lax.fori_loop(0, TOKEN_BLOCK // dec_seq, one_batch, 0)
    out_ref[...] = x_ref[...] + _dot(o_scr[...], wo_ref[...])


def _attn_sample(sinks, x, q, kdup, v, cache_k, cache_v, w_o, dec_seq, cache_rows):
    n_tok = x.shape[0]
    tb = TOKEN_BLOCK
    batches_per_block = tb // dec_seq
    row = lambda width: pl.BlockSpec((tb, width), lambda i: (i, 0))
    cache = pl.BlockSpec((batches_per_block * cache_rows, KV_DIM), lambda i: (i, 0))
    return pl.pallas_call(
        functools.partial(_attn_sample_kernel, dec_seq=dec_seq, cache_rows=cache_rows),
        grid=(n_tok // tb,),
        in_specs=[pl.BlockSpec(memory_space=pltpu.SMEM),
                  row(D_MODEL), row(D_MODEL), row(KDUP_DIM), cache, row(KV_DIM), cache,
                  _resident((D_MODEL, D_MODEL))],
        out_specs=row(D_MODEL),
        out_shape=jax.ShapeDtypeStruct(x.shape, _F32),
        scratch_shapes=[
            pltpu.VMEM((N_KV_HEADS, GQA_GROUP * dec_seq, ATTN_KEYS), _F32),
            pltpu.VMEM((ATTN_KEYS, KV_DIM), _BF16),
            pltpu.VMEM((ATTN_KEYS, KV_DIM), _BF16),
            pltpu.VMEM((tb, D_MODEL), _BF16),
        ],
        compiler_params=_params(1),
        name="attn_sample",
    )(sinks, x, q, kdup, cache_k, v, cache_v, w_o)


def _ffn_ple_kernel(h_ref, p_ref, gf_ref, w1_ref, w2_ref, gp_ref, wg_ref, wp_ref, out_ref):
    h = h_ref[...]
    n = _rms(h, gf_ref[...]).astype(_BF16)
    for c in range(D_FF // FF_CHUNK):
        cols = slice(c * FF_CHUNK, (c + 1) * FF_CHUNK)
        a = jnp.maximum(_dot(n, w1_ref[:, cols]), 0.0)
        h = h + _dot((a * a).astype(_BF16), w2_ref[cols, :])
    z = _dot(_rms(h, gp_ref[...]).astype(_BF16), wg_ref[...])
    gate = 1.0 / (1.0 + jnp.exp(-z))
    out_ref[...] = h + gate * _dot(p_ref[...].astype(_BF16), wp_ref[...])


def _ffn_ple(h, p, g_ffn, w1, w2, g_ple, w_gate, w_proj):
    n_tok = h.shape[0]
    tb = TOKEN_BLOCK
    row = lambda width: pl.BlockSpec((tb, width), lambda i: (i, 0))
    return pl.pallas_call(
        _ffn_ple_kernel,
        grid=(n_tok // tb,),
        in_specs=[row(D_MODEL), row(PLE_DIM), _resident((1, D_MODEL)),
                  _resident((D_MODEL, D_FF)), _resident((D_FF, D_MODEL)),
                  _resident((1, D_MODEL)), _resident((D_MODEL, D_MODEL)),
                  _resident((PLE_DIM, D_MODEL))],
        out_specs=row(D_MODEL),
        out_shape=jax.ShapeDtypeStruct(h.shape, _F32),
        compiler_params=_params(1),
        name="ffn_ple",
    )(h, p, g_ffn, w1, w2, g_ple, w_gate, w_proj)


def _gelu_tanh(x):
    c = math.sqrt(2.0 / math.pi)
    return x * (0.5 * (1.0 + jnp.tanh(c * (x + 0.044715 * (x * x * x)))))


def _gmlp_kernel(h_ref, g_ref, wuv_ref, vg_ref, ws_ref, bs_ref, wout_ref, out_ref, *rest,
                 emit_v):
    if emit_v:
        vout_ref, v_scr, vb_scr = rest
    else:
        v_scr, vb_scr = rest
    tb = h_ref.shape[0]
    h = h_ref[...]
    n = _rms(h, g_ref[...]).astype(_BF16)

    ssq = jnp.zeros((tb, 1), _F32)
    for c in range(GMLP_HALF // FF_CHUNK):
        cols = slice(c * FF_CHUNK, (c + 1) * FF_CHUNK)
        vc = _gelu_tanh(_dot(n, wuv_ref[:, GMLP_HALF + c * FF_CHUNK:GMLP_HALF + (c + 1) * FF_CHUNK]))
        v_scr[:, cols] = vc
        ssq = ssq + jnp.sum(vc * vc, axis=-1, keepdims=True)
    inv = lax.rsqrt(ssq * (1.0 / GMLP_HALF) + EPS)
    for c in range(GMLP_HALF // FF_CHUNK):
        cols = slice(c * FF_CHUNK, (c + 1) * FF_CHUNK)
        vn = (v_scr[:, cols] * inv) * vg_ref[:, cols]
        if emit_v:
            vout_ref[:, cols] = vn
        vb_scr[:, cols] = vn.astype(_BF16)

    groups_per_step = U_CHUNK // GMLP_GROUP_DIM
    for c in range(GMLP_HALF // U_CHUNK):
        u = _gelu_tanh(_dot(n, wuv_ref[:, c * U_CHUNK:(c + 1) * U_CHUNK]))
        gated_rows = []
        for r in range(tb // GMLP_CHUNK):
            rows = slice(r * GMLP_CHUNK, (r + 1) * GMLP_CHUNK)
            parts = []
            for gg in range(groups_per_step):
                grp = c * groups_per_step + gg
                cols = slice(grp * GMLP_GROUP_DIM, (grp + 1) * GMLP_GROUP_DIM)
                s = _dot(ws_ref[grp], vb_scr[rows, cols]) + bs_ref[:, grp:grp + 1]
                parts.append(u[rows, gg * GMLP_GROUP_DIM:(gg + 1) * GMLP_GROUP_DIM] * s)
            gated_rows.append(jnp.concatenate(parts, axis=1))
        gated = jnp.concatenate(gated_rows, axis=0).astype(_BF16)
        h = h + _dot(gated, wout_ref[c * U_CHUNK:(c + 1) * U_CHUNK, :])
    out_ref[...] = h


def _gmlp(h, g, w_uv, v_gain, ws, bs_t, w_out, emit_v):
    n_tok = h.shape[0]
    tb = TOKEN_BLOCK
    row = lambda width: pl.BlockSpec((tb, width), lambda i: (i, 0))
    out_specs = [row(D_MODEL)]
    out_shape = [jax.ShapeDtypeStruct(h.shape, _F32)]
    if emit_v:
        out_specs.append(row(GMLP_HALF))
        out_shape.append(jax.ShapeDtypeStruct((n_tok, GMLP_HALF), _F32))
    return pl.pallas_call(
        functools.partial(_gmlp_kernel, emit_v=emit_v),
        grid=(n_tok // tb,),
        in_specs=[row(D_MODEL), _resident((1, D_MODEL)), _resident((D_MODEL, 2 * GMLP_HALF)),
                  _resident((1, GMLP_HALF)),
                  _resident((GMLP_GROUPS, GMLP_CHUNK, GMLP_CHUNK)),
                  _resident((GMLP_CHUNK, GMLP_GROUPS)),
                  _resident((GMLP_HALF, D_MODEL))],
        out_specs=out_specs,
        out_shape=out_shape,
        scratch_shapes=[pltpu.VMEM((tb, GMLP_HALF), _F32), pltpu.VMEM((tb, GMLP_HALF), _BF16)],
        compiler_params=_params(1),
        name="gmlp_v" if emit_v else "gmlp",
    )(h, g, w_uv, v_gain, ws, bs_t, w_out)


def _gmlp_spatial_weights(w_s, b_s, length):
    tril = jnp.tril(jnp.ones((length, length), w_s.dtype))
    ws = w_s[:, :length, :length] * tril
    reps = GMLP_CHUNK // length
    if reps > 1:
        eye = jnp.eye(reps, dtype=w_s.dtype)
        ws = jnp.einsum("ab,gij->gaibj", eye, ws).reshape(GMLP_GROUPS, GMLP_CHUNK, GMLP_CHUNK)
    bs_t = jnp.tile(b_s[:, :length].T, (reps, 1))
    return ws.astype(_BF16), bs_t


def _first_copy(kdup):
    return kdup.reshape(kdup.shape[0], N_KV_HEADS, 2, HEAD_DIM)[:, :, 0, :]


def _trunk(x, p, cache, w, batch, seq):
    q, kdup, v = _qkv_proj(x, w["g_mix"][0], w["w_qkv"], w["qk_gain"])
    if cache is None:
        h = _attn_prompt(w["sinks"], x, q, kdup, v, w["w_o"], batch, seq)
    else:
        h = _attn_sample(w["sinks"], x, q, kdup, v, cache[0], cache[1], w["w_o"], seq,
                         cache[0].shape[0] // batch)
    h = _ffn_ple(h, p[0], w["g_ffn"][0], w["w1"][0], w["w2"][0], w["g_ple"][0],
                 w["w_gate"][0], w["w_proj"][0])
    length = min(seq, GMLP_CHUNK)
    ws, bs_t = _gmlp_spatial_weights(w["w_s"], w["b_s"], length)
    emit_v = cache is not None
    res = _gmlp(h, w["g_mix"][1], w["w_uv"], w["v_gain"], ws, bs_t, w["w_out"], emit_v)
    h = res[0]
    v_rows = res[1] if emit_v else None
    h = _ffn_ple(h, p[1], w["g_ffn"][1], w["w1"][1], w["w2"][1], w["g_ple"][1],
                 w["w_gate"][1], w["w_proj"][1])
    return h, kdup, v, v_rows


def kernel(x_prompt, x_sample, p_prompt, p_sample, cache_k, cache_v, g_mix, g_ffn, g_ple,
           attn_w_qkv, attn_q_norm, attn_k_norm, attn_sinks, attn_w_o, gmlp_w_uv, gmlp_v_norm,
           gmlp_w_s, gmlp_b_s, gmlp_w_out, ffn_w1, ffn_w2, ple_w_proj, ple_w_gate):
    batch, seq, _ = x_prompt.shape
    dec_batch, dec_seq, _ = x_sample.shape
    depth = g_mix.shape[0]
    cache_rows = cache_k.shape[2]
    assert depth == 2 and attn_w_qkv.shape[0] == 1 and gmlp_w_uv.shape[0] == 1
    assert seq % TOKEN_BLOCK == 0 and (dec_batch * dec_seq) % TOKEN_BLOCK == 0
    assert TOKEN_BLOCK % dec_seq == 0 and GMLP_CHUNK % dec_seq == 0 and dec_seq % 16 == 0
    assert cache_rows == WINDOW and cache_rows + dec_seq <= ATTN_KEYS

    scale = HEAD_DIM ** -0.5
    w_q = attn_w_qkv[0][:, :D_MODEL]
    w_k = attn_w_qkv[0][:, D_MODEL:D_MODEL + KV_DIM].reshape(D_MODEL, N_KV_HEADS, 1, HEAD_DIM)
    w_kdup = jnp.broadcast_to(w_k, (D_MODEL, N_KV_HEADS, 2, HEAD_DIM)).reshape(D_MODEL, KDUP_DIM)
    w_v = attn_w_qkv[0][:, D_MODEL + KV_DIM:]
    w = {
        "g_mix": g_mix.reshape(depth, 1, D_MODEL),
        "g_ffn": g_ffn.reshape(depth, 1, D_MODEL),
        "g_ple": g_ple.reshape(depth, 1, D_MODEL),
        "w_qkv": jnp.concatenate([w_q, w_kdup, w_v], axis=1).astype(_BF16),
        "qk_gain": jnp.concatenate([jnp.tile(attn_q_norm[0] * scale, N_HEADS),
                                    jnp.tile(attn_k_norm[0], 2 * N_KV_HEADS)]).reshape(1, QK_DIM),
        "sinks": attn_sinks[0].reshape(1, N_HEADS),
        "w_o": attn_w_o[0].astype(_BF16),
        "w_uv": gmlp_w_uv[0].astype(_BF16),
        "v_gain": gmlp_v_norm[0].reshape(1, GMLP_HALF),
        "w_s": gmlp_w_s[0],
        "b_s": gmlp_b_s[0],
        "w_out": gmlp_w_out[0].astype(_BF16),
        "w1": ffn_w1.astype(_BF16),
        "w2": ffn_w2.astype(_BF16),
        "w_gate": ple_w_gate.astype(_BF16),
        "w_proj": ple_w_proj.astype(_BF16),
    }

    n_prompt = batch * seq
    n_sample = dec_batch * dec_seq
    y_p, kdup_p, v_p, _ = _trunk(x_prompt.reshape(n_prompt, D_MODEL),
                                 p_prompt.reshape(depth, n_prompt, PLE_DIM), None, w, batch, seq)
    cache = (cache_k[0].reshape(dec_batch * cache_rows, KV_DIM),
             cache_v[0].reshape(dec_batch * cache_rows, KV_DIM))
    y_s, kdup_s, v_s, vrows = _trunk(x_sample.reshape(n_sample, D_MODEL),
                                     p_sample.reshape(depth, n_sample, PLE_DIM), cache, w,
                                     dec_batch, dec_seq)

    heads = (N_KV_HEADS, HEAD_DIM)
    tail_k = kdup_p.reshape(batch, seq, KDUP_DIM)[:, seq - WINDOW:].reshape(batch * WINDOW, KDUP_DIM)
    tail_v = v_p.reshape(batch, seq, KV_DIM)[:, seq - WINDOW:]
    new_k_p = _first_copy(tail_k).reshape(1, batch, WINDOW, *heads)
    new_v_p = tail_v.reshape(1, batch, WINDOW, *heads)
    new_k_s = jnp.concatenate(
        [cache_k[:, :, dec_seq:], _first_copy(kdup_s).reshape(1, dec_batch, dec_seq, *heads)], axis=2)
    new_v_s = jnp.concatenate(
        [cache_v[:, :, dec_seq:], v_s.reshape(1, dec_batch, dec_seq, *heads)], axis=2)
    return (y_p.reshape(batch, seq, D_MODEL), y_s.reshape(dec_batch, dec_seq, D_MODEL),
            new_k_p, new_v_p, new_k_s, new_v_s,
            vrows.reshape(1, dec_batch, dec_seq, GMLP_HALF))
```

```python
import functools
import math

import jax
import jax.numpy as jnp
from jax import lax
from jax.experimental import pallas as pl
from jax.experimental.pallas import tpu as pltpu

D_MODEL = 1024
HEAD_DIM = 64
N_HEADS = 16
N_KV_HEADS = 4
GQA_GROUP = N_HEADS // N_KV_HEADS
KV_DIM = N_KV_HEADS * HEAD_DIM
CHUNK = 64
WINDOW = 128
GMLP_CHUNK = 128
GMLP_HALF = 3 * D_MODEL
GMLP_GROUPS = 8
GMLP_GROUP_DIM = GMLP_HALF // GMLP_GROUPS
D_FF = 4 * D_MODEL
PLE_DIM = 256
EPS = 1e-6
NEG_INF = -1e30

LANES = 128
MXU_DIM_V7X = 256
BF16_SUBLANES = 16
VMEM_LIMIT_BYTES_V7X = 56 * 1024 * 1024

KDUP_DIM = 2 * KV_DIM
QK_DIM = D_MODEL + KDUP_DIM
QKV_DIM = QK_DIM + KV_DIM

TOKEN_BLOCK = 512
ATTN_UNIT = 2 * CHUNK
ATTN_KEYS = 2 * WINDOW
PV_ROWS = HEAD_DIM + BF16_SUBLANES
FF_CHUNK = 1024
U_CHUNK = 2 * GMLP_GROUP_DIM

_BF16 = jnp.bfloat16
_F32 = jnp.float32
_CONTRACT_LAST = (((1,), (1,)), ((), ()))


def _resident(shape, layer=None):
    if layer is None:
        return pl.BlockSpec(shape, lambda *_: (0,) * len(shape), pipeline_mode=pl.Buffered(1))
    return pl.BlockSpec((None,) + tuple(shape), lambda *_: (layer,) + (0,) * len(shape),
                        pipeline_mode=pl.Buffered(1))


def _params(n_axes):
    return pltpu.CompilerParams(
        dimension_semantics=("arbitrary",) * n_axes,
        vmem_limit_bytes=VMEM_LIMIT_BYTES_V7X,
    )


def _rms(x, g):
    ms = jnp.mean(x * x, axis=-1, keepdims=True)
    return (x * lax.rsqrt(ms + EPS)) * g


def _dot(a, b):
    return jnp.dot(a, b, preferred_element_type=_F32)


def _qkv_kernel(x_ref, g_ref, w_ref, qkg_ref, q_ref, k_ref, v_ref):
    n = _rms(x_ref[...], g_ref[...]).astype(_BF16)
    qkv = _dot(n, w_ref[...])
    r = lax.broadcasted_iota(jnp.int32, (MXU_DIM_V7X, MXU_DIM_V7X), 0) // HEAD_DIM
    c = lax.broadcasted_iota(jnp.int32, (MXU_DIM_V7X, MXU_DIM_V7X), 1) // HEAD_DIM
    seg = jnp.where(r == c, 1.0, 0.0).astype(_BF16)
    normed = []
    for j in range(QK_DIM // MXU_DIM_V7X):
        cols = slice(j * MXU_DIM_V7X, (j + 1) * MXU_DIM_V7X)
        t = qkv[:, cols]
        ss = _dot((t * t).astype(_BF16), seg)
        inv = lax.rsqrt(ss * (1.0 / HEAD_DIM) + EPS)
        normed.append((t * inv) * qkg_ref[:, cols])
    n_q = D_MODEL // MXU_DIM_V7X
    q_ref[...] = jnp.concatenate(normed[:n_q], axis=1).astype(_BF16)
    k_ref[...] = jnp.concatenate(normed[n_q:], axis=1)
    v_ref[...] = qkv[:, QK_DIM:]


def _qkv_proj(x, g, w_qkv, qk_gain):
    n_tok = x.shape[0]
    tb = TOKEN_BLOCK
    row = lambda width: pl.BlockSpec((tb, width), lambda i: (i, 0))
    return pl.pallas_call(
        _qkv_kernel,
        grid=(n_tok // tb,),
        in_specs=[row(D_MODEL), _resident((1, D_MODEL), layer=0), _resident((D_MODEL, QKV_DIM)),
                  _resident((1, QK_DIM))],
        out_specs=[row(D_MODEL), row(KDUP_DIM), row(KV_DIM)],
        out_shape=[jax.ShapeDtypeStruct((n_tok, D_MODEL), _BF16),
                   jax.ShapeDtypeStruct((n_tok, KDUP_DIM), _F32),
                   jax.ShapeDtypeStruct((n_tok, KV_DIM), _F32)],
        compiler_params=_params(1),
        name="qkv_proj",
    )(x, g, w_qkv, qk_gain)


def _alibi_slope(head):
    return 2.0 ** (-8.0 * (head + 1) / N_HEADS)


def _fill_bias_t(bias_ref, variant, valid_fn):
    s = lax.broadcasted_iota(jnp.int32, (ATTN_KEYS, ATTN_UNIT), 0)
    t = lax.broadcasted_iota(jnp.int32, (ATTN_KEYS, ATTN_UNIT), 1)
    dist = jnp.abs(t + WINDOW - s).astype(_F32)
    valid = valid_fn(t, s)
    for kvh in range(N_KV_HEADS):
        for p in range(2):
            for g in range(2):
                slope = _alibi_slope(kvh * GQA_GROUP + 2 * g + p)
                bias_ref[variant, kvh, p * ATTN_KEYS:(p + 1) * ATTN_KEYS,
                         g * ATTN_UNIT:(g + 1) * ATTN_UNIT] = jnp.where(valid, -slope * dist, NEG_INF)


def _attn_prompt_kernel(sinks_ref, x_ref, q_ref, k_ref, kh_ref, v_ref, vh_ref, wo_ref,
                        out_ref, bias_ref, klo_scr, khi_scr, vt_scr, ot_scr):
    b, i = pl.program_id(0), pl.program_id(1)
    tb = TOKEN_BLOCK

    @pl.when((b == 0) & (i == 0))
    def _():
        in_window = lambda t, s: ((t < CHUNK) & (s < WINDOW + CHUNK)) | ((t >= CHUNK) & (s >= CHUNK))
        _fill_bias_t(bias_ref, 0, in_window)
        _fill_bias_t(bias_ref, 1, lambda t, s: in_window(t, s) & (s >= WINDOW))
        vt_scr[:, HEAD_DIM:, :] = jnp.ones((N_KV_HEADS, PV_ROWS - HEAD_DIM, WINDOW + tb), _BF16)

    kd = jnp.concatenate([kh_ref[...], k_ref[...]], axis=0)
    low_half = lax.broadcasted_iota(jnp.int32, (WINDOW + tb, LANES), 1) < HEAD_DIM
    for kvh in range(N_KV_HEADS):
        kk = kd[:, kvh * LANES:(kvh + 1) * LANES]
        klo_scr[kvh] = jnp.where(low_half, kk, 0.0).astype(_BF16)
        khi_scr[kvh] = jnp.where(low_half, 0.0, kk).astype(_BF16)
    vt = jnp.concatenate([vh_ref[...], v_ref[...]], axis=0).T
    for kvh in range(N_KV_HEADS):
        vt_scr[kvh, :HEAD_DIM, :] = vt[kvh * HEAD_DIM:(kvh + 1) * HEAD_DIM, :].astype(_BF16)

    first_in_seq = jnp.where(i == 0, 1, 0)
    lane2 = lax.broadcasted_iota(jnp.int32, (1, 2 * ATTN_UNIT), 1)
    for u in range(tb // ATTN_UNIT):
        rows = slice(u * ATTN_UNIT, (u + 1) * ATTN_UNIT)
        keys = slice(u * ATTN_UNIT, u * ATTN_UNIT + ATTN_KEYS)
        variant = first_in_seq if u == 0 else 0
        for kvh in range(N_KV_HEADS):
            k2 = jnp.concatenate([klo_scr[kvh, keys, :], khi_scr[kvh, keys, :]], axis=0)
            q2 = jnp.concatenate(
                [q_ref[rows, (2 * kvh + g) * LANES:(2 * kvh + g + 1) * LANES] for g in range(2)],
                axis=0)
            s_t = lax.dot_general(k2, q2, _CONTRACT_LAST, preferred_element_type=_F32)
            s_t = s_t + bias_ref[variant, kvh]
            v1 = vt_scr[kvh, :, keys]
            for p in range(2):
                head0 = kvh * GQA_GROUP + p
                sink = jnp.where(lane2 < ATTN_UNIT, sinks_ref[0, head0], sinks_ref[0, head0 + 2])
                sp = s_t[p * ATTN_KEYS:(p + 1) * ATTN_KEYS, :]
                m = jnp.maximum(jnp.max(sp, axis=0, keepdims=True), sink)
                e = jnp.exp(sp - m)
                r = _dot(v1, e.astype(_BF16))
                den = r[HEAD_DIM:HEAD_DIM + 1, :] + jnp.exp(sink - m)
                o = r[:HEAD_DIM, :] * (1.0 / den)
                for g in range(2):
                    head = head0 + 2 * g
                    ot_scr[head * HEAD_DIM:(head + 1) * HEAD_DIM, rows] = (
                        o[:, g * ATTN_UNIT:(g + 1) * ATTN_UNIT].astype(_BF16))

    out_ref[...] = x_ref[...] + _dot(ot_scr[...].T, wo_ref[...])


def _attn_prompt(sinks, x, q, kdup, v, w_o, batch, seq):
    tb = TOKEN_BLOCK
    nt = seq // tb
    halo_per_block = tb // WINDOW
    row = lambda width: pl.BlockSpec((tb, width), lambda b, i: (b * nt + i, 0))
    halo = lambda width: pl.BlockSpec(
        (WINDOW, width), lambda b, i: (jnp.maximum((b * nt + i) * halo_per_block - 1, 0), 0))
    return pl.pallas_call(
        _attn_prompt_kernel,
        grid=(batch, nt),
        in_specs=[pl.BlockSpec(memory_space=pltpu.SMEM),
                  row(D_MODEL), row(D_MODEL), row(KDUP_DIM), halo(KDUP_DIM), row(KV_DIM),
                  halo(KV_DIM), _resident((D_MODEL, D_MODEL), layer=0)],
        out_specs=row(D_MODEL),
        out_shape=jax.ShapeDtypeStruct(x.shape, _F32),
        scratch_shapes=[
            pltpu.VMEM((2, N_KV_HEADS, 2 * ATTN_KEYS, 2 * ATTN_UNIT), _F32),
            pltpu.VMEM((N_KV_HEADS, WINDOW + tb, LANES), _BF16),
            pltpu.VMEM((N_KV_HEADS, WINDOW + tb, LANES), _BF16),
            pltpu.VMEM((N_KV_HEADS, PV_ROWS, WINDOW + tb), _BF16),
            pltpu.VMEM((D_MODEL, tb), _BF16),
        ],
        compiler_params=_params(2),
        name="attn_prompt",
    )(sinks, x, q, kdup, kdup, v, v, w_o)


def _fill_bias(bias_ref, q_rows, n_keys):
    t = lax.broadcasted_iota(jnp.int32, (q_rows, ATTN_KEYS), 0)
    s = lax.broadcasted_iota(jnp.int32, (q_rows, ATTN_KEYS), 1)
    dist = jnp.abs(t + WINDOW - s).astype(_F32)
    for kvh in range(N_KV_HEADS):
        for j in range(GQA_GROUP):
            slope = _alibi_slope(kvh * GQA_GROUP + j)
            bias_ref[kvh, j * q_rows:(j + 1) * q_rows, :] = jnp.where(
                s < n_keys, -slope * dist, NEG_INF)


def _attn_sample_kernel(sinks_ref, x_ref, q_ref, k_ref, ck_ref, v_ref, cv_ref, wo_ref,
                        out_ref, bias_ref, k_scr, v_scr, o_scr, *, dec_seq, cache_rows):
    n_keys = cache_rows + dec_seq

    @pl.when(pl.program_id(0) == 0)
    def _():
        _fill_bias(bias_ref, dec_seq, n_keys)
        k_scr[...] = jnp.zeros_like(k_scr)
        v_scr[...] = jnp.zeros_like(v_scr)

    def one_batch(bb, carry):
        rows = pl.ds(pl.multiple_of(bb * dec_seq, dec_seq), dec_seq)
        cache = pl.ds(pl.multiple_of(bb * cache_rows, cache_rows), cache_rows)
        k_scr[:cache_rows, :] = ck_ref[cache, :].astype(_BF16)
        k_new = k_ref[rows, :]
        for kvh in range(N_KV_HEADS):
            k_scr[cache_rows:n_keys, kvh * HEAD_DIM:(kvh + 1) * HEAD_DIM] = (
                k_new[:, kvh * LANES:kvh * LANES + HEAD_DIM].astype(_BF16))
        v_scr[:cache_rows, :] = cv_ref[cache, :].astype(_BF16)
        v_scr[cache_rows:n_keys, :] = v_ref[rows, :].astype(_BF16)
        q_u = q_ref[rows, :]
        k_u = k_scr[...]
        v_u = v_scr[...]
        for kvh in range(N_KV_HEADS):
            kv_cols = slice(kvh * HEAD_DIM, (kvh + 1) * HEAD_DIM)
            q4 = jnp.concatenate(
                [q_u[:, (kvh * GQA_GROUP + j) * HEAD_DIM:(kvh * GQA_GROUP + j + 1) * HEAD_DIM]
                 for j in range(GQA_GROUP)], axis=0)
            sink = jnp.concatenate(
                [jnp.full((dec_seq, 1), sinks_ref[0, kvh * GQA_GROUP + j], _F32)
                 for j in range(GQA_GROUP)], axis=0)
            s = lax.dot_general(q4, k_u[:, kv_cols], _CONTRACT_LAST, preferred_element_type=_F32)
            s = s + bias_ref[kvh]
            m = jnp.maximum(jnp.max(s, axis=-1, keepdims=True), sink)
            e = jnp.exp(s - m)
            den = jnp.sum(e, axis=-1, keepdims=True) + jnp.exp(sink - m)
            o = _dot(e.astype(_BF16), v_u[:, kv_cols]) / den
            for j in range(GQA_GROUP):
                head = kvh * GQA_GROUP + j
                o_scr[rows, head * HEAD_DIM:(head + 1) * HEAD_DIM] = (
                    o[j * dec_seq:(j + 1) * dec_seq, :].astype(_BF16))
        return carry

    lax.fori_loop(0, TOKEN_BLOCK // dec_seq, one_batch, 0)
    out_ref[...] = x_ref[...] + _dot(o_scr[...], wo_ref[...])


def _attn_sample(sinks, x, q, kdup, v, cache_k, cache_v, w_o, dec_seq, cache_rows):
    n_tok = x.shape[0]
    tb = TOKEN_BLOCK
    batches_per_block = tb // dec_seq
    row = lambda width: pl.BlockSpec((tb, width), lambda i: (i, 0))
    cache = pl.BlockSpec((batches_per_block * cache_rows, KV_DIM), lambda i: (i, 0))
    return pl.pallas_call(
        functools.partial(_attn_sample_kernel, dec_seq=dec_seq, cache_rows=cache_rows),
        grid=(n_tok // tb,),
        in_specs=[pl.BlockSpec(memory_space=pltpu.SMEM),
                  row(D_MODEL), row(D_MODEL), row(KDUP_DIM), cache, row(KV_DIM), cache,
                  _resident((D_MODEL, D_MODEL), layer=0)],
        out_specs=row(D_MODEL),
        out_shape=jax.ShapeDtypeStruct(x.shape, _F32),
        scratch_shapes=[
            pltpu.VMEM((N_KV_HEADS, GQA_GROUP * dec_seq, ATTN_KEYS), _F32),
            pltpu.VMEM((ATTN_KEYS, KV_DIM), _BF16),
            pltpu.VMEM((ATTN_KEYS, KV_DIM), _BF16),
            pltpu.VMEM((tb, D_MODEL), _BF16),
        ],
        compiler_params=_params(1),
        name="attn_sample",
    )(sinks, x, q, kdup, cache_k, v, cache_v, w_o)


def _ffn_ple_kernel(h_ref, p_ref, gf_ref, w1_ref, w2_ref, gp_ref, wg_ref, wp_ref, out_ref):
    h = h_ref[...]
    n = _rms(h, gf_ref[...]).astype(_BF16)
    for c in range(D_FF // FF_CHUNK):
        cols = slice(c * FF_CHUNK, (c + 1) * FF_CHUNK)
        a = jnp.maximum(_dot(n, w1_ref[:, cols]), 0.0)
        h = h + _dot((a * a).astype(_BF16), w2_ref[cols, :])
    z = _dot(_rms(h, gp_ref[...]).astype(_BF16), wg_ref[...])
    gate = 1.0 / (1.0 + jnp.exp(-z))
    out_ref[...] = h + gate * _dot(p_ref[...].astype(_BF16), wp_ref[...])


def _ffn_ple(h, p, g_ffn, w1, w2, g_ple, w_gate, w_proj, layer):
    n_tok = h.shape[0]
    tb = TOKEN_BLOCK
    row = lambda width: pl.BlockSpec((tb, width), lambda i: (i, 0))
    return pl.pallas_call(
        _ffn_ple_kernel,
        grid=(n_tok // tb,),
        in_specs=[row(D_MODEL), pl.BlockSpec((None, tb, PLE_DIM), lambda i: (layer, i, 0)),
                  _resident((1, D_MODEL), layer),
                  _resident((D_MODEL, D_FF), layer), _resident((D_FF, D_MODEL), layer),
                  _resident((1, D_MODEL), layer), _resident((D_MODEL, D_MODEL), layer),
                  _resident((PLE_DIM, D_MODEL), layer)],
        out_specs=row(D_MODEL),
        out_shape=jax.ShapeDtypeStruct(h.shape, _F32),
        compiler_params=_params(1),
        name="ffn_ple",
    )(h, p, g_ffn, w1, w2, g_ple, w_gate, w_proj)


def _gelu_tanh(x):
    c = math.sqrt(2.0 / math.pi)
    return x * (0.5 * (1.0 + jnp.tanh(c * (x + 0.044715 * (x * x * x)))))


def _gmlp_kernel(h_ref, g_ref, wuv_ref, vg_ref, ws_ref, bs_ref, wout_ref, out_ref, *rest,
                 emit_v):
    if emit_v:
        vout_ref, v_scr, vb_scr = rest
    else:
        v_scr, vb_scr = rest
    tb = h_ref.shape[0]
    h = h_ref[...]
    n = _rms(h, g_ref[...]).astype(_BF16)

    ssq = jnp.zeros((tb, 1), _F32)
    for c in range(GMLP_HALF // FF_CHUNK):
        cols = slice(c * FF_CHUNK, (c + 1) * FF_CHUNK)
        vc = _gelu_tanh(_dot(n, wuv_ref[:, GMLP_HALF + c * FF_CHUNK:GMLP_HALF + (c + 1) * FF_CHUNK]))
        v_scr[:, cols] = vc
        ssq = ssq + jnp.sum(vc * vc, axis=-1, keepdims=True)
    inv = lax.rsqrt(ssq * (1.0 / GMLP_HALF) + EPS)
    for c in range(GMLP_HALF // FF_CHUNK):
        cols = slice(c * FF_CHUNK, (c + 1) * FF_CHUNK)
        vn = (v_scr[:, cols] * inv) * vg_ref[:, cols]
        if emit_v:
            vout_ref[:, cols] = vn
        vb_scr[:, cols] = vn.astype(_BF16)

    groups_per_step = U_CHUNK // GMLP_GROUP_DIM
    for c in range(GMLP_HALF // U_CHUNK):
        u = _gelu_tanh(_dot(n, wuv_ref[:, c * U_CHUNK:(c + 1) * U_CHUNK]))
        gated_rows = []
        for r in range(tb // GMLP_CHUNK):
            rows = slice(r * GMLP_CHUNK, (r + 1) * GMLP_CHUNK)
            parts = []
            for gg in range(groups_per_step):
                grp = c * groups_per_step + gg
                cols = slice(grp * GMLP_GROUP_DIM, (grp + 1) * GMLP_GROUP_DIM)
                s = _dot(ws_ref[grp], vb_scr[rows, cols]) + bs_ref[:, grp:grp + 1]
                parts.append(u[rows, gg * GMLP_GROUP_DIM:(gg + 1) * GMLP_GROUP_DIM] * s)
            gated_rows.append(jnp.concatenate(parts, axis=1))
        gated = jnp.concatenate(gated_rows, axis=0).astype(_BF16)
        h = h + _dot(gated, wout_ref[c * U_CHUNK:(c + 1) * U_CHUNK, :])
    out_ref[...] = h


def _gmlp(h, g, w_uv, v_gain, ws, bs_t, w_out, emit_v):
    n_tok = h.shape[0]
    tb = TOKEN_BLOCK
    row = lambda width: pl.BlockSpec((tb, width), lambda i: (i, 0))
    out_specs = [row(D_MODEL)]
    out_shape = [jax.ShapeDtypeStruct(h.shape, _F32)]
    if emit_v:
        out_specs.append(row(GMLP_HALF))
        out_shape.append(jax.ShapeDtypeStruct((n_tok, GMLP_HALF), _F32))
    return pl.pallas_call(
        functools.partial(_gmlp_kernel, emit_v=emit_v),
        grid=(n_tok // tb,),
        in_specs=[row(D_MODEL), _resident((1, D_MODEL), layer=1),
                  _resident((D_MODEL, 2 * GMLP_HALF), layer=0),
                  _resident((1, GMLP_HALF)),
                  _resident((GMLP_GROUPS, GMLP_CHUNK, GMLP_CHUNK)),
                  _resident((GMLP_CHUNK, GMLP_GROUPS)),
                  _resident((GMLP_HALF, D_MODEL), layer=0)],
        out_specs=out_specs,
        out_shape=out_shape,
        scratch_shapes=[pltpu.VMEM((tb, GMLP_HALF), _F32), pltpu.VMEM((tb, GMLP_HALF), _BF16)],
        compiler_params=_params(1),
        name="gmlp_v" if emit_v else "gmlp",
    )(h, g, w_uv, v_gain, ws, bs_t, w_out)


def _gmlp_spatial_weights(w_s, b_s, length):
    tril = jnp.tril(jnp.ones((length, length), w_s.dtype))
    ws = w_s[:, :length, :length] * tril
    reps = GMLP_CHUNK // length
    if reps > 1:
        eye = jnp.eye(reps, dtype=w_s.dtype)
        ws = jnp.einsum("ab,gij->gaibj", eye, ws).reshape(GMLP_GROUPS, GMLP_CHUNK, GMLP_CHUNK)
    bs_t = jnp.tile(b_s[:, :length].T, (reps, 1))
    return ws.astype(_BF16), bs_t


def _first_copy(kdup):
    return kdup.reshape(kdup.shape[0], N_KV_HEADS, 2, HEAD_DIM)[:, :, 0, :]


def _trunk(x, p, cache, w, batch, seq):
    ffn_ple = lambda h, layer: _ffn_ple(h, p, w["g_ffn"], w["w1"], w["w2"], w["g_ple"],
                                        w["w_gate"], w["w_proj"], layer)
    q, kdup, v = _qkv_proj(x, w["g_mix"], w["w_qkv"], w["qk_gain"])
    if cache is None:
        h = _attn_prompt(w["sinks"], x, q, kdup, v, w["w_o"], batch, seq)
    else:
        h = _attn_sample(w["sinks"], x, q, kdup, v, cache[0], cache[1], w["w_o"], seq,
                         cache[0].shape[0] // batch)
    h = ffn_ple(h, 0)
    length = min(seq, GMLP_CHUNK)
    ws, bs_t = _gmlp_spatial_weights(w["w_s"], w["b_s"], length)
    emit_v = cache is not None
    res = _gmlp(h, w["g_mix"], w["w_uv"], w["v_gain"], ws, bs_t, w["w_out"], emit_v)
    h = res[0]
    v_rows = res[1] if emit_v else None
    h = ffn_ple(h, 1)
    return h, kdup, v, v_rows


def kernel(x_prompt, x_sample, p_prompt, p_sample, cache_k, cache_v, g_mix, g_ffn, g_ple,
           attn_w_qkv, attn_q_norm, attn_k_norm, attn_sinks, attn_w_o, gmlp_w_uv, gmlp_v_norm,
           gmlp_w_s, gmlp_b_s, gmlp_w_out, ffn_w1, ffn_w2, ple_w_proj, ple_w_gate):
    batch, seq, _ = x_prompt.shape
    dec_batch, dec_seq, _ = x_sample.shape
    depth = g_mix.shape[0]
    cache_rows = cache_k.shape[2]
    assert depth == 2 and attn_w_qkv.shape[0] == 1 and gmlp_w_uv.shape[0] == 1
    assert seq % TOKEN_BLOCK == 0 and (dec_batch * dec_seq) % TOKEN_BLOCK == 0
    assert TOKEN_BLOCK % dec_seq == 0 and GMLP_CHUNK % dec_seq == 0 and dec_seq % 16 == 0
    assert cache_rows == WINDOW and cache_rows + dec_seq <= ATTN_KEYS

    scale = HEAD_DIM ** -0.5
    w_q = attn_w_qkv[0][:, :D_MODEL]
    w_k = attn_w_qkv[0][:, D_MODEL:D_MODEL + KV_DIM].reshape(D_MODEL, N_KV_HEADS, 1, HEAD_DIM)
    w_kdup = jnp.broadcast_to(w_k, (D_MODEL, N_KV_HEADS, 2, HEAD_DIM)).reshape(D_MODEL, KDUP_DIM)
    w_v = attn_w_qkv[0][:, D_MODEL + KV_DIM:]
    w = {
        "g_mix": g_mix.reshape(depth, 1, D_MODEL),
        "g_ffn": g_ffn.reshape(depth, 1, D_MODEL),
        "g_ple": g_ple.reshape(depth, 1, D_MODEL),
        "w_qkv": jnp.concatenate([w_q, w_kdup, w_v], axis=1).astype(_BF16),
        "qk_gain": jnp.concatenate([jnp.tile(attn_q_norm[0] * scale, N_HEADS),
                                    jnp.tile(attn_k_norm[0], 2 * N_KV_HEADS)]).reshape(1, QK_DIM),
        "sinks": attn_sinks[0].reshape(1, N_HEADS),
        "w_o": attn_w_o.astype(_BF16),
        "w_uv": gmlp_w_uv.astype(_BF16),
        "v_gain": gmlp_v_norm[0].reshape(1, GMLP_HALF),
        "w_s": gmlp_w_s[0],
        "b_s": gmlp_b_s[0],
        "w_out": gmlp_w_out.astype(_BF16),
        "w1": ffn_w1.astype(_BF16),
        "w2": ffn_w2.astype(_BF16),
        "w_gate": ple_w_gate.astype(_BF16),
        "w_proj": ple_w_proj.astype(_BF16),
    }

    n_prompt = batch * seq
    n_sample = dec_batch * dec_seq
    y_p, kdup_p, v_p, _ = _trunk(x_prompt.reshape(n_prompt, D_MODEL),
                                 p_prompt.reshape(depth, n_prompt, PLE_DIM), None, w, batch, seq)
    cache = (cache_k[0].reshape(dec_batch * cache_rows, KV_DIM),
             cache_v[0].reshape(dec_batch * cache_rows, KV_DIM))
    y_s, kdup_s, v_s, vrows = _trunk(x_sample.reshape(n_sample, D_MODEL),
                                     p_sample.reshape(depth, n_sample, PLE_DIM), cache, w,
                                     dec_batch, dec_seq)

    heads = (N_KV_HEADS, HEAD_DIM)
    tail_k = kdup_p.reshape(batch, seq, KDUP_DIM)[:, seq - WINDOW:].reshape(batch * WINDOW, KDUP_DIM)
    tail_v = v_p.reshape(batch, seq, KV_DIM)[:, seq - WINDOW:]
    new_k_p = _first_copy(tail_k).reshape(1, batch, WINDOW, *heads)
    new_v_p = tail_v.reshape(1, batch, WINDOW, *heads)
    new_k_s = jnp.concatenate(
        [cache_k[:, :, dec_seq:], _first_copy(kdup_s).reshape(1, dec_batch, dec_seq, *heads)], axis=2)
    new_v_s = jnp.concatenate(
        [cache_v[:, :, dec_seq:], v_s.reshape(1, dec_batch, dec_seq, *heads)], axis=2)
    return (y_p.reshape(batch, seq, D_MODEL), y_s.reshape(dec_batch, dec_seq, D_MODEL),
            new_k_p, new_v_p, new_k_s, new_v_s,
            vrows.reshape(1, dec_batch, dec_seq, GMLP_HALF))
```

```python
import functools
import math

import jax
import jax.numpy as jnp
from jax import lax
from jax.experimental import pallas as pl
from jax.experimental.pallas import tpu as pltpu

D_MODEL = 1024
HEAD_DIM = 64
N_HEADS = 16
N_KV_HEADS = 4
GQA_GROUP = N_HEADS // N_KV_HEADS
KV_DIM = N_KV_HEADS * HEAD_DIM
CHUNK = 64
WINDOW = 128
GMLP_CHUNK = 128
GMLP_HALF = 3 * D_MODEL
GMLP_GROUPS = 8
GMLP_GROUP_DIM = GMLP_HALF // GMLP_GROUPS
D_FF = 4 * D_MODEL
PLE_DIM = 256
EPS = 1e-6
NEG_INF = -1e30

LANES = 128
MXU_DIM_V7X = 256
BF16_SUBLANES = 16
VMEM_LIMIT_BYTES_V7X = 56 * 1024 * 1024

KDUP_DIM = 2 * KV_DIM
QK_DIM = D_MODEL + KDUP_DIM
QKV_DIM = QK_DIM + KV_DIM

TOKEN_BLOCK = 512
ATTN_UNIT = 2 * CHUNK
ATTN_KEYS = 2 * WINDOW
PV_ROWS = HEAD_DIM + BF16_SUBLANES
FF_CHUNK = 1024
U_CHUNK = 2 * GMLP_GROUP_DIM

_BF16 = jnp.bfloat16
_F32 = jnp.float32
_CONTRACT_LAST = (((1,), (1,)), ((), ()))


def _resident(shape, layer=None):
    if layer is None:
        return pl.BlockSpec(shape, lambda *_: (0,) * len(shape), pipeline_mode=pl.Buffered(1))
    return pl.BlockSpec((None,) + tuple(shape), lambda *_: (layer,) + (0,) * len(shape),
                        pipeline_mode=pl.Buffered(1))


def _params(n_axes):
    return pltpu.CompilerParams(
        dimension_semantics=("arbitrary",) * n_axes,
        vmem_limit_bytes=VMEM_LIMIT_BYTES_V7X,
    )


def _rms(x, g):
    ms = jnp.mean(x * x, axis=-1, keepdims=True)
    return (x * lax.rsqrt(ms + EPS)) * g


def _dot(a, b):
    return jnp.dot(a, b, preferred_element_type=_F32)


def _qkv_kernel(x_ref, g_ref, w_ref, qkg_ref, q_ref, k_ref, v_ref):
    n = _rms(x_ref[...], g_ref[...]).astype(_BF16)
    qkv = _dot(n, w_ref[...])
    r = lax.broadcasted_iota(jnp.int32, (MXU_DIM_V7X, MXU_DIM_V7X), 0) // HEAD_DIM
    c = lax.broadcasted_iota(jnp.int32, (MXU_DIM_V7X, MXU_DIM_V7X), 1) // HEAD_DIM
    seg = jnp.where(r == c, 1.0, 0.0).astype(_BF16)
    normed = []
    for j in range(QK_DIM // MXU_DIM_V7X):
        cols = slice(j * MXU_DIM_V7X, (j + 1) * MXU_DIM_V7X)
        t = qkv[:, cols]
        ss = _dot((t * t).astype(_BF16), seg)
        inv = lax.rsqrt(ss * (1.0 / HEAD_DIM) + EPS)
        normed.append((t * inv) * qkg_ref[:, cols])
    n_q = D_MODEL // MXU_DIM_V7X
    q_ref[...] = jnp.concatenate(normed[:n_q], axis=1).astype(_BF16)
    k_ref[...] = jnp.concatenate(normed[n_q:], axis=1)
    v_ref[...] = qkv[:, QK_DIM:]


def _qkv_proj(x, g, w_qkv, qk_gain):
    n_tok = x.shape[0]
    tb = TOKEN_BLOCK
    row = lambda width: pl.BlockSpec((tb, width), lambda i: (i, 0))
    return pl.pallas_call(
        _qkv_kernel,
        grid=(n_tok // tb,),
        in_specs=[row(D_MODEL), _resident((1, D_MODEL), layer=0), _resident((D_MODEL, QKV_DIM)),
                  _resident((1, QK_DIM))],
        out_specs=[row(D_MODEL), row(KDUP_DIM), row(KV_DIM)],
        out_shape=[jax.ShapeDtypeStruct((n_tok, D_MODEL), _BF16),
                   jax.ShapeDtypeStruct((n_tok, KDUP_DIM), _F32),
                   jax.ShapeDtypeStruct((n_tok, KV_DIM), _F32)],
        compiler_params=_params(1),
        name="qkv_proj",
    )(x, g, w_qkv, qk_gain)


def _alibi_slope(head):
    return 2.0 ** (-8.0 * (head + 1) / N_HEADS)


def _fill_bias_t(bias_ref, variant, valid_fn):
    s = lax.broadcasted_iota(jnp.int32, (ATTN_KEYS, ATTN_UNIT), 0)
    t = lax.broadcasted_iota(jnp.int32, (ATTN_KEYS, ATTN_UNIT), 1)
    dist = jnp.abs(t + WINDOW - s).astype(_F32)
    valid = valid_fn(t, s)
    for kvh in range(N_KV_HEADS):
        for p in range(2):
            for g in range(2):
                slope = _alibi_slope(kvh * GQA_GROUP + 2 * g + p)
                bias_ref[variant, kvh, p * ATTN_KEYS:(p + 1) * ATTN_KEYS,
                         g * ATTN_UNIT:(g + 1) * ATTN_UNIT] = jnp.where(valid, -slope * dist, NEG_INF)


def _attn_prompt_kernel(sinks_ref, x_ref, q_ref, k_ref, kh_ref, v_ref, vh_ref, wo_ref,
                        out_ref, bias_ref, klo_scr, khi_scr, vt_scr, ot_scr):
    b, i = pl.program_id(0), pl.program_id(1)
    tb = TOKEN_BLOCK

    @pl.when((b == 0) & (i == 0))
    def _():
        in_window = lambda t, s: ((t < CHUNK) & (s < WINDOW + CHUNK)) | ((t >= CHUNK) & (s >= CHUNK))
        _fill_bias_t(bias_ref, 0, in_window)
        _fill_bias_t(bias_ref, 1, lambda t, s: in_window(t, s) & (s >= WINDOW))
        vt_scr[:, HEAD_DIM:, :] = jnp.ones((N_KV_HEADS, PV_ROWS - HEAD_DIM, WINDOW + tb), _BF16)

    kd = jnp.concatenate([kh_ref[...], k_ref[...]], axis=0)
    low_half = lax.broadcasted_iota(jnp.int32, (WINDOW + tb, LANES), 1) < HEAD_DIM
    for kvh in range(N_KV_HEADS):
        kk = kd[:, kvh * LANES:(kvh + 1) * LANES]
        klo_scr[kvh] = jnp.where(low_half, kk, 0.0).astype(_BF16)
        khi_scr[kvh] = jnp.where(low_half, 0.0, kk).astype(_BF16)
    vt = jnp.concatenate([vh_ref[...], v_ref[...]], axis=0).T
    for kvh in range(N_KV_HEADS):
        vt_scr[kvh, :HEAD_DIM, :] = vt[kvh * HEAD_DIM:(kvh + 1) * HEAD_DIM, :].astype(_BF16)

    first_in_seq = jnp.where(i == 0, 1, 0)
    lane2 = lax.broadcasted_iota(jnp.int32, (1, 2 * ATTN_UNIT), 1)
    n_units = tb // ATTN_UNIT

    def scores(u):
        rows = slice(u * ATTN_UNIT, (u + 1) * ATTN_UNIT)
        keys = slice(u * ATTN_UNIT, u * ATTN_UNIT + ATTN_KEYS)
        variant = first_in_seq if u == 0 else 0
        out = []
        for kvh in range(N_KV_HEADS):
            k2 = jnp.concatenate([klo_scr[kvh, keys, :], khi_scr[kvh, keys, :]], axis=0)
            q2 = jnp.concatenate(
                [q_ref[rows, (2 * kvh + g) * LANES:(2 * kvh + g + 1) * LANES] for g in range(2)],
                axis=0)
            s_t = lax.dot_general(k2, q2, _CONTRACT_LAST, preferred_element_type=_F32)
            out.append(s_t + bias_ref[variant, kvh])
        return out

    s_next = scores(0)
    for u in range(n_units):
        rows = slice(u * ATTN_UNIT, (u + 1) * ATTN_UNIT)
        keys = slice(u * ATTN_UNIT, u * ATTN_UNIT + ATTN_KEYS)
        s_cur = s_next
        if u + 1 < n_units:
            s_next = scores(u + 1)
        chains = []
        for kvh in range(N_KV_HEADS):
            for p in range(2):
                head0 = kvh * GQA_GROUP + p
                sink = jnp.where(lane2 < ATTN_UNIT, sinks_ref[0, head0], sinks_ref[0, head0 + 2])
                sp = s_cur[kvh][p * ATTN_KEYS:(p + 1) * ATTN_KEYS, :]
                m = jnp.maximum(jnp.max(sp, axis=0, keepdims=True), sink)
                chains.append((head0, kvh, jnp.exp(sp - m).astype(_BF16), jnp.exp(sink - m)))
        results = [(head0, _dot(vt_scr[kvh, :, keys], e), sink_e)
                   for head0, kvh, e, sink_e in chains]
        for head0, r, sink_e in results:
            den = r[HEAD_DIM:HEAD_DIM + 1, :] + sink_e
            o = r[:HEAD_DIM, :] * (1.0 / den)
            for g in range(2):
                head = head0 + 2 * g
                ot_scr[head * HEAD_DIM:(head + 1) * HEAD_DIM, rows] = (
                    o[:, g * ATTN_UNIT:(g + 1) * ATTN_UNIT].astype(_BF16))

    out_ref[...] = x_ref[...] + _dot(ot_scr[...].T, wo_ref[...])


def _attn_prompt(sinks, x, q, kdup, v, w_o, batch, seq):
    tb = TOKEN_BLOCK
    nt = seq // tb
    halo_per_block = tb // WINDOW
    row = lambda width: pl.BlockSpec((tb, width), lambda b, i: (b * nt + i, 0))
    halo = lambda width: pl.BlockSpec(
        (WINDOW, width), lambda b, i: (jnp.maximum((b * nt + i) * halo_per_block - 1, 0), 0))
    return pl.pallas_call(
        _attn_prompt_kernel,
        grid=(batch, nt),
        in_specs=[pl.BlockSpec(memory_space=pltpu.SMEM),
                  row(D_MODEL), row(D_MODEL), row(KDUP_DIM), halo(KDUP_DIM), row(KV_DIM),
                  halo(KV_DIM), _resident((D_MODEL, D_MODEL), layer=0)],
        out_specs=row(D_MODEL),
        out_shape=jax.ShapeDtypeStruct(x.shape, _F32),
        scratch_shapes=[
            pltpu.VMEM((2, N_KV_HEADS, 2 * ATTN_KEYS, 2 * ATTN_UNIT), _F32),
            pltpu.VMEM((N_KV_HEADS, WINDOW + tb, LANES), _BF16),
            pltpu.VMEM((N_KV_HEADS, WINDOW + tb, LANES), _BF16),
            pltpu.VMEM((N_KV_HEADS, PV_ROWS, WINDOW + tb), _BF16),
            pltpu.VMEM((D_MODEL, tb), _BF16),
        ],
        compiler_params=_params(2),
        name="attn_prompt",
    )(sinks, x, q, kdup, kdup, v, v, w_o)


def _fill_bias(bias_ref, q_rows, n_keys):
    t = lax.broadcasted_iota(jnp.int32, (q_rows, ATTN_KEYS), 0)
    s = lax.broadcasted_iota(jnp.int32, (q_rows, ATTN_KEYS), 1)
    dist = jnp.abs(t + WINDOW - s).astype(_F32)
    for kvh in range(N_KV_HEADS):
        for j in range(GQA_GROUP):
            slope = _alibi_slope(kvh * GQA_GROUP + j)
            bias_ref[kvh, j * q_rows:(j + 1) * q_rows, :] = jnp.where(
                s < n_keys, -slope * dist, NEG_INF)


def _attn_sample_kernel(sinks_ref, x_ref, q_ref, k_ref, ck_ref, v_ref, cv_ref, wo_ref,
                        out_ref, bias_ref, k_scr, v_scr, o_scr, *, dec_seq, cache_rows):
    n_keys = cache_rows + dec_seq

    @pl.when(pl.program_id(0) == 0)
    def _():
        _fill_bias(bias_ref, dec_seq, n_keys)
        k_scr[...] = jnp.zeros_like(k_scr)
        v_scr[...] = jnp.zeros_like(v_scr)

    def one_batch(bb, carry):
        rows = pl.ds(pl.multiple_of(bb * dec_seq, dec_seq), dec_seq)
        cache = pl.ds(pl.multiple_of(bb * cache_rows, cache_rows), cache_rows)
        k_scr[:cache_rows, :] = ck_ref[cache, :].astype(_BF16)
        k_new = k_ref[rows, :]
        for kvh in range(N_KV_HEADS):
            k_scr[cache_rows:n_keys, kvh * HEAD_DIM:(kvh + 1) * HEAD_DIM] = (
                k_new[:, kvh * LANES:kvh * LANES + HEAD_DIM].astype(_BF16))
        v_scr[:cache_rows, :] = cv_ref[cache, :].astype(_BF16)
        v_scr[cache_rows:n_keys, :] = v_ref[rows, :].astype(_BF16)
        q_u = q_ref[rows, :]
        k_u = k_scr[...]
        v_u = v_scr[...]
        for kvh in range(N_KV_HEADS):
            kv_cols = slice(kvh * HEAD_DIM, (kvh + 1) * HEAD_DIM)
            q4 = jnp.concatenate(
                [q_u[:, (kvh * GQA_GROUP + j) * HEAD_DIM:(kvh * GQA_GROUP + j + 1) * HEAD_DIM]
                 for j in range(GQA_GROUP)], axis=0)
            sink = jnp.concatenate(
                [jnp.full((dec_seq, 1), sinks_ref[0, kvh * GQA_GROUP + j], _F32)
                 for j in range(GQA_GROUP)], axis=0)
            s = lax.dot_general(q4, k_u[:, kv_cols], _CONTRACT_LAST, preferred_element_type=_F32)
            s = s + bias_ref[kvh]
            m = jnp.maximum(jnp.max(s, axis=-1, keepdims=True), sink)
            e = jnp.exp(s - m)
            den = jnp.sum(e, axis=-1, keepdims=True) + jnp.exp(sink - m)
            o = _dot(e.astype(_BF16), v_u[:, kv_cols]) / den
            for j in range(GQA_GROUP):
                head = kvh * GQA_GROUP + j
                o_scr[rows, head * HEAD_DIM:(head + 1) * HEAD_DIM] = (
                    o[j * dec_seq:(j + 1) * dec_seq, :].astype(_BF16))
        return carry

    lax.fori_loop(0, TOKEN_BLOCK // dec_seq, one_batch, 0)
    out_ref[...] = x_ref[...] + _dot(o_scr[...], wo_ref[...])


def _attn_sample(sinks, x, q, kdup, v, cache_k, cache_v, w_o, dec_seq, cache_rows):
    n_tok = x.shape[0]
    tb = TOKEN_BLOCK
    batches_per_block = tb // dec_seq
    row = lambda width: pl.BlockSpec((tb, width), lambda i: (i, 0))
    cache = pl.BlockSpec((batches_per_block * cache_rows, KV_DIM), lambda i: (i, 0))
    return pl.pallas_call(
        functools.partial(_attn_sample_kernel, dec_seq=dec_seq, cache_rows=cache_rows),
        grid=(n_tok // tb,),
        in_specs=[pl.BlockSpec(memory_space=pltpu.SMEM),
                  row(D_MODEL), row(D_MODEL), row(KDUP_DIM), cache, row(KV_DIM), cache,
                  _resident((D_MODEL, D_MODEL), layer=0)],
        out_specs=row(D_MODEL),
        out_shape=jax.ShapeDtypeStruct(x.shape, _F32),
        scratch_shapes=[
            pltpu.VMEM((N_KV_HEADS, GQA_GROUP * dec_seq, ATTN_KEYS), _F32),
            pltpu.VMEM((ATTN_KEYS, KV_DIM), _BF16),
            pltpu.VMEM((ATTN_KEYS, KV_DIM), _BF16),
            pltpu.VMEM((tb, D_MODEL), _BF16),
        ],
        compiler_params=_params(1),
        name="attn_sample",
    )(sinks, x, q, kdup, cache_k, v, cache_v, w_o)


def _ffn_ple_kernel(h_ref, p_ref, gf_ref, w1_ref, w2_ref, gp_ref, wg_ref, wp_ref, out_ref):
    h = h_ref[...]
    n = _rms(h, gf_ref[...]).astype(_BF16)
    for c in range(D_FF // FF_CHUNK):
        cols = slice(c * FF_CHUNK, (c + 1) * FF_CHUNK)
        a = jnp.maximum(_dot(n, w1_ref[:, cols]), 0.0)
        h = h + _dot((a * a).astype(_BF16), w2_ref[cols, :])
    z = _dot(_rms(h, gp_ref[...]).astype(_BF16), wg_ref[...])
    gate = 1.0 / (1.0 + jnp.exp(-z))
    out_ref[...] = h + gate * _dot(p_ref[...].astype(_BF16), wp_ref[...])


def _ffn_ple(h, p, g_ffn, w1, w2, g_ple, w_gate, w_proj, layer):
    n_tok = h.shape[0]
    tb = TOKEN_BLOCK
    row = lambda width: pl.BlockSpec((tb, width), lambda i: (i, 0))
    return pl.pallas_call(
        _ffn_ple_kernel,
        grid=(n_tok // tb,),
        in_specs=[row(D_MODEL), pl.BlockSpec((None, tb, PLE_DIM), lambda i: (layer, i, 0)),
                  _resident((1, D_MODEL), layer),
                  _resident((D_MODEL, D_FF), layer), _resident((D_FF, D_MODEL), layer),
                  _resident((1, D_MODEL), layer), _resident((D_MODEL, D_MODEL), layer),
                  _resident((PLE_DIM, D_MODEL), layer)],
        out_specs=row(D_MODEL),
        out_shape=jax.ShapeDtypeStruct(h.shape, _F32),
        compiler_params=_params(1),
        name="ffn_ple",
    )(h, p, g_ffn, w1, w2, g_ple, w_gate, w_proj)


def _gelu_tanh(x):
    c = math.sqrt(2.0 / math.pi)
    return x * (0.5 * (1.0 + jnp.tanh(c * (x + 0.044715 * (x * x * x)))))


def _gmlp_kernel(h_ref, g_ref, wuv_ref, vg_ref, ws_ref, bs_ref, wout_ref, out_ref, *rest,
                 emit_v):
    if emit_v:
        vout_ref, v_scr, vb_scr = rest
    else:
        v_scr, vb_scr = rest
    tb = h_ref.shape[0]
    h = h_ref[...]
    n = _rms(h, g_ref[...]).astype(_BF16)

    ssq = jnp.zeros((tb, 1), _F32)
    for c in range(GMLP_HALF // FF_CHUNK):
        cols = slice(c * FF_CHUNK, (c + 1) * FF_CHUNK)
        vc = _gelu_tanh(_dot(n, wuv_ref[:, GMLP_HALF + c * FF_CHUNK:GMLP_HALF + (c + 1) * FF_CHUNK]))
        v_scr[:, cols] = vc
        ssq = ssq + jnp.sum(vc * vc, axis=-1, keepdims=True)
    inv = lax.rsqrt(ssq * (1.0 / GMLP_HALF) + EPS)
    for c in range(GMLP_HALF // FF_CHUNK):
        cols = slice(c * FF_CHUNK, (c + 1) * FF_CHUNK)
        vn = (v_scr[:, cols] * inv) * vg_ref[:, cols]
        if emit_v:
            vout_ref[:, cols] = vn
        vb_scr[:, cols] = vn.astype(_BF16)

    groups_per_step = U_CHUNK // GMLP_GROUP_DIM
    for c in range(GMLP_HALF // U_CHUNK):
        u = _gelu_tanh(_dot(n, wuv_ref[:, c * U_CHUNK:(c + 1) * U_CHUNK]))
        gated_rows = []
        for r in range(tb // GMLP_CHUNK):
            rows = slice(r * GMLP_CHUNK, (r + 1) * GMLP_CHUNK)
            parts = []
            for gg in range(groups_per_step):
                grp = c * groups_per_step + gg
                cols = slice(grp * GMLP_GROUP_DIM, (grp + 1) * GMLP_GROUP_DIM)
                s = _dot(ws_ref[grp], vb_scr[rows, cols]) + bs_ref[:, grp:grp + 1]
                parts.append(u[rows, gg * GMLP_GROUP_DIM:(gg + 1) * GMLP_GROUP_DIM] * s)
            gated_rows.append(jnp.concatenate(parts, axis=1))
        gated = jnp.concatenate(gated_rows, axis=0).astype(_BF16)
        h = h + _dot(gated, wout_ref[c * U_CHUNK:(c + 1) * U_CHUNK, :])
    out_ref[...] = h


def _gmlp(h, g, w_uv, v_gain, ws, bs_t, w_out, emit_v):
    n_tok = h.shape[0]
    tb = TOKEN_BLOCK
    row = lambda width: pl.BlockSpec((tb, width), lambda i: (i, 0))
    out_specs = [row(D_MODEL)]
    out_shape = [jax.ShapeDtypeStruct(h.shape, _F32)]
    if emit_v:
        out_specs.append(row(GMLP_HALF))
        out_shape.append(jax.ShapeDtypeStruct((n_tok, GMLP_HALF), _F32))
    return pl.pallas_call(
        functools.partial(_gmlp_kernel, emit_v=emit_v),
        grid=(n_tok // tb,),
        in_specs=[row(D_MODEL), _resident((1, D_MODEL), layer=1),
                  _resident((D_MODEL, 2 * GMLP_HALF), layer=0),
                  _resident((1, GMLP_HALF)),
                  _resident((GMLP_GROUPS, GMLP_CHUNK, GMLP_CHUNK)),
                  _resident((GMLP_CHUNK, GMLP_GROUPS)),
                  _resident((GMLP_HALF, D_MODEL), layer=0)],
        out_specs=out_specs,
        out_shape=out_shape,
        scratch_shapes=[pltpu.VMEM((tb, GMLP_HALF), _F32), pltpu.VMEM((tb, GMLP_HALF), _BF16)],
        compiler_params=_params(1),
        name="gmlp_v" if emit_v else "gmlp",
    )(h, g, w_uv, v_gain, ws, bs_t, w_out)


def _gmlp_spatial_weights(w_s, b_s, length):
    tril = jnp.tril(jnp.ones((length, length), w_s.dtype))
    ws = w_s[:, :length, :length] * tril
    reps = GMLP_CHUNK // length
    if reps > 1:
        eye = jnp.eye(reps, dtype=w_s.dtype)
        ws = jnp.einsum("ab,gij->gaibj", eye, ws).reshape(GMLP_GROUPS, GMLP_CHUNK, GMLP_CHUNK)
    bs_t = jnp.tile(b_s[:, :length].T, (reps, 1))
    return ws.astype(_BF16), bs_t


def _first_copy(kdup):
    return kdup.reshape(kdup.shape[0], N_KV_HEADS, 2, HEAD_DIM)[:, :, 0, :]


def _trunk(x, p, cache, w, batch, seq):
    ffn_ple = lambda h, layer: _ffn_ple(h, p, w["g_ffn"], w["w1"], w["w2"], w["g_ple"],
                                        w["w_gate"], w["w_proj"], layer)
    q, kdup, v = _qkv_proj(x, w["g_mix"], w["w_qkv"], w["qk_gain"])
    if cache is None:
        h = _attn_prompt(w["sinks"], x, q, kdup, v, w["w_o"], batch, seq)
    else:
        h = _attn_sample(w["sinks"], x, q, kdup, v, cache[0], cache[1], w["w_o"], seq,
                         cache[0].shape[0] // batch)
    h = ffn_ple(h, 0)
    length = min(seq, GMLP_CHUNK)
    ws, bs_t = _gmlp_spatial_weights(w["w_s"], w["b_s"], length)
    emit_v = cache is not None
    res = _gmlp(h, w["g_mix"], w["w_uv"], w["v_gain"], ws, bs_t, w["w_out"], emit_v)
    h = res[0]
    v_rows = res[1] if emit_v else None
    h = ffn_ple(h, 1)
    return h, kdup, v, v_rows


def kernel(x_prompt, x_sample, p_prompt, p_sample, cache_k, cache_v, g_mix, g_ffn, g_ple,
           attn_w_qkv, attn_q_norm, attn_k_norm, attn_sinks, attn_w_o, gmlp_w_uv, gmlp_v_norm,
           gmlp_w_s, gmlp_b_s, gmlp_w_out, ffn_w1, ffn_w2, ple_w_proj, ple_w_gate):
    batch, seq, _ = x_prompt.shape
    dec_batch, dec_seq, _ = x_sample.shape
    depth = g_mix.shape[0]
    cache_rows = cache_k.shape[2]
    assert depth == 2 and attn_w_qkv.shape[0] == 1 and gmlp_w_uv.shape[0] == 1
    assert seq % TOKEN_BLOCK == 0 and (dec_batch * dec_seq) % TOKEN_BLOCK == 0
    assert TOKEN_BLOCK % dec_seq == 0 and GMLP_CHUNK % dec_seq == 0 and dec_seq % 16 == 0
    assert cache_rows == WINDOW and cache_rows + dec_seq <= ATTN_KEYS

    scale = HEAD_DIM ** -0.5
    w_q = attn_w_qkv[0][:, :D_MODEL]
    w_k = attn_w_qkv[0][:, D_MODEL:D_MODEL + KV_DIM].reshape(D_MODEL, N_KV_HEADS, 1, HEAD_DIM)
    w_kdup = jnp.broadcast_to(w_k, (D_MODEL, N_KV_HEADS, 2, HEAD_DIM)).reshape(D_MODEL, KDUP_DIM)
    w_v = attn_w_qkv[0][:, D_MODEL + KV_DIM:]
    w = {
        "g_mix": g_mix.reshape(depth, 1, D_MODEL),
        "g_ffn": g_ffn.reshape(depth, 1, D_MODEL),
        "g_ple": g_ple.reshape(depth, 1, D_MODEL),
        "w_qkv": jnp.concatenate([w_q, w_kdup, w_v], axis=1).astype(_BF16),
        "qk_gain": jnp.concatenate([jnp.tile(attn_q_norm[0] * scale, N_HEADS),
                                    jnp.tile(attn_k_norm[0], 2 * N_KV_HEADS)]).reshape(1, QK_DIM),
        "sinks": attn_sinks[0].reshape(1, N_HEADS),
        "w_o": attn_w_o.astype(_BF16),
        "w_uv": gmlp_w_uv.astype(_BF16),
        "v_gain": gmlp_v_norm[0].reshape(1, GMLP_HALF),
        "w_s": gmlp_w_s[0],
        "b_s": gmlp_b_s[0],
        "w_out": gmlp_w_out.astype(_BF16),
        "w1": ffn_w1.astype(_BF16),
        "w2": ffn_w2.astype(_BF16),
        "w_gate": ple_w_gate.astype(_BF16),
        "w_proj": ple_w_proj.astype(_BF16),
    }

    n_prompt = batch * seq
    n_sample = dec_batch * dec_seq
    y_p, kdup_p, v_p, _ = _trunk(x_prompt.reshape(n_prompt, D_MODEL),
                                 p_prompt.reshape(depth, n_prompt, PLE_DIM), None, w, batch, seq)
    cache = (cache_k[0].reshape(dec_batch * cache_rows, KV_DIM),
             cache_v[0].reshape(dec_batch * cache_rows, KV_DIM))
    y_s, kdup_s, v_s, vrows = _trunk(x_sample.reshape(n_sample, D_MODEL),
                                     p_sample.reshape(depth, n_sample, PLE_DIM), cache, w,
                                     dec_batch, dec_seq)

    heads = (N_KV_HEADS, HEAD_DIM)
    tail_k = kdup_p.reshape(batch, seq, KDUP_DIM)[:, seq - WINDOW:].reshape(batch * WINDOW, KDUP_DIM)
    tail_v = v_p.reshape(batch, seq, KV_DIM)[:, seq - WINDOW:]
    new_k_p = _first_copy(tail_k).reshape(1, batch, WINDOW, *heads)
    new_v_p = tail_v.reshape(1, batch, WINDOW, *heads)
    new_k_s = jnp.concatenate(
        [cache_k[:, :, dec_seq:], _first_copy(kdup_s).reshape(1, dec_batch, dec_seq, *heads)], axis=2)
    new_v_s = jnp.concatenate(
        [cache_v[:, :, dec_seq:], v_s.reshape(1, dec_batch, dec_seq, *heads)], axis=2)
    return (y_p.reshape(batch, seq, D_MODEL), y_s.reshape(dec_batch, dec_seq, D_MODEL),
            new_k_p, new_v_p, new_k_s, new_v_s,
            vrows.reshape(1, dec_batch, dec_seq, GMLP_HALF))
```

```python
import functools
import math

import jax
import jax.numpy as jnp
from jax import lax
from jax.experimental import pallas as pl
from jax.experimental.pallas import tpu as pltpu

D_MODEL = 1024
HEAD_DIM = 64
N_HEADS = 16
N_KV_HEADS = 4
GQA_GROUP = N_HEADS // N_KV_HEADS
KV_DIM = N_KV_HEADS * HEAD_DIM
CHUNK = 64
WINDOW = 128
GMLP_CHUNK = 128
GMLP_HALF = 3 * D_MODEL
GMLP_GROUPS = 8
GMLP_GROUP_DIM = GMLP_HALF // GMLP_GROUPS
D_FF = 4 * D_MODEL
PLE_DIM = 256
EPS = 1e-6
NEG_INF = -1e30

LANES = 128
MXU_DIM_V7X = 256
BF16_SUBLANES = 16
VMEM_LIMIT_BYTES_V7X = 56 * 1024 * 1024

KDUP_DIM = 2 * KV_DIM
QK_DIM = D_MODEL + KDUP_DIM
QKV_DIM = QK_DIM + KV_DIM

TOKEN_BLOCK = 512
ATTN_UNIT = 2 * CHUNK
ATTN_KEYS = 2 * WINDOW
PV_ROWS = HEAD_DIM + BF16_SUBLANES
FF_CHUNK = 1024
U_CHUNK = 2 * GMLP_GROUP_DIM

_BF16 = jnp.bfloat16
_F32 = jnp.float32
_CONTRACT_LAST = (((1,), (1,)), ((), ()))


def _resident(shape, layer=None):
    if layer is None:
        return pl.BlockSpec(shape, lambda *_: (0,) * len(shape), pipeline_mode=pl.Buffered(1))
    return pl.BlockSpec((None,) + tuple(shape), lambda *_: (layer,) + (0,) * len(shape),
                        pipeline_mode=pl.Buffered(1))


def _params(n_axes):
    return pltpu.CompilerParams(
        dimension_semantics=("arbitrary",) * n_axes,
        vmem_limit_bytes=VMEM_LIMIT_BYTES_V7X,
    )


def _rms(x, g):
    ms = jnp.mean(x * x, axis=-1, keepdims=True)
    return (x * lax.rsqrt(ms + EPS)) * g


def _dot(a, b):
    return jnp.dot(a, b, preferred_element_type=_F32)


def _qkv_kernel(x_ref, g_ref, w_ref, qkg_ref, q_ref, k_ref, v_ref):
    n = _rms(x_ref[...], g_ref[...]).astype(_BF16)
    qkv = _dot(n, w_ref[...])
    r = lax.broadcasted_iota(jnp.int32, (MXU_DIM_V7X, MXU_DIM_V7X), 0) // HEAD_DIM
    c = lax.broadcasted_iota(jnp.int32, (MXU_DIM_V7X, MXU_DIM_V7X), 1) // HEAD_DIM
    seg = jnp.where(r == c, 1.0, 0.0).astype(_BF16)
    normed = []
    for j in range(QK_DIM // MXU_DIM_V7X):
        cols = slice(j * MXU_DIM_V7X, (j + 1) * MXU_DIM_V7X)
        t = qkv[:, cols]
        ss = _dot((t * t).astype(_BF16), seg)
        inv = lax.rsqrt(ss * (1.0 / HEAD_DIM) + EPS)
        normed.append((t * inv) * qkg_ref[:, cols])
    n_q = D_MODEL // MXU_DIM_V7X
    q_ref[...] = jnp.concatenate(normed[:n_q], axis=1).astype(_BF16)
    k_ref[...] = jnp.concatenate(normed[n_q:], axis=1)
    v_ref[...] = qkv[:, QK_DIM:]


def _qkv_proj(x, g, w_qkv, qk_gain):
    n_tok = x.shape[0]
    tb = TOKEN_BLOCK
    row = lambda width: pl.BlockSpec((tb, width), lambda i: (i, 0))
    return pl.pallas_call(
        _qkv_kernel,
        grid=(n_tok // tb,),
        in_specs=[row(D_MODEL), _resident((1, D_MODEL), layer=0), _resident((D_MODEL, QKV_DIM)),
                  _resident((1, QK_DIM))],
        out_specs=[row(D_MODEL), row(KDUP_DIM), row(KV_DIM)],
        out_shape=[jax.ShapeDtypeStruct((n_tok, D_MODEL), _BF16),
                   jax.ShapeDtypeStruct((n_tok, KDUP_DIM), _F32),
                   jax.ShapeDtypeStruct((n_tok, KV_DIM), _F32)],
        compiler_params=_params(1),
        name="qkv_proj",
    )(x, g, w_qkv, qk_gain)


def _alibi_slope(head):
    return 2.0 ** (-8.0 * (head + 1) / N_HEADS)


def _fill_bias_t(bias_ref, variant, valid_fn):
    s = lax.broadcasted_iota(jnp.int32, (ATTN_KEYS, ATTN_UNIT), 0)
    t = lax.broadcasted_iota(jnp.int32, (ATTN_KEYS, ATTN_UNIT), 1)
    dist = jnp.abs(t + WINDOW - s).astype(_F32)
    valid = valid_fn(t, s)
    for kvh in range(N_KV_HEADS):
        for p in range(2):
            for g in range(2):
                slope = _alibi_slope(kvh * GQA_GROUP + 2 * g + p)
                bias_ref[variant, kvh, p * ATTN_KEYS:(p + 1) * ATTN_KEYS,
                         g * ATTN_UNIT:(g + 1) * ATTN_UNIT] = jnp.where(valid, -slope * dist, NEG_INF)


def _first_copy_t(kdup_rows):
    kt = kdup_rows.T
    return jnp.concatenate(
        [kt[kvh * LANES:kvh * LANES + HEAD_DIM, :] for kvh in range(N_KV_HEADS)], axis=0)


def _attn_prompt_kernel(sinks_ref, x_ref, q_ref, k_ref, kh_ref, v_ref, vh_ref, wo_ref,
                        out_ref, nkt_ref, nvt_ref, bias_ref, klo_scr, khi_scr, vt_scr, ot_scr):
    b, i = pl.program_id(0), pl.program_id(1)
    tb = TOKEN_BLOCK

    @pl.when((b == 0) & (i == 0))
    def _():
        in_window = lambda t, s: ((t < CHUNK) & (s < WINDOW + CHUNK)) | ((t >= CHUNK) & (s >= CHUNK))
        _fill_bias_t(bias_ref, 0, in_window)
        _fill_bias_t(bias_ref, 1, lambda t, s: in_window(t, s) & (s >= WINDOW))
        vt_scr[:, HEAD_DIM:, :] = jnp.ones((N_KV_HEADS, PV_ROWS - HEAD_DIM, WINDOW + tb), _BF16)

    kd = jnp.concatenate([kh_ref[...], k_ref[...]], axis=0)
    low_half = lax.broadcasted_iota(jnp.int32, (WINDOW + tb, LANES), 1) < HEAD_DIM
    for kvh in range(N_KV_HEADS):
        kk = kd[:, kvh * LANES:(kvh + 1) * LANES]
        klo_scr[kvh] = jnp.where(low_half, kk, 0.0).astype(_BF16)
        khi_scr[kvh] = jnp.where(low_half, 0.0, kk).astype(_BF16)
    vt = jnp.concatenate([vh_ref[...], v_ref[...]], axis=0).T
    for kvh in range(N_KV_HEADS):
        vt_scr[kvh, :HEAD_DIM, :] = vt[kvh * HEAD_DIM:(kvh + 1) * HEAD_DIM, :].astype(_BF16)

    nkt_ref[...] = _first_copy_t(k_ref[tb - WINDOW:, :])
    nvt_ref[...] = vt[:, tb:]

    first_in_seq = jnp.where(i == 0, 1, 0)
    lane2 = lax.broadcasted_iota(jnp.int32, (1, 2 * ATTN_UNIT), 1)
    n_units = tb // ATTN_UNIT

    def scores(u):
        rows = slice(u * ATTN_UNIT, (u + 1) * ATTN_UNIT)
        keys = slice(u * ATTN_UNIT, u * ATTN_UNIT + ATTN_KEYS)
        variant = first_in_seq if u == 0 else 0
        out = []
        for kvh in range(N_KV_HEADS):
            k2 = jnp.concatenate([klo_scr[kvh, keys, :], khi_scr[kvh, keys, :]], axis=0)
            q2 = jnp.concatenate(
                [q_ref[rows, (2 * kvh + g) * LANES:(2 * kvh + g + 1) * LANES] for g in range(2)],
                axis=0)
            s_t = lax.dot_general(k2, q2, _CONTRACT_LAST, preferred_element_type=_F32)
            out.append(s_t + bias_ref[variant, kvh])
        return out

    s_next = scores(0)
    for u in range(n_units):
        rows = slice(u * ATTN_UNIT, (u + 1) * ATTN_UNIT)
        keys = slice(u * ATTN_UNIT, u * ATTN_UNIT + ATTN_KEYS)
        s_cur = s_next
        if u + 1 < n_units:
            s_next = scores(u + 1)
        chains = []
        for kvh in range(N_KV_HEADS):
            for p in range(2):
                head0 = kvh * GQA_GROUP + p
                sink = jnp.where(lane2 < ATTN_UNIT, sinks_ref[0, head0], sinks_ref[0, head0 + 2])
                sp = s_cur[kvh][p * ATTN_KEYS:(p + 1) * ATTN_KEYS, :]
                m = jnp.maximum(jnp.max(sp, axis=0, keepdims=True), sink)
                chains.append((head0, kvh, jnp.exp(sp - m).astype(_BF16), jnp.exp(sink - m)))
        results = [(head0, _dot(vt_scr[kvh, :, keys], e), sink_e)
                   for head0, kvh, e, sink_e in chains]
        for head0, r, sink_e in results:
            den = r[HEAD_DIM:HEAD_DIM + 1, :] + sink_e
            o = r[:HEAD_DIM, :] * (1.0 / den)
            for g in range(2):
                head = head0 + 2 * g
                ot_scr[head * HEAD_DIM:(head + 1) * HEAD_DIM, rows] = (
                    o[:, g * ATTN_UNIT:(g + 1) * ATTN_UNIT].astype(_BF16))

    out_ref[...] = x_ref[...] + _dot(ot_scr[...].T, wo_ref[...])


def _attn_prompt(sinks, x, q, kdup, v, w_o, batch, seq):
    tb = TOKEN_BLOCK
    nt = seq // tb
    halo_per_block = tb // WINDOW
    row = lambda width: pl.BlockSpec((tb, width), lambda b, i: (b * nt + i, 0))
    halo = lambda width: pl.BlockSpec(
        (WINDOW, width), lambda b, i: (jnp.maximum((b * nt + i) * halo_per_block - 1, 0), 0))
    per_seq = pl.BlockSpec((KV_DIM, WINDOW), lambda b, i: (b, 0))
    return pl.pallas_call(
        _attn_prompt_kernel,
        grid=(batch, nt),
        in_specs=[pl.BlockSpec(memory_space=pltpu.SMEM),
                  row(D_MODEL), row(D_MODEL), row(KDUP_DIM), halo(KDUP_DIM), row(KV_DIM),
                  halo(KV_DIM), _resident((D_MODEL, D_MODEL), layer=0)],
        out_specs=[row(D_MODEL), per_seq, per_seq],
        out_shape=[jax.ShapeDtypeStruct(x.shape, _F32),
                   jax.ShapeDtypeStruct((batch * KV_DIM, WINDOW), _F32),
                   jax.ShapeDtypeStruct((batch * KV_DIM, WINDOW), _F32)],
        scratch_shapes=[
            pltpu.VMEM((2, N_KV_HEADS, 2 * ATTN_KEYS, 2 * ATTN_UNIT), _F32),
            pltpu.VMEM((N_KV_HEADS, WINDOW + tb, LANES), _BF16),
            pltpu.VMEM((N_KV_HEADS, WINDOW + tb, LANES), _BF16),
            pltpu.VMEM((N_KV_HEADS, PV_ROWS, WINDOW + tb), _BF16),
            pltpu.VMEM((D_MODEL, tb), _BF16),
        ],
        compiler_params=_params(2),
        name="attn_prompt",
    )(sinks, x, q, kdup, kdup, v, v, w_o)


def _fill_bias(bias_c_ref, bias_n_ref, q_rows):
    t_c = lax.broadcasted_iota(jnp.int32, (q_rows, WINDOW), 0)
    s_c = lax.broadcasted_iota(jnp.int32, (q_rows, WINDOW), 1)
    dist_c = jnp.abs(t_c + WINDOW - s_c).astype(_F32)
    t_n = lax.broadcasted_iota(jnp.int32, (q_rows, q_rows), 0)
    s_n = lax.broadcasted_iota(jnp.int32, (q_rows, q_rows), 1)
    dist_n = jnp.abs(t_n - s_n).astype(_F32)
    for kvh in range(N_KV_HEADS):
        for j in range(GQA_GROUP):
            slope = _alibi_slope(kvh * GQA_GROUP + j)
            bias_c_ref[kvh, j * q_rows:(j + 1) * q_rows, :] = -slope * dist_c
            bias_n_ref[kvh, j * q_rows:(j + 1) * q_rows, :] = -slope * dist_n


def _attn_sample_kernel(sinks_ref, x_ref, q_ref, k_ref, ckt_ref, v_ref, cvt_ref, wo_ref,
                        out_ref, nkt_ref, nvt_ref, bias_c_ref, bias_n_ref, o_scr, *, dec_seq):
    kept = WINDOW - dec_seq

    @pl.when(pl.program_id(0) == 0)
    def _():
        _fill_bias(bias_c_ref, bias_n_ref, dec_seq)

    keep_lane = lax.broadcasted_iota(jnp.int32, (HEAD_DIM, WINDOW), 1) < kept
    pad = jnp.zeros((kept, LANES), _F32)

    def new_cols_t(rows_f32):
        return jnp.concatenate([pad, rows_f32], axis=0).T

    def one_batch(bb, carry):
        rows = pl.ds(pl.multiple_of(bb * dec_seq, dec_seq), dec_seq)
        q_u = q_ref[rows, :]
        k_new = k_ref[rows, :]
        v_new = v_ref[rows, :]
        vt_new = [new_cols_t(v_new[:, g * LANES:(g + 1) * LANES]) for g in range(KV_DIM // LANES)]
        for kvh in range(N_KV_HEADS):
            cache = pl.ds(pl.multiple_of(bb * KV_DIM + kvh * HEAD_DIM, HEAD_DIM), HEAD_DIM)
            kt_c = ckt_ref[cache, :]
            vt_c = cvt_ref[cache, :]
            k_n = k_new[:, kvh * LANES:(kvh + 1) * LANES]
            kt_n = new_cols_t(k_n)[:HEAD_DIM, :]
            half = (kvh % 2) * HEAD_DIM
            vt_n = vt_new[kvh // 2][half:half + HEAD_DIM, :]
            nkt_ref[cache, :] = jnp.where(keep_lane, pltpu.roll(kt_c, kept, axis=1), kt_n)
            nvt_ref[cache, :] = jnp.where(keep_lane, pltpu.roll(vt_c, kept, axis=1), vt_n)

            q4 = jnp.concatenate(
                [q_u[:, (kvh * GQA_GROUP + j) * HEAD_DIM:(kvh * GQA_GROUP + j + 1) * HEAD_DIM]
                 for j in range(GQA_GROUP)], axis=0)
            sink = jnp.concatenate(
                [jnp.full((dec_seq, 1), sinks_ref[0, kvh * GQA_GROUP + j], _F32)
                 for j in range(GQA_GROUP)], axis=0)
            s_c = _dot(q4, kt_c.astype(_BF16)) + bias_c_ref[kvh]
            s_n = lax.dot_general(q4, k_n[:, :HEAD_DIM].astype(_BF16), _CONTRACT_LAST,
                                  preferred_element_type=_F32) + bias_n_ref[kvh]
            m = jnp.maximum(jnp.maximum(jnp.max(s_c, axis=-1, keepdims=True),
                                        jnp.max(s_n, axis=-1, keepdims=True)), sink)
            e_c = jnp.exp(s_c - m)
            e_n = jnp.exp(s_n - m)
            den = (jnp.sum(e_c, axis=-1, keepdims=True) + jnp.sum(e_n, axis=-1, keepdims=True)
                   + jnp.exp(sink - m))
            v_n = v_new[:, kvh * HEAD_DIM:(kvh + 1) * HEAD_DIM].astype(_BF16)
            o = lax.dot_general(e_c.astype(_BF16), vt_c.astype(_BF16), _CONTRACT_LAST,
                                preferred_element_type=_F32) + _dot(e_n.astype(_BF16), v_n)
            o = o / den
            for j in range(GQA_GROUP):
                head = kvh * GQA_GROUP + j
                o_scr[rows, head * HEAD_DIM:(head + 1) * HEAD_DIM] = (
                    o[j * dec_seq:(j + 1) * dec_seq, :].astype(_BF16))
        return carry

    lax.fori_loop(0, TOKEN_BLOCK // dec_seq, one_batch, 0)
    out_ref[...] = x_ref[...] + _dot(o_scr[...], wo_ref[...])


def _attn_sample(sinks, x, q, kdup, v, cache_kt, cache_vt, w_o, dec_seq):
    n_tok = x.shape[0]
    tb = TOKEN_BLOCK
    batches_per_block = tb // dec_seq
    row = lambda width: pl.BlockSpec((tb, width), lambda i: (i, 0))
    cache = pl.BlockSpec((batches_per_block * KV_DIM, WINDOW), lambda i: (i, 0))
    return pl.pallas_call(
        functools.partial(_attn_sample_kernel, dec_seq=dec_seq),
        grid=(n_tok // tb,),
        in_specs=[pl.BlockSpec(memory_space=pltpu.SMEM),
                  row(D_MODEL), row(D_MODEL), row(KDUP_DIM), cache, row(KV_DIM), cache,
                  _resident((D_MODEL, D_MODEL), layer=0)],
        out_specs=[row(D_MODEL), cache, cache],
        out_shape=[jax.ShapeDtypeStruct(x.shape, _F32),
                   jax.ShapeDtypeStruct(cache_kt.shape, _F32),
                   jax.ShapeDtypeStruct(cache_vt.shape, _F32)],
        scratch_shapes=[
            pltpu.VMEM((N_KV_HEADS, GQA_GROUP * dec_seq, WINDOW), _F32),
            pltpu.VMEM((N_KV_HEADS, GQA_GROUP * dec_seq, dec_seq), _F32),
            pltpu.VMEM((tb, D_MODEL), _BF16),
        ],
        compiler_params=_params(1),
        name="attn_sample",
    )(sinks, x, q, kdup, cache_kt, v, cache_vt, w_o)


def _ffn_ple_kernel(h_ref, p_ref, gf_ref, w1_ref, w2_ref, gp_ref, wg_ref, wp_ref, out_ref):
    h = h_ref[...]
    n = _rms(h, gf_ref[...]).astype(_BF16)
    for c in range(D_FF // FF_CHUNK):
        cols = slice(c * FF_CHUNK, (c + 1) * FF_CHUNK)
        a = jnp.maximum(_dot(n, w1_ref[:, cols]), 0.0)
        h = h + _dot((a * a).astype(_BF16), w2_ref[cols, :])
    z = _dot(_rms(h, gp_ref[...]).astype(_BF16), wg_ref[...])
    gate = 1.0 / (1.0 + jnp.exp(-z))
    out_ref[...] = h + gate * _dot(p_ref[...].astype(_BF16), wp_ref[...])


def _ffn_ple(h, p, g_ffn, w1, w2, g_ple, w_gate, w_proj, layer):
    n_tok = h.shape[0]
    tb = TOKEN_BLOCK
    row = lambda width: pl.BlockSpec((tb, width), lambda i: (i, 0))
    return pl.pallas_call(
        _ffn_ple_kernel,
        grid=(n_tok // tb,),
        in_specs=[row(D_MODEL), pl.BlockSpec((None, tb, PLE_DIM), lambda i: (layer, i, 0)),
                  _resident((1, D_MODEL), layer),
                  _resident((D_MODEL, D_FF), layer), _resident((D_FF, D_MODEL), layer),
                  _resident((1, D_MODEL), layer), _resident((D_MODEL, D_MODEL), layer),
                  _resident((PLE_DIM, D_MODEL), layer)],
        out_specs=row(D_MODEL),
        out_shape=jax.ShapeDtypeStruct(h.shape, _F32),
        compiler_params=_params(1),
        name="ffn_ple",
    )(h, p, g_ffn, w1, w2, g_ple, w_gate, w_proj)


def _gelu_tanh(x):
    c = math.sqrt(2.0 / math.pi)
    return x * (0.5 * (1.0 + jnp.tanh(c * (x + 0.044715 * (x * x * x)))))


def _gmlp_kernel(h_ref, g_ref, wuv_ref, vg_ref, ws_ref, bs_ref, wout_ref, out_ref, *rest,
                 emit_v):
    if emit_v:
        vout_ref, v_scr, vb_scr = rest
    else:
        v_scr, vb_scr = rest
    tb = h_ref.shape[0]
    h = h_ref[...]
    n = _rms(h, g_ref[...]).astype(_BF16)

    ssq = jnp.zeros((tb, 1), _F32)
    for c in range(GMLP_HALF // FF_CHUNK):
        cols = slice(c * FF_CHUNK, (c + 1) * FF_CHUNK)
        vc = _gelu_tanh(_dot(n, wuv_ref[:, GMLP_HALF + c * FF_CHUNK:GMLP_HALF + (c + 1) * FF_CHUNK]))
        v_scr[:, cols] = vc
        ssq = ssq + jnp.sum(vc * vc, axis=-1, keepdims=True)
    inv = lax.rsqrt(ssq * (1.0 / GMLP_HALF) + EPS)
    for c in range(GMLP_HALF // FF_CHUNK):
        cols = slice(c * FF_CHUNK, (c + 1) * FF_CHUNK)
        vn = (v_scr[:, cols] * inv) * vg_ref[:, cols]
        if emit_v:
            vout_ref[:, cols] = vn
        vb_scr[:, cols] = vn.astype(_BF16)

    groups_per_step = U_CHUNK // GMLP_GROUP_DIM
    for c in range(GMLP_HALF // U_CHUNK):
        u = _gelu_tanh(_dot(n, wuv_ref[:, c * U_CHUNK:(c + 1) * U_CHUNK]))
        gated_rows = []
        for r in range(tb // GMLP_CHUNK):
            rows = slice(r * GMLP_CHUNK, (r + 1) * GMLP_CHUNK)
            parts = []
            for gg in range(groups_per_step):
                grp = c * groups_per_step + gg
                cols = slice(grp * GMLP_GROUP_DIM, (grp + 1) * GMLP_GROUP_DIM)
                s = _dot(ws_ref[grp], vb_scr[rows, cols]) + bs_ref[:, grp:grp + 1]
                parts.append(u[rows, gg * GMLP_GROUP_DIM:(gg + 1) * GMLP_GROUP_DIM] * s)
            gated_rows.append(jnp.concatenate(parts, axis=1))
        gated = jnp.concatenate(gated_rows, axis=0).astype(_BF16)
        h = h + _dot(gated, wout_ref[c * U_CHUNK:(c + 1) * U_CHUNK, :])
    out_ref[...] = h


def _gmlp(h, g, w_uv, v_gain, ws, bs_t, w_out, emit_v):
    n_tok = h.shape[0]
    tb = TOKEN_BLOCK
    row = lambda width: pl.BlockSpec((tb, width), lambda i: (i, 0))
    out_specs = [row(D_MODEL)]
    out_shape = [jax.ShapeDtypeStruct(h.shape, _F32)]
    if emit_v:
        out_specs.append(row(GMLP_HALF))
        out_shape.append(jax.ShapeDtypeStruct((n_tok, GMLP_HALF), _F32))
    return pl.pallas_call(
        functools.partial(_gmlp_kernel, emit_v=emit_v),
        grid=(n_tok // tb,),
        in_specs=[row(D_MODEL), _resident((1, D_MODEL), layer=1),
                  _resident((D_MODEL, 2 * GMLP_HALF), layer=0),
                  _resident((1, GMLP_HALF)),
                  _resident((GMLP_GROUPS, GMLP_CHUNK, GMLP_CHUNK)),
                  _resident((GMLP_CHUNK, GMLP_GROUPS)),
                  _resident((GMLP_HALF, D_MODEL), layer=0)],
        out_specs=out_specs,
        out_shape=out_shape,
        scratch_shapes=[pltpu.VMEM((tb, GMLP_HALF), _F32), pltpu.VMEM((tb, GMLP_HALF), _BF16)],
        compiler_params=_params(1),
        name="gmlp_v" if emit_v else "gmlp",
    )(h, g, w_uv, v_gain, ws, bs_t, w_out)


def _gmlp_spatial_weights(w_s, b_s, length):
    tril = jnp.tril(jnp.ones((length, length), w_s.dtype))
    ws = w_s[:, :length, :length] * tril
    reps = GMLP_CHUNK // length
    if reps > 1:
        eye = jnp.eye(reps, dtype=w_s.dtype)
        ws = jnp.einsum("ab,gij->gaibj", eye, ws).reshape(GMLP_GROUPS, GMLP_CHUNK, GMLP_CHUNK)
    bs_t = jnp.tile(b_s[:, :length].T, (reps, 1))
    return ws.astype(_BF16), bs_t


def _to_cache_t(cache):
    _, b, rows, kvh, hd = cache.shape
    return jnp.transpose(cache, (0, 1, 3, 4, 2)).reshape(b * kvh * hd, rows)


def _from_cache_t(cache_t, batch):
    t = cache_t.reshape(1, batch, N_KV_HEADS, HEAD_DIM, cache_t.shape[1])
    return jnp.transpose(t, (0, 1, 4, 2, 3))


def _trunk(x, p, cache, w, batch, seq):
    ffn_ple = lambda h, layer: _ffn_ple(h, p, w["g_ffn"], w["w1"], w["w2"], w["g_ple"],
                                        w["w_gate"], w["w_proj"], layer)
    q, kdup, v = _qkv_proj(x, w["g_mix"], w["w_qkv"], w["qk_gain"])
    if cache is None:
        h, nkt, nvt = _attn_prompt(w["sinks"], x, q, kdup, v, w["w_o"], batch, seq)
    else:
        h, nkt, nvt = _attn_sample(w["sinks"], x, q, kdup, v, cache[0], cache[1], w["w_o"], seq)
    h = ffn_ple(h, 0)
    length = min(seq, GMLP_CHUNK)
    ws, bs_t = _gmlp_spatial_weights(w["w_s"], w["b_s"], length)
    emit_v = cache is not None
    res = _gmlp(h, w["g_mix"], w["w_uv"], w["v_gain"], ws, bs_t, w["w_out"], emit_v)
    h = res[0]
    v_rows = res[1] if emit_v else None
    h = ffn_ple(h, 1)
    return h, nkt, nvt, v_rows


def kernel(x_prompt, x_sample, p_prompt, p_sample, cache_k, cache_v, g_mix, g_ffn, g_ple,
           attn_w_qkv, attn_q_norm, attn_k_norm, attn_sinks, attn_w_o, gmlp_w_uv, gmlp_v_norm,
           gmlp_w_s, gmlp_b_s, gmlp_w_out, ffn_w1, ffn_w2, ple_w_proj, ple_w_gate):
    batch, seq, _ = x_prompt.shape
    dec_batch, dec_seq, _ = x_sample.shape
    depth = g_mix.shape[0]
    cache_rows = cache_k.shape[2]
    assert depth == 2 and attn_w_qkv.shape[0] == 1 and gmlp_w_uv.shape[0] == 1
    assert seq % TOKEN_BLOCK == 0 and (dec_batch * dec_seq) % TOKEN_BLOCK == 0
    assert TOKEN_BLOCK % dec_seq == 0 and GMLP_CHUNK % dec_seq == 0 and dec_seq % 16 == 0
    assert cache_rows == WINDOW and cache_rows + dec_seq <= ATTN_KEYS

    scale = HEAD_DIM ** -0.5
    w_q = attn_w_qkv[0][:, :D_MODEL]
    w_k = attn_w_qkv[0][:, D_MODEL:D_MODEL + KV_DIM].reshape(D_MODEL, N_KV_HEADS, 1, HEAD_DIM)
    w_kdup = jnp.broadcast_to(w_k, (D_MODEL, N_KV_HEADS, 2, HEAD_DIM)).reshape(D_MODEL, KDUP_DIM)
    w_v = attn_w_qkv[0][:, D_MODEL + KV_DIM:]
    w = {
        "g_mix": g_mix.reshape(depth, 1, D_MODEL),
        "g_ffn": g_ffn.reshape(depth, 1, D_MODEL),
        "g_ple": g_ple.reshape(depth, 1, D_MODEL),
        "w_qkv": jnp.concatenate([w_q, w_kdup, w_v], axis=1).astype(_BF16),
        "qk_gain": jnp.concatenate([jnp.tile(attn_q_norm[0] * scale, N_HEADS),
                                    jnp.tile(attn_k_norm[0], 2 * N_KV_HEADS)]).reshape(1, QK_DIM),
        "sinks": attn_sinks[0].reshape(1, N_HEADS),
        "w_o": attn_w_o.astype(_BF16),
        "w_uv": gmlp_w_uv.astype(_BF16),
        "v_gain": gmlp_v_norm[0].reshape(1, GMLP_HALF),
        "w_s": gmlp_w_s[0],
        "b_s": gmlp_b_s[0],
        "w_out": gmlp_w_out.astype(_BF16),
        "w1": ffn_w1.astype(_BF16),
        "w2": ffn_w2.astype(_BF16),
        "w_gate": ple_w_gate.astype(_BF16),
        "w_proj": ple_w_proj.astype(_BF16),
    }

    n_prompt = batch * seq
    n_sample = dec_batch * dec_seq
    y_p, nkt_p, nvt_p, _ = _trunk(x_prompt.reshape(n_prompt, D_MODEL),
                                  p_prompt.reshape(depth, n_prompt, PLE_DIM), None, w, batch, seq)
    cache = (_to_cache_t(cache_k), _to_cache_t(cache_v))
    y_s, nkt_s, nvt_s, vrows = _trunk(x_sample.reshape(n_sample, D_MODEL),
                                      p_sample.reshape(depth, n_sample, PLE_DIM), cache, w,
                                      dec_batch, dec_seq)
    return (y_p.reshape(batch, seq, D_MODEL), y_s.reshape(dec_batch, dec_seq, D_MODEL),
            _from_cache_t(nkt_p, batch), _from_cache_t(nvt_p, batch),
            _from_cache_t(nkt_s, dec_batch), _from_cache_t(nvt_s, dec_batch),
            vrows.reshape(1, dec_batch, dec_seq, GMLP_HALF))
```

```python
import functools
import math

import jax
import jax.numpy as jnp
from jax import lax
from jax.experimental import pallas as pl
from jax.experimental.pallas import tpu as pltpu

D_MODEL = 1024
HEAD_DIM = 64
N_HEADS = 16
N_KV_HEADS = 4
GQA_GROUP = N_HEADS // N_KV_HEADS
KV_DIM = N_KV_HEADS * HEAD_DIM
CHUNK = 64
WINDOW = 128
GMLP_CHUNK = 128
GMLP_HALF = 3 * D_MODEL
GMLP_GROUPS = 8
GMLP_GROUP_DIM = GMLP_HALF // GMLP_GROUPS
D_FF = 4 * D_MODEL
PLE_DIM = 256
EPS = 1e-6
NEG_INF = -1e30

LANES = 128
MXU_DIM_V7X = 256
BF16_SUBLANES = 16
VMEM_LIMIT_BYTES_V7X = 56 * 1024 * 1024

KDUP_DIM = 2 * KV_DIM
QK_DIM = D_MODEL + KDUP_DIM
QKV_DIM = QK_DIM + KV_DIM

TOKEN_BLOCK = 512
ATTN_UNIT = 2 * CHUNK
ATTN_KEYS = 2 * WINDOW
PV_ROWS = HEAD_DIM + BF16_SUBLANES
FF_CHUNK = 1024
U_CHUNK = 2 * GMLP_GROUP_DIM

_BF16 = jnp.bfloat16
_F32 = jnp.float32
_CONTRACT_LAST = (((1,), (1,)), ((), ()))


def _resident(shape, layer=None):
    if layer is None:
        return pl.BlockSpec(shape, lambda *_: (0,) * len(shape), pipeline_mode=pl.Buffered(1))
    return pl.BlockSpec((None,) + tuple(shape), lambda *_: (layer,) + (0,) * len(shape),
                        pipeline_mode=pl.Buffered(1))


def _params(n_axes):
    return pltpu.CompilerParams(
        dimension_semantics=("arbitrary",) * n_axes,
        vmem_limit_bytes=VMEM_LIMIT_BYTES_V7X,
    )


def _rms(x, g):
    ms = jnp.mean(x * x, axis=-1, keepdims=True)
    return (x * lax.rsqrt(ms + EPS)) * g


def _dot(a, b):
    return jnp.dot(a, b, preferred_element_type=_F32)


class _TwoStreams:
    def __init__(self, n_prompt, n_sample):
        self.tb = TOKEN_BLOCK
        self.n_tok = (n_prompt, n_sample)
        self.n_p = n_prompt // self.tb
        self.grid = (self.n_p + n_sample // self.tb,)

    def specs(self, width, layer=None):
        n_p, tb = self.n_p, self.tb
        streams = ((lambda i: jnp.minimum(i, n_p - 1), None),
                   (lambda i: jnp.maximum(i - n_p, 0), pl.Buffered(1)))
        if layer is None:
            return [pl.BlockSpec((tb, width), lambda i, f=f: (f(i), 0), pipeline_mode=mode)
                    for f, mode in streams]
        return [pl.BlockSpec((None, tb, width), lambda i, f=f: (layer, f(i), 0), pipeline_mode=mode)
                for f, mode in streams]

    def shapes(self, width, dtype):
        return [jax.ShapeDtypeStruct((n, width), dtype) for n in self.n_tok]


def _load2(is_prompt, p_ref, s_ref):
    return jnp.where(is_prompt, p_ref[...], s_ref[...])


def _store2(is_prompt, p_ref, s_ref, value):
    @pl.when(is_prompt)
    def _():
        p_ref[...] = value

    @pl.when(jnp.logical_not(is_prompt))
    def _():
        s_ref[...] = value


def _qkv_kernel(xp_ref, xs_ref, g_ref, w_ref, qkg_ref, qp_ref, qs_ref, kp_ref, ks_ref, vp_ref,
                vs_ref, *, n_prompt_blocks):
    is_prompt = pl.program_id(0) < n_prompt_blocks
    n = _rms(_load2(is_prompt, xp_ref, xs_ref), g_ref[...]).astype(_BF16)
    qkv = _dot(n, w_ref[...])
    r = lax.broadcasted_iota(jnp.int32, (MXU_DIM_V7X, MXU_DIM_V7X), 0) // HEAD_DIM
    c = lax.broadcasted_iota(jnp.int32, (MXU_DIM_V7X, MXU_DIM_V7X), 1) // HEAD_DIM
    seg = jnp.where(r == c, 1.0, 0.0).astype(_BF16)
    normed = []
    for j in range(QK_DIM // MXU_DIM_V7X):
        cols = slice(j * MXU_DIM_V7X, (j + 1) * MXU_DIM_V7X)
        t = qkv[:, cols]
        ss = _dot((t * t).astype(_BF16), seg)
        inv = lax.rsqrt(ss * (1.0 / HEAD_DIM) + EPS)
        normed.append((t * inv) * qkg_ref[:, cols])
    n_q = D_MODEL // MXU_DIM_V7X
    _store2(is_prompt, qp_ref, qs_ref, jnp.concatenate(normed[:n_q], axis=1).astype(_BF16))
    _store2(is_prompt, kp_ref, ks_ref, jnp.concatenate(normed[n_q:], axis=1))
    _store2(is_prompt, vp_ref, vs_ref, qkv[:, QK_DIM:])


def _qkv_proj(streams, x_p, x_s, g, w_qkv, qk_gain):
    return pl.pallas_call(
        functools.partial(_qkv_kernel, n_prompt_blocks=streams.n_p),
        grid=streams.grid,
        in_specs=streams.specs(D_MODEL) + [
            _resident((1, D_MODEL), layer=0), _resident((D_MODEL, QKV_DIM)),
            _resident((1, QK_DIM))],
        out_specs=streams.specs(D_MODEL) + streams.specs(KDUP_DIM) + streams.specs(KV_DIM),
        out_shape=(streams.shapes(D_MODEL, _BF16) + streams.shapes(KDUP_DIM, _F32)
                   + streams.shapes(KV_DIM, _F32)),
        compiler_params=_params(1),
        name="qkv_proj",
    )(x_p, x_s, g, w_qkv, qk_gain)


def _alibi_slope(head):
    return 2.0 ** (-8.0 * (head + 1) / N_HEADS)


def _fill_bias_t(bias_ref, variant, valid_fn):
    s = lax.broadcasted_iota(jnp.int32, (ATTN_KEYS, ATTN_UNIT), 0)
    t = lax.broadcasted_iota(jnp.int32, (ATTN_KEYS, ATTN_UNIT), 1)
    dist = jnp.abs(t + WINDOW - s).astype(_F32)
    valid = valid_fn(t, s)
    for kvh in range(N_KV_HEADS):
        for p in range(2):
            for g in range(2):
                slope = _alibi_slope(kvh * GQA_GROUP + 2 * g + p)
                bias_ref[variant, kvh, p * ATTN_KEYS:(p + 1) * ATTN_KEYS,
                         g * ATTN_UNIT:(g + 1) * ATTN_UNIT] = jnp.where(valid, -slope * dist, NEG_INF)


def _first_copy_t(kdup_rows):
    kt = kdup_rows.T
    return jnp.concatenate(
        [kt[kvh * LANES:kvh * LANES + HEAD_DIM, :] for kvh in range(N_KV_HEADS)], axis=0)


def _attn_prompt_kernel(sinks_ref, x_ref, q_ref, k_ref, kh_ref, v_ref, vh_ref, wo_ref,
                        out_ref, nkt_ref, nvt_ref, bias_ref, klo_scr, khi_scr, vt_scr, ot_scr):
    b, i = pl.program_id(0), pl.program_id(1)
    tb = TOKEN_BLOCK

    @pl.when((b == 0) & (i == 0))
    def _():
        in_window = lambda t, s: ((t < CHUNK) & (s < WINDOW + CHUNK)) | ((t >= CHUNK) & (s >= CHUNK))
        _fill_bias_t(bias_ref, 0, in_window)
        _fill_bias_t(bias_ref, 1, lambda t, s: in_window(t, s) & (s >= WINDOW))
        vt_scr[:, HEAD_DIM:, :] = jnp.ones((N_KV_HEADS, PV_ROWS - HEAD_DIM, WINDOW + tb), _BF16)

    kd = jnp.concatenate([kh_ref[...], k_ref[...]], axis=0)
    low_half = lax.broadcasted_iota(jnp.int32, (WINDOW + tb, LANES), 1) < HEAD_DIM
    for kvh in range(N_KV_HEADS):
        kk = kd[:, kvh * LANES:(kvh + 1) * LANES]
        klo_scr[kvh] = jnp.where(low_half, kk, 0.0).astype(_BF16)
        khi_scr[kvh] = jnp.where(low_half, 0.0, kk).astype(_BF16)
    vt = jnp.concatenate([vh_ref[...], v_ref[...]], axis=0).T
    for kvh in range(N_KV_HEADS):
        vt_scr[kvh, :HEAD_DIM, :] = vt[kvh * HEAD_DIM:(kvh + 1) * HEAD_DIM, :].astype(_BF16)

    nkt_ref[...] = _first_copy_t(k_ref[tb - WINDOW:, :])
    nvt_ref[...] = vt[:, tb:]

    first_in_seq = jnp.where(i == 0, 1, 0)
    lane2 = lax.broadcasted_iota(jnp.int32, (1, 2 * ATTN_UNIT), 1)
    n_units = tb // ATTN_UNIT

    def scores(u):
        rows = slice(u * ATTN_UNIT, (u + 1) * ATTN_UNIT)
        keys = slice(u * ATTN_UNIT, u * ATTN_UNIT + ATTN_KEYS)
        variant = first_in_seq if u == 0 else 0
        out = []
        for kvh in range(N_KV_HEADS):
            k2 = jnp.concatenate([klo_scr[kvh, keys, :], khi_scr[kvh, keys, :]], axis=0)
            q2 = jnp.concatenate(
                [q_ref[rows, (2 * kvh + g) * LANES:(2 * kvh + g + 1) * LANES] for g in range(2)],
                axis=0)
            s_t = lax.dot_general(k2, q2, _CONTRACT_LAST, preferred_element_type=_F32)
            out.append(s_t + bias_ref[variant, kvh])
        return out

    s_next = scores(0)
    for u in range(n_units):
        rows = slice(u * ATTN_UNIT, (u + 1) * ATTN_UNIT)
        keys = slice(u * ATTN_UNIT, u * ATTN_UNIT + ATTN_KEYS)
        s_cur = s_next
        if u + 1 < n_units:
            s_next = scores(u + 1)
        chains = []
        for kvh in range(N_KV_HEADS):
            for p in range(2):
                head0 = kvh * GQA_GROUP + p
                sink = jnp.where(lane2 < ATTN_UNIT, sinks_ref[0, head0], sinks_ref[0, head0 + 2])
                sp = s_cur[kvh][p * ATTN_KEYS:(p + 1) * ATTN_KEYS, :]
                m = jnp.maximum(jnp.max(sp, axis=0, keepdims=True), sink)
                chains.append((head0, kvh, jnp.exp(sp - m).astype(_BF16), jnp.exp(sink - m)))
        results = [(head0, _dot(vt_scr[kvh, :, keys], e), sink_e)
                   for head0, kvh, e, sink_e in chains]
        for head0, r, sink_e in results:
            den = r[HEAD_DIM:HEAD_DIM + 1, :] + sink_e
            o = r[:HEAD_DIM, :] * (1.0 / den)
            for g in range(2):
                head = head0 + 2 * g
                ot_scr[head * HEAD_DIM:(head + 1) * HEAD_DIM, rows] = (
                    o[:, g * ATTN_UNIT:(g + 1) * ATTN_UNIT].astype(_BF16))

    out_ref[...] = x_ref[...] + _dot(ot_scr[...].T, wo_ref[...])


def _attn_prompt(sinks, x, q, kdup, v, w_o, batch, seq):
    tb = TOKEN_BLOCK
    nt = seq // tb
    halo_per_block = tb // WINDOW
    row = lambda width: pl.BlockSpec((tb, width), lambda b, i: (b * nt + i, 0))
    halo = lambda width: pl.BlockSpec(
        (WINDOW, width), lambda b, i: (jnp.maximum((b * nt + i) * halo_per_block - 1, 0), 0))
    per_seq = pl.BlockSpec((KV_DIM, WINDOW), lambda b, i: (b, 0))
    return pl.pallas_call(
        _attn_prompt_kernel,
        grid=(batch, nt),
        in_specs=[pl.BlockSpec(memory_space=pltpu.SMEM),
                  row(D_MODEL), row(D_MODEL), row(KDUP_DIM), halo(KDUP_DIM), row(KV_DIM),
                  halo(KV_DIM), _resident((D_MODEL, D_MODEL), layer=0)],
        out_specs=[row(D_MODEL), per_seq, per_seq],
        out_shape=[jax.ShapeDtypeStruct(x.shape, _F32),
                   jax.ShapeDtypeStruct((batch * KV_DIM, WINDOW), _F32),
                   jax.ShapeDtypeStruct((batch * KV_DIM, WINDOW), _F32)],
        scratch_shapes=[
            pltpu.VMEM((2, N_KV_HEADS, 2 * ATTN_KEYS, 2 * ATTN_UNIT), _F32),
            pltpu.VMEM((N_KV_HEADS, WINDOW + tb, LANES), _BF16),
            pltpu.VMEM((N_KV_HEADS, WINDOW + tb, LANES), _BF16),
            pltpu.VMEM((N_KV_HEADS, PV_ROWS, WINDOW + tb), _BF16),
            pltpu.VMEM((D_MODEL, tb), _BF16),
        ],
        compiler_params=_params(2),
        name="attn_prompt",
    )(sinks, x, q, kdup, kdup, v, v, w_o)


def _fill_bias(bias_c_ref, bias_n_ref, q_rows):
    t_c = lax.broadcasted_iota(jnp.int32, (q_rows, WINDOW), 0)
    s_c = lax.broadcasted_iota(jnp.int32, (q_rows, WINDOW), 1)
    dist_c = jnp.abs(t_c + WINDOW - s_c).astype(_F32)
    t_n = lax.broadcasted_iota(jnp.int32, (q_rows, q_rows), 0)
    s_n = lax.broadcasted_iota(jnp.int32, (q_rows, q_rows), 1)
    dist_n = jnp.abs(t_n - s_n).astype(_F32)
    for kvh in range(N_KV_HEADS):
        for j in range(GQA_GROUP):
            slope = _alibi_slope(kvh * GQA_GROUP + j)
            bias_c_ref[kvh, j * q_rows:(j + 1) * q_rows, :] = -slope * dist_c
            bias_n_ref[kvh, j * q_rows:(j + 1) * q_rows, :] = -slope * dist_n


def _attn_sample_kernel(sinks_ref, x_ref, q_ref, k_ref, ckt_ref, v_ref, cvt_ref, wo_ref,
                        out_ref, nkt_ref, nvt_ref, bias_c_ref, bias_n_ref, o_scr, *, dec_seq):
    kept = WINDOW - dec_seq

    @pl.when(pl.program_id(0) == 0)
    def _():
        _fill_bias(bias_c_ref, bias_n_ref, dec_seq)

    keep_lane = lax.broadcasted_iota(jnp.int32, (HEAD_DIM, WINDOW), 1) < kept
    pad = jnp.zeros((kept, LANES), _F32)

    def new_cols_t(rows_f32):
        return jnp.concatenate([pad, rows_f32], axis=0).T

    def one_batch(bb, carry):
        rows = pl.ds(pl.multiple_of(bb * dec_seq, dec_seq), dec_seq)
        q_u = q_ref[rows, :]
        k_new = k_ref[rows, :]
        v_new = v_ref[rows, :]
        vt_new = [new_cols_t(v_new[:, g * LANES:(g + 1) * LANES]) for g in range(KV_DIM // LANES)]
        for kvh in range(N_KV_HEADS):
            cache = pl.ds(pl.multiple_of(bb * KV_DIM + kvh * HEAD_DIM, HEAD_DIM), HEAD_DIM)
            kt_c = ckt_ref[cache, :]
            vt_c = cvt_ref[cache, :]
            k_n = k_new[:, kvh * LANES:(kvh + 1) * LANES]
            kt_n = new_cols_t(k_n)[:HEAD_DIM, :]
            half = (kvh % 2) * HEAD_DIM
            vt_n = vt_new[kvh // 2][half:half + HEAD_DIM, :]
            nkt_ref[cache, :] = jnp.where(keep_lane, pltpu.roll(kt_c, kept, axis=1), kt_n)
            nvt_ref[cache, :] = jnp.where(keep_lane, pltpu.roll(vt_c, kept, axis=1), vt_n)

            q4 = jnp.concatenate(
                [q_u[:, (kvh * GQA_GROUP + j) * HEAD_DIM:(kvh * GQA_GROUP + j + 1) * HEAD_DIM]
                 for j in range(GQA_GROUP)], axis=0)
            sink = jnp.concatenate(
                [jnp.full((dec_seq, 1), sinks_ref[0, kvh * GQA_GROUP + j], _F32)
                 for j in range(GQA_GROUP)], axis=0)
            s_c = _dot(q4, kt_c.astype(_BF16)) + bias_c_ref[kvh]
            s_n = lax.dot_general(q4, k_n[:, :HEAD_DIM].astype(_BF16), _CONTRACT_LAST,
                                  preferred_element_type=_F32) + bias_n_ref[kvh]
            m = jnp.maximum(jnp.maximum(jnp.max(s_c, axis=-1, keepdims=True),
                                        jnp.max(s_n, axis=-1, keepdims=True)), sink)
            e_c = jnp.exp(s_c - m)
            e_n = jnp.exp(s_n - m)
            den = (jnp.sum(e_c, axis=-1, keepdims=True) + jnp.sum(e_n, axis=-1, keepdims=True)
                   + jnp.exp(sink - m))
            v_n = v_new[:, kvh * HEAD_DIM:(kvh + 1) * HEAD_DIM].astype(_BF16)
            o = lax.dot_general(e_c.astype(_BF16), vt_c.astype(_BF16), _CONTRACT_LAST,
                                preferred_element_type=_F32) + _dot(e_n.astype(_BF16), v_n)
            o = o / den
            for j in range(GQA_GROUP):
                head = kvh * GQA_GROUP + j
                o_scr[rows, head * HEAD_DIM:(head + 1) * HEAD_DIM] = (
                    o[j * dec_seq:(j + 1) * dec_seq, :].astype(_BF16))
        return carry

    lax.fori_loop(0, TOKEN_BLOCK // dec_seq, one_batch, 0)
    out_ref[...] = x_ref[...] + _dot(o_scr[...], wo_ref[...])


def _attn_sample(sinks, x, q, kdup, v, cache_kt, cache_vt, w_o, dec_seq):
    n_tok = x.shape[0]
    tb = TOKEN_BLOCK
    batches_per_block = tb // dec_seq
    row = lambda width: pl.BlockSpec((tb, width), lambda i: (i, 0))
    cache = pl.BlockSpec((batches_per_block * KV_DIM, WINDOW), lambda i: (i, 0))
    return pl.pallas_call(
        functools.partial(_attn_sample_kernel, dec_seq=dec_seq),
        grid=(n_tok // tb,),
        in_specs=[pl.BlockSpec(memory_space=pltpu.SMEM),
                  row(D_MODEL), row(D_MODEL), row(KDUP_DIM), cache, row(KV_DIM), cache,
                  _resident((D_MODEL, D_MODEL), layer=0)],
        out_specs=[row(D_MODEL), cache, cache],
        out_shape=[jax.ShapeDtypeStruct(x.shape, _F32),
                   jax.ShapeDtypeStruct(cache_kt.shape, _F32),
                   jax.ShapeDtypeStruct(cache_vt.shape, _F32)],
        scratch_shapes=[
            pltpu.VMEM((N_KV_HEADS, GQA_GROUP * dec_seq, WINDOW), _F32),
            pltpu.VMEM((N_KV_HEADS, GQA_GROUP * dec_seq, dec_seq), _F32),
            pltpu.VMEM((tb, D_MODEL), _BF16),
        ],
        compiler_params=_params(1),
        name="attn_sample",
    )(sinks, x, q, kdup, cache_kt, v, cache_vt, w_o)


def _ffn_ple_kernel(hp_ref, hs_ref, pp_ref, ps_ref, gf_ref, w1_ref, w2_ref, gp_ref, wg_ref, wp_ref,
                    outp_ref, outs_ref, *, n_prompt_blocks):
    is_prompt = pl.program_id(0) < n_prompt_blocks
    h = _load2(is_prompt, hp_ref, hs_ref)
    n = _rms(h, gf_ref[...]).astype(_BF16)
    for c in range(D_FF // FF_CHUNK):
        cols = slice(c * FF_CHUNK, (c + 1) * FF_CHUNK)
        a = jnp.maximum(_dot(n, w1_ref[:, cols]), 0.0)
        h = h + _dot((a * a).astype(_BF16), w2_ref[cols, :])
    z = _dot(_rms(h, gp_ref[...]).astype(_BF16), wg_ref[...])
    gate = 1.0 / (1.0 + jnp.exp(-z))
    p = _load2(is_prompt, pp_ref, ps_ref).astype(_BF16)
    _store2(is_prompt, outp_ref, outs_ref, h + gate * _dot(p, wp_ref[...]))


def _ffn_ple(streams, h_p, h_s, p_p, p_s, g_ffn, w1, w2, g_ple, w_gate, w_proj, layer):
    return pl.pallas_call(
        functools.partial(_ffn_ple_kernel, n_prompt_blocks=streams.n_p),
        grid=streams.grid,
        in_specs=streams.specs(D_MODEL) + streams.specs(PLE_DIM, layer) + [
            _resident((1, D_MODEL), layer),
            _resident((D_MODEL, D_FF), layer), _resident((D_FF, D_MODEL), layer),
            _resident((1, D_MODEL), layer), _resident((D_MODEL, D_MODEL), layer),
            _resident((PLE_DIM, D_MODEL), layer)],
        out_specs=streams.specs(D_MODEL),
        out_shape=streams.shapes(D_MODEL, _F32),
        compiler_params=_params(1),
        name="ffn_ple",
    )(h_p, h_s, p_p, p_s, g_ffn, w1, w2, g_ple, w_gate, w_proj)


def _gelu_tanh(x):
    c = math.sqrt(2.0 / math.pi)
    return x * (0.5 * (1.0 + jnp.tanh(c * (x + 0.044715 * (x * x * x)))))


def _gmlp_kernel(hp_ref, hs_ref, g_ref, wuv_ref, vg_ref, ws_ref, bs_ref, wout_ref,
                 outp_ref, outs_ref, vout_ref, v_scr, vb_scr, *, n_prompt_blocks):
    is_prompt = pl.program_id(0) < n_prompt_blocks
    variant = jnp.where(is_prompt, 0, 1)
    tb = hp_ref.shape[0]
    h = _load2(is_prompt, hp_ref, hs_ref)
    n = _rms(h, g_ref[...]).astype(_BF16)

    ssq = jnp.zeros((tb, 1), _F32)
    for c in range(GMLP_HALF // FF_CHUNK):
        cols = slice(c * FF_CHUNK, (c + 1) * FF_CHUNK)
        vc = _gelu_tanh(_dot(n, wuv_ref[:, GMLP_HALF + c * FF_CHUNK:GMLP_HALF + (c + 1) * FF_CHUNK]))
        v_scr[:, cols] = vc
        ssq = ssq + jnp.sum(vc * vc, axis=-1, keepdims=True)
    inv = lax.rsqrt(ssq * (1.0 / GMLP_HALF) + EPS)
    for c in range(GMLP_HALF // FF_CHUNK):
        cols = slice(c * FF_CHUNK, (c + 1) * FF_CHUNK)
        vn = (v_scr[:, cols] * inv) * vg_ref[:, cols]
        vb_scr[:, cols] = vn.astype(_BF16)

    @pl.when(jnp.logical_not(is_prompt))
    def _():
        vout_ref[...] = (v_scr[...] * inv) * vg_ref[...]

    groups_per_step = U_CHUNK // GMLP_GROUP_DIM
    for c in range(GMLP_HALF // U_CHUNK):
        u = _gelu_tanh(_dot(n, wuv_ref[:, c * U_CHUNK:(c + 1) * U_CHUNK]))
        gated_rows = []
        for r in range(tb // GMLP_CHUNK):
            rows = slice(r * GMLP_CHUNK, (r + 1) * GMLP_CHUNK)
            parts = []
            for gg in range(groups_per_step):
                grp = c * groups_per_step + gg
                cols = slice(grp * GMLP_GROUP_DIM, (grp + 1) * GMLP_GROUP_DIM)
                bias = bs_ref[variant][:, grp:grp + 1]
                s = _dot(ws_ref[variant, grp], vb_scr[rows, cols]) + bias
                parts.append(u[rows, gg * GMLP_GROUP_DIM:(gg + 1) * GMLP_GROUP_DIM] * s)
            gated_rows.append(jnp.concatenate(parts, axis=1))
        gated = jnp.concatenate(gated_rows, axis=0).astype(_BF16)
        h = h + _dot(gated, wout_ref[c * U_CHUNK:(c + 1) * U_CHUNK, :])
    _store2(is_prompt, outp_ref, outs_ref, h)


def _gmlp(streams, h_p, h_s, g, w_uv, v_gain, ws, bs_t, w_out):
    return pl.pallas_call(
        functools.partial(_gmlp_kernel, n_prompt_blocks=streams.n_p),
        grid=streams.grid,
        in_specs=streams.specs(D_MODEL) + [
            _resident((1, D_MODEL), layer=1),
            _resident((D_MODEL, 2 * GMLP_HALF), layer=0),
            _resident((1, GMLP_HALF)),
            _resident((2, GMLP_GROUPS, GMLP_CHUNK, GMLP_CHUNK)),
            _resident((2, GMLP_CHUNK, GMLP_GROUPS)),
            _resident((GMLP_HALF, D_MODEL), layer=0)],
        out_specs=streams.specs(D_MODEL) + streams.specs(GMLP_HALF)[1:],
        out_shape=streams.shapes(D_MODEL, _F32) + streams.shapes(GMLP_HALF, _F32)[1:],
        scratch_shapes=[pltpu.VMEM((streams.tb, GMLP_HALF), _F32),
                        pltpu.VMEM((streams.tb, GMLP_HALF), _BF16)],
        compiler_params=_params(1),
        name="gmlp",
    )(h_p, h_s, g, w_uv, v_gain, ws, bs_t, w_out)


def _gmlp_spatial_weights(w_s, b_s, length):
    tril = jnp.tril(jnp.ones((length, length), w_s.dtype))
    ws = w_s[:, :length, :length] * tril
    reps = GMLP_CHUNK // length
    if reps > 1:
        eye = jnp.eye(reps, dtype=w_s.dtype)
        ws = jnp.einsum("ab,gij->gaibj", eye, ws).reshape(GMLP_GROUPS, GMLP_CHUNK, GMLP_CHUNK)
    bs_t = jnp.tile(b_s[:, :length].T, (reps, 1))
    return ws.astype(_BF16), bs_t


def _to_cache_t(cache):
    _, b, rows, kvh, hd = cache.shape
    return jnp.transpose(cache, (0, 1, 3, 4, 2)).reshape(b * kvh * hd, rows)


def _from_cache_t(cache_t, batch):
    t = cache_t.reshape(1, batch, N_KV_HEADS, HEAD_DIM, cache_t.shape[1])
    return jnp.transpose(t, (0, 1, 4, 2, 3))


def kernel(x_prompt, x_sample, p_prompt, p_sample, cache_k, cache_v, g_mix, g_ffn, g_ple,
           attn_w_qkv, attn_q_norm, attn_k_norm, attn_sinks, attn_w_o, gmlp_w_uv, gmlp_v_norm,
           gmlp_w_s, gmlp_b_s, gmlp_w_out, ffn_w1, ffn_w2, ple_w_proj, ple_w_gate):
    batch, seq, _ = x_prompt.shape
    dec_batch, dec_seq, _ = x_sample.shape
    depth = g_mix.shape[0]
    cache_rows = cache_k.shape[2]
    assert depth == 2 and attn_w_qkv.shape[0] == 1 and gmlp_w_uv.shape[0] == 1
    assert seq % TOKEN_BLOCK == 0 and (dec_batch * dec_seq) % TOKEN_BLOCK == 0
    assert TOKEN_BLOCK % dec_seq == 0 and GMLP_CHUNK % dec_seq == 0 and dec_seq % 16 == 0
    assert cache_rows == WINDOW and cache_rows + dec_seq <= ATTN_KEYS

    scale = HEAD_DIM ** -0.5
    w_q = attn_w_qkv[0][:, :D_MODEL]
    w_k = attn_w_qkv[0][:, D_MODEL:D_MODEL + KV_DIM].reshape(D_MODEL, N_KV_HEADS, 1, HEAD_DIM)
    w_kdup = jnp.broadcast_to(w_k, (D_MODEL, N_KV_HEADS, 2, HEAD_DIM)).reshape(D_MODEL, KDUP_DIM)
    w_v = attn_w_qkv[0][:, D_MODEL + KV_DIM:]
    w = {
        "g_mix": g_mix.reshape(depth, 1, D_MODEL),
        "g_ffn": g_ffn.reshape(depth, 1, D_MODEL),
        "g_ple": g_ple.reshape(depth, 1, D_MODEL),
        "w_qkv": jnp.concatenate([w_q, w_kdup, w_v], axis=1).astype(_BF16),
        "qk_gain": jnp.concatenate([jnp.tile(attn_q_norm[0] * scale, N_HEADS),
                                    jnp.tile(attn_k_norm[0], 2 * N_KV_HEADS)]).reshape(1, QK_DIM),
        "sinks": attn_sinks[0].reshape(1, N_HEADS),
        "w_o": attn_w_o.astype(_BF16),
        "w_uv": gmlp_w_uv.astype(_BF16),
        "v_gain": gmlp_v_norm[0].reshape(1, GMLP_HALF),
        "w_s": gmlp_w_s[0],
        "b_s": gmlp_b_s[0],
        "w_out": gmlp_w_out.astype(_BF16),
        "w1": ffn_w1.astype(_BF16),
        "w2": ffn_w2.astype(_BF16),
        "w_gate": ple_w_gate.astype(_BF16),
        "w_proj": ple_w_proj.astype(_BF16),
    }

    n_prompt = batch * seq
    n_sample = dec_batch * dec_seq
    streams = _TwoStreams(n_prompt, n_sample)
    x_p = x_prompt.reshape(n_prompt, D_MODEL)
    x_s = x_sample.reshape(n_sample, D_MODEL)
    p_p = p_prompt.reshape(depth, n_prompt, PLE_DIM)
    p_s = p_sample.reshape(depth, n_sample, PLE_DIM)
    ffn_ple = lambda h_p, h_s, layer: _ffn_ple(
        streams, h_p, h_s, p_p, p_s, w["g_ffn"], w["w1"], w["w2"], w["g_ple"], w["w_gate"],
        w["w_proj"], layer)

    q_p, q_s, kdup_p, kdup_s, v_p, v_s = _qkv_proj(streams, x_p, x_s, w["g_mix"], w["w_qkv"],
                                                   w["qk_gain"])
    h_p, nkt_p, nvt_p = _attn_prompt(w["sinks"], x_p, q_p, kdup_p, v_p, w["w_o"], batch, seq)
    h_s, nkt_s, nvt_s = _attn_sample(w["sinks"], x_s, q_s, kdup_s, v_s, _to_cache_t(cache_k),
                                     _to_cache_t(cache_v), w["w_o"], dec_seq)
    h_p, h_s = ffn_ple(h_p, h_s, 0)

    ws_p, bs_p = _gmlp_spatial_weights(w["w_s"], w["b_s"], min(seq, GMLP_CHUNK))
    ws_s, bs_s = _gmlp_spatial_weights(w["w_s"], w["b_s"], min(dec_seq, GMLP_CHUNK))
    h_p, h_s, vrows = _gmlp(streams, h_p, h_s, w["g_mix"], w["w_uv"], w["v_gain"],
                            jnp.stack([ws_p, ws_s]), jnp.stack([bs_p, bs_s]), w["w_out"])
    y_p, y_s = ffn_ple(h_p, h_s, 1)
    return (y_p.reshape(batch, seq, D_MODEL), y_s.reshape(dec_batch, dec_seq, D_MODEL),
            _from_cache_t(nkt_p, batch), _from_cache_t(nvt_p, batch),
            _from_cache_t(nkt_s, dec_batch), _from_cache_t(nvt_s, dec_batch),
            vrows.reshape(1, dec_batch, dec_seq, GMLP_HALF))
```

```python
import functools
import math

import jax
import jax.numpy as jnp
from jax import lax
from jax.experimental import pallas as pl
from jax.experimental.pallas import tpu as pltpu

D_MODEL = 1024
HEAD_DIM = 64
N_HEADS = 16
N_KV_HEADS = 4
GQA_GROUP = N_HEADS // N_KV_HEADS
KV_DIM = N_KV_HEADS * HEAD_DIM
QK_DIM = D_MODEL + KV_DIM
QKV_DIM = QK_DIM + KV_DIM
CHUNK = 64
WINDOW = 128
GMLP_CHUNK = 128
GMLP_HALF = 3 * D_MODEL
GMLP_GROUPS = 8
GMLP_GROUP_DIM = GMLP_HALF // GMLP_GROUPS
D_FF = 4 * D_MODEL
PLE_DIM = 256
EPS = 1e-6
NEG_INF = -1e30

LANES = 128
MXU_DIM_V7X = 256
BF16_SUBLANES = 16
VMEM_LIMIT_BYTES_V7X = 56 * 1024 * 1024

TOKEN_BLOCK = 512
ATTN_UNIT = 2 * CHUNK
ATTN_KEYS = 2 * WINDOW
PV_ROWS = HEAD_DIM + BF16_SUBLANES
FF_CHUNK = 1024
U_CHUNK = 2 * GMLP_GROUP_DIM

_BF16 = jnp.bfloat16
_F32 = jnp.float32
_CONTRACT_LAST = (((1,), (1,)), ((), ()))


def _resident(shape, layer=None):
    if layer is None:
        return pl.BlockSpec(shape, lambda *_: (0,) * len(shape), pipeline_mode=pl.Buffered(1))
    return pl.BlockSpec((None,) + tuple(shape), lambda *_: (layer,) + (0,) * len(shape),
                        pipeline_mode=pl.Buffered(1))


def _params(n_axes):
    return pltpu.CompilerParams(
        dimension_semantics=("arbitrary",) * n_axes,
        vmem_limit_bytes=VMEM_LIMIT_BYTES_V7X,
    )


def _rms(x, g):
    ms = jnp.mean(x * x, axis=-1, keepdims=True)
    return (x * lax.rsqrt(ms + EPS)) * g


def _dot(a, b):
    return jnp.dot(a, b, preferred_element_type=_F32)


def _qkv_kernel(x_ref, g_ref, w_ref, qkg_ref, q_ref, k_ref, v_ref):
    n = _rms(x_ref[...], g_ref[...]).astype(_BF16)
    qkv = _dot(n, w_ref[...])
    r = lax.broadcasted_iota(jnp.int32, (MXU_DIM_V7X, MXU_DIM_V7X), 0) // HEAD_DIM
    c = lax.broadcasted_iota(jnp.int32, (MXU_DIM_V7X, MXU_DIM_V7X), 1) // HEAD_DIM
    seg = jnp.where(r == c, 1.0, 0.0).astype(_BF16)
    normed = []
    for j in range(QK_DIM // MXU_DIM_V7X):
        cols = slice(j * MXU_DIM_V7X, (j + 1) * MXU_DIM_V7X)
        t = qkv[:, cols]
        ss = _dot((t * t).astype(_BF16), seg)
        inv = lax.rsqrt(ss * (1.0 / HEAD_DIM) + EPS)
        normed.append((t * inv) * qkg_ref[:, cols])
    q_ref[...] = jnp.concatenate(normed[:-1], axis=1).astype(_BF16)
    k_ref[...] = normed[-1]
    v_ref[...] = qkv[:, QK_DIM:]


def _qkv_proj(x, g, w_qkv, qk_gain):
    n_tok = x.shape[0]
    tb = TOKEN_BLOCK
    row = lambda width: pl.BlockSpec((tb, width), lambda i: (i, 0))
    return pl.pallas_call(
        _qkv_kernel,
        grid=(n_tok // tb,),
        in_specs=[row(D_MODEL), _resident((1, D_MODEL), layer=0),
                  _resident((D_MODEL, QKV_DIM), layer=0), _resident((1, QK_DIM))],
        out_specs=[row(D_MODEL), row(KV_DIM), row(KV_DIM)],
        out_shape=[jax.ShapeDtypeStruct((n_tok, D_MODEL), _BF16),
                   jax.ShapeDtypeStruct((n_tok, KV_DIM), _F32),
                   jax.ShapeDtypeStruct((n_tok, KV_DIM), _F32)],
        compiler_params=_params(1),
        name="qkv_proj",
    )(x, g, w_qkv, qk_gain)


def _alibi_slope(head):
    return 2.0 ** (-8.0 * (head + 1) / N_HEADS)


def _fill_bias_t(bias_ref, variant, valid_fn):
    s = lax.broadcasted_iota(jnp.int32, (ATTN_KEYS, ATTN_UNIT), 0)
    t = lax.broadcasted_iota(jnp.int32, (ATTN_KEYS, ATTN_UNIT), 1)
    dist = jnp.abs(t + WINDOW - s).astype(_F32)
    valid = valid_fn(t, s)
    for kvh in range(N_KV_HEADS):
        for p in range(2):
            for g in range(2):
                slope = _alibi_slope(kvh * GQA_GROUP + 2 * g + p)
                bias_ref[variant, kvh, p * ATTN_KEYS:(p + 1) * ATTN_KEYS,
                         g * ATTN_UNIT:(g + 1) * ATTN_UNIT] = jnp.where(valid, -slope * dist, NEG_INF)


def _attn_prompt_kernel(sinks_ref, x_ref, q_ref, k_ref, kh_ref, v_ref, vh_ref, wo_ref,
                        out_ref, nkt_ref, nvt_ref, bias_ref, klo_scr, khi_scr, vt_scr, ot_scr):
    b, i = pl.program_id(0), pl.program_id(1)
    tb = TOKEN_BLOCK

    @pl.when((b == 0) & (i == 0))
    def _():
        in_window = lambda t, s: ((t < CHUNK) & (s < WINDOW + CHUNK)) | ((t >= CHUNK) & (s >= CHUNK))
        _fill_bias_t(bias_ref, 0, in_window)
        _fill_bias_t(bias_ref, 1, lambda t, s: in_window(t, s) & (s >= WINDOW))
        vt_scr[:, HEAD_DIM:, :] = jnp.ones((N_KV_HEADS, PV_ROWS - HEAD_DIM, WINDOW + tb), _BF16)

    kd = jnp.concatenate([kh_ref[...], k_ref[...]], axis=0)
    low_half = lax.broadcasted_iota(jnp.int32, (WINDOW + tb, LANES), 1) < HEAD_DIM
    for pair in range(N_KV_HEADS // 2):
        kk = kd[:, pair * LANES:(pair + 1) * LANES]
        swapped = pltpu.roll(kk, HEAD_DIM, axis=1)
        klo_scr[2 * pair] = jnp.where(low_half, kk, 0.0).astype(_BF16)
        khi_scr[2 * pair] = jnp.where(low_half, 0.0, swapped).astype(_BF16)
        klo_scr[2 * pair + 1] = jnp.where(low_half, swapped, 0.0).astype(_BF16)
        khi_scr[2 * pair + 1] = jnp.where(low_half, 0.0, kk).astype(_BF16)
    vt = jnp.concatenate([vh_ref[...], v_ref[...]], axis=0).T
    for kvh in range(N_KV_HEADS):
        vt_scr[kvh, :HEAD_DIM, :] = vt[kvh * HEAD_DIM:(kvh + 1) * HEAD_DIM, :].astype(_BF16)

    nkt_ref[...] = k_ref[tb - WINDOW:, :].T
    nvt_ref[...] = vt[:, tb:]

    first_in_seq = jnp.where(i == 0, 1, 0)
    lane2 = lax.broadcasted_iota(jnp.int32, (1, 2 * ATTN_UNIT), 1)
    n_units = tb // ATTN_UNIT

    def scores(u):
        rows = slice(u * ATTN_UNIT, (u + 1) * ATTN_UNIT)
        keys = slice(u * ATTN_UNIT, u * ATTN_UNIT + ATTN_KEYS)
        variant = first_in_seq if u == 0 else 0
        out = []
        for kvh in range(N_KV_HEADS):
            k2 = jnp.concatenate([klo_scr[kvh, keys, :], khi_scr[kvh, keys, :]], axis=0)
            q2 = jnp.concatenate(
                [q_ref[rows, (2 * kvh + g) * LANES:(2 * kvh + g + 1) * LANES] for g in range(2)],
                axis=0)
            s_t = lax.dot_general(k2, q2, _CONTRACT_LAST, preferred_element_type=_F32)
            out.append(s_t + bias_ref[variant, kvh])
        return out

    s_next = scores(0)
    for u in range(n_units):
        rows = slice(u * ATTN_UNIT, (u + 1) * ATTN_UNIT)
        keys = slice(u * ATTN_UNIT, u * ATTN_UNIT + ATTN_KEYS)
        s_cur = s_next
        if u + 1 < n_units:
            s_next = scores(u + 1)
        chains = []
        for kvh in range(N_KV_HEADS):
            for p in range(2):
                head0 = kvh * GQA_GROUP + p
                sink = jnp.where(lane2 < ATTN_UNIT, sinks_ref[0, head0], sinks_ref[0, head0 + 2])
                sp = s_cur[kvh][p * ATTN_KEYS:(p + 1) * ATTN_KEYS, :]
                m = jnp.maximum(jnp.max(sp, axis=0, keepdims=True), sink)
                chains.append((head0, kvh, jnp.exp(sp - m).astype(_BF16), jnp.exp(sink - m)))
        results = [(head0, _dot(vt_scr[kvh, :, keys], e), sink_e)
                   for head0, kvh, e, sink_e in chains]
        for head0, r, sink_e in results:
            den = r[HEAD_DIM:HEAD_DIM + 1, :] + sink_e
            o = r[:HEAD_DIM, :] * (1.0 / den)
            for g in range(2):
                head = head0 + 2 * g
                ot_scr[head * HEAD_DIM:(head + 1) * HEAD_DIM, rows] = (
                    o[:, g * ATTN_UNIT:(g + 1) * ATTN_UNIT].astype(_BF16))

    out_ref[...] = x_ref[...] + _dot(ot_scr[...].T, wo_ref[...])


def _attn_prompt(sinks, x, q, k, v, w_o, batch, seq):
    tb = TOKEN_BLOCK
    nt = seq // tb
    halo_per_block = tb // WINDOW
    row = lambda width: pl.BlockSpec((tb, width), lambda b, i: (b * nt + i, 0))
    halo = pl.BlockSpec(
        (WINDOW, KV_DIM), lambda b, i: (jnp.maximum((b * nt + i) * halo_per_block - 1, 0), 0))
    per_seq = pl.BlockSpec((KV_DIM, WINDOW), lambda b, i: (b, 0))
    return pl.pallas_call(
        _attn_prompt_kernel,
        grid=(batch, nt),
        in_specs=[pl.BlockSpec(memory_space=pltpu.SMEM),
                  row(D_MODEL), row(D_MODEL), row(KV_DIM), halo, row(KV_DIM), halo,
                  _resident((D_MODEL, D_MODEL), layer=0)],
        out_specs=[row(D_MODEL), per_seq, per_seq],
        out_shape=[jax.ShapeDtypeStruct(x.shape, _F32),
                   jax.ShapeDtypeStruct((batch * KV_DIM, WINDOW), _F32),
                   jax.ShapeDtypeStruct((batch * KV_DIM, WINDOW), _F32)],
        scratch_shapes=[
            pltpu.VMEM((2, N_KV_HEADS, 2 * ATTN_KEYS, 2 * ATTN_UNIT), _F32),
            pltpu.VMEM((N_KV_HEADS, WINDOW + tb, LANES), _BF16),
            pltpu.VMEM((N_KV_HEADS, WINDOW + tb, LANES), _BF16),
            pltpu.VMEM((N_KV_HEADS, PV_ROWS, WINDOW + tb), _BF16),
            pltpu.VMEM((D_MODEL, tb), _BF16),
        ],
        compiler_params=_params(2),
        name="attn_prompt",
    )(sinks, x, q, k, k, v, v, w_o)


def _fill_bias(bias_c_ref, bias_n_ref, q_rows):
    t_c = lax.broadcasted_iota(jnp.int32, (q_rows, WINDOW), 0)
    s_c = lax.broadcasted_iota(jnp.int32, (q_rows, WINDOW), 1)
    dist_c = jnp.abs(t_c + WINDOW - s_c).astype(_F32)
    t_n = lax.broadcasted_iota(jnp.int32, (q_rows, q_rows), 0)
    s_n = lax.broadcasted_iota(jnp.int32, (q_rows, q_rows), 1)
    dist_n = jnp.abs(t_n - s_n).astype(_F32)
    for kvh in range(N_KV_HEADS):
        for j in range(GQA_GROUP):
            slope = _alibi_slope(kvh * GQA_GROUP + j)
            bias_c_ref[kvh, j * q_rows:(j + 1) * q_rows, :] = -slope * dist_c
            bias_n_ref[kvh, j * q_rows:(j + 1) * q_rows, :] = -slope * dist_n


def _attn_sample_kernel(sinks_ref, x_ref, q_ref, k_ref, ckt_ref, v_ref, cvt_ref, wo_ref,
                        out_ref, nkt_ref, nvt_ref, bias_c_ref, bias_n_ref, o_scr, *, dec_seq):
    kept = WINDOW - dec_seq

    @pl.when(pl.program_id(0) == 0)
    def _():
        _fill_bias(bias_c_ref, bias_n_ref, dec_seq)

    keep_lane = lax.broadcasted_iota(jnp.int32, (HEAD_DIM, WINDOW), 1) < kept
    pad = jnp.zeros((kept, LANES), _F32)

    def new_cols_t(rows_f32):
        return jnp.concatenate([pad, rows_f32], axis=0).T

    def one_batch(bb, carry):
        rows = pl.ds(pl.multiple_of(bb * dec_seq, dec_seq), dec_seq)
        q_u = q_ref[rows, :]
        k_new = k_ref[rows, :]
        v_new = v_ref[rows, :]
        k_new_b = k_new.astype(_BF16)
        v_new_b = v_new.astype(_BF16)
        kt_new = [new_cols_t(k_new[:, g * LANES:(g + 1) * LANES]) for g in range(KV_DIM // LANES)]
        vt_new = [new_cols_t(v_new[:, g * LANES:(g + 1) * LANES]) for g in range(KV_DIM // LANES)]

        staged = []
        for kvh in range(N_KV_HEADS):
            cache = pl.ds(pl.multiple_of(bb * KV_DIM + kvh * HEAD_DIM, HEAD_DIM), HEAD_DIM)
            head_cols = slice(kvh * HEAD_DIM, (kvh + 1) * HEAD_DIM)
            kt_c = ckt_ref[cache, :]
            vt_c = cvt_ref[cache, :]
            half = slice((kvh % 2) * HEAD_DIM, (kvh % 2 + 1) * HEAD_DIM)
            nkt_ref[cache, :] = jnp.where(keep_lane, pltpu.roll(kt_c, kept, axis=1),
                                          kt_new[kvh // 2][half, :])
            nvt_ref[cache, :] = jnp.where(keep_lane, pltpu.roll(vt_c, kept, axis=1),
                                          vt_new[kvh // 2][half, :])
            q4 = jnp.concatenate(
                [q_u[:, (kvh * GQA_GROUP + j) * HEAD_DIM:(kvh * GQA_GROUP + j + 1) * HEAD_DIM]
                 for j in range(GQA_GROUP)], axis=0)
            s_c = _dot(q4, kt_c.astype(_BF16)) + bias_c_ref[kvh]
            s_n = lax.dot_general(q4, k_new_b[:, head_cols], _CONTRACT_LAST,
                                  preferred_element_type=_F32) + bias_n_ref[kvh]
            staged.append((s_c, s_n, vt_c.astype(_BF16), v_new_b[:, head_cols]))

        soft = []
        for kvh, (s_c, s_n, vt_c, v_n) in enumerate(staged):
            sink = jnp.concatenate(
                [jnp.full((dec_seq, 1), sinks_ref[0, kvh * GQA_GROUP + j], _F32)
                 for j in range(GQA_GROUP)], axis=0)
            m = jnp.maximum(jnp.maximum(jnp.max(s_c, axis=-1, keepdims=True),
                                        jnp.max(s_n, axis=-1, keepdims=True)), sink)
            e_c = jnp.exp(s_c - m)
            e_n = jnp.exp(s_n - m)
            den = (jnp.sum(e_c, axis=-1, keepdims=True) + jnp.sum(e_n, axis=-1, keepdims=True)
                   + jnp.exp(sink - m))
            soft.append((e_c.astype(_BF16), e_n.astype(_BF16), den, vt_c, v_n))

        outs = [(lax.dot_general(e_c, vt_c, _CONTRACT_LAST, preferred_element_type=_F32)
                 + _dot(e_n, v_n), den) for e_c, e_n, den, vt_c, v_n in soft]
        for kvh, (o, den) in enumerate(outs):
            o = o / den
            for j in range(GQA_GROUP):
                head = kvh * GQA_GROUP + j
                o_scr[rows, head * HEAD_DIM:(head + 1) * HEAD_DIM] = (
                    o[j * dec_seq:(j + 1) * dec_seq, :].astype(_BF16))
        return carry

    lax.fori_loop(0, TOKEN_BLOCK // dec_seq, one_batch, 0, unroll=2)
    out_ref[...] = x_ref[...] + _dot(o_scr[...], wo_ref[...])


def _attn_sample(sinks, x, q, k, v, cache_kt, cache_vt, w_o, dec_seq):
    n_tok = x.shape[0]
    tb = TOKEN_BLOCK
    batches_per_block = tb // dec_seq
    row = lambda width: pl.BlockSpec((tb, width), lambda i: (i, 0))
    cache = pl.BlockSpec((batches_per_block * KV_DIM, WINDOW), lambda i: (i, 0))
    return pl.pallas_call(
        functools.partial(_attn_sample_kernel, dec_seq=dec_seq),
        grid=(n_tok // tb,),
        in_specs=[pl.BlockSpec(memory_space=pltpu.SMEM),
                  row(D_MODEL), row(D_MODEL), row(KV_DIM), cache, row(KV_DIM), cache,
                  _resident((D_MODEL, D_MODEL), layer=0)],
        out_specs=[row(D_MODEL), cache, cache],
        out_shape=[jax.ShapeDtypeStruct(x.shape, _F32),
                   jax.ShapeDtypeStruct(cache_kt.shape, _F32),
                   jax.ShapeDtypeStruct(cache_vt.shape, _F32)],
        scratch_shapes=[
            pltpu.VMEM((N_KV_HEADS, GQA_GROUP * dec_seq, WINDOW), _F32),
            pltpu.VMEM((N_KV_HEADS, GQA_GROUP * dec_seq, dec_seq), _F32),
            pltpu.VMEM((tb, D_MODEL), _BF16),
        ],
        compiler_params=_params(1),
        name="attn_sample",
    )(sinks, x, q, k, cache_kt, v, cache_vt, w_o)


def _ffn_ple_kernel(h_ref, p_ref, gf_ref, w1_ref, w2_ref, gp_ref, wg_ref, wp_ref, out_ref):
    h = h_ref[...]
    n = _rms(h, gf_ref[...]).astype(_BF16)
    for c in range(D_FF // FF_CHUNK):
        cols = slice(c * FF_CHUNK, (c + 1) * FF_CHUNK)
        a = jnp.maximum(_dot(n, w1_ref[:, cols]), 0.0)
        h = h + _dot((a * a).astype(_BF16), w2_ref[cols, :])
    z = _dot(_rms(h, gp_ref[...]).astype(_BF16), wg_ref[...])
    gate = 1.0 / (1.0 + jnp.exp(-z))
    out_ref[...] = h + gate * _dot(p_ref[...].astype(_BF16), wp_ref[...])


def _ffn_ple(h, p, g_ffn, w1, w2, g_ple, w_gate, w_proj, layer):
    n_tok = h.shape[0]
    tb = TOKEN_BLOCK
    row = lambda width: pl.BlockSpec((tb, width), lambda i: (i, 0))
    return pl.pallas_call(
        _ffn_ple_kernel,
        grid=(n_tok // tb,),
        in_specs=[row(D_MODEL), pl.BlockSpec((None, tb, PLE_DIM), lambda i: (layer, i, 0)),
                  _resident((1, D_MODEL), layer),
                  _resident((D_MODEL, D_FF), layer), _resident((D_FF, D_MODEL), layer),
                  _resident((1, D_MODEL), layer), _resident((D_MODEL, D_MODEL), layer),
                  _resident((PLE_DIM, D_MODEL), layer)],
        out_specs=row(D_MODEL),
        out_shape=jax.ShapeDtypeStruct(h.shape, _F32),
        compiler_params=_params(1),
        name="ffn_ple",
    )(h, p, g_ffn, w1, w2, g_ple, w_gate, w_proj)


def _gelu_tanh(x):
    c = math.sqrt(2.0 / math.pi)
    return x * (0.5 * (1.0 + jnp.tanh(c * (x + 0.044715 * (x * x * x)))))


def _gmlp_kernel(h_ref, g_ref, wuv_ref, vg_ref, ws_ref, bs_ref, wout_ref, out_ref, *rest,
                 emit_v):
    if emit_v:
        vout_ref, v_scr, vb_scr = rest
    else:
        v_scr, vb_scr = rest
    tb = h_ref.shape[0]
    h = h_ref[...]
    n = _rms(h, g_ref[...]).astype(_BF16)

    ssq = jnp.zeros((tb, 1), _F32)
    for c in range(GMLP_HALF // FF_CHUNK):
        cols = slice(c * FF_CHUNK, (c + 1) * FF_CHUNK)
        vc = _gelu_tanh(_dot(n, wuv_ref[:, GMLP_HALF + c * FF_CHUNK:GMLP_HALF + (c + 1) * FF_CHUNK]))
        v_scr[:, cols] = vc
        ssq = ssq + jnp.sum(vc * vc, axis=-1, keepdims=True)
    inv = lax.rsqrt(ssq * (1.0 / GMLP_HALF) + EPS)
    for c in range(GMLP_HALF // FF_CHUNK):
        cols = slice(c * FF_CHUNK, (c + 1) * FF_CHUNK)
        vn = (v_scr[:, cols] * inv) * vg_ref[:, cols]
        if emit_v:
            vout_ref[:, cols] = vn
        vb_scr[:, cols] = vn.astype(_BF16)

    groups_per_step = U_CHUNK // GMLP_GROUP_DIM
    for c in range(GMLP_HALF // U_CHUNK):
        u = _gelu_tanh(_dot(n, wuv_ref[:, c * U_CHUNK:(c + 1) * U_CHUNK]))
        gated_rows = []
        for r in range(tb // GMLP_CHUNK):
            rows = slice(r * GMLP_CHUNK, (r + 1) * GMLP_CHUNK)
            parts = []
            for gg in range(groups_per_step):
                grp = c * groups_per_step + gg
                cols = slice(grp * GMLP_GROUP_DIM, (grp + 1) * GMLP_GROUP_DIM)
                s = _dot(ws_ref[grp], vb_scr[rows, cols]) + bs_ref[:, grp:grp + 1]
                parts.append(u[rows, gg * GMLP_GROUP_DIM:(gg + 1) * GMLP_GROUP_DIM] * s)
            gated_rows.append(jnp.concatenate(parts, axis=1))
        gated = jnp.concatenate(gated_rows, axis=0).astype(_BF16)
        h = h + _dot(gated, wout_ref[c * U_CHUNK:(c + 1) * U_CHUNK, :])
    out_ref[...] = h


def _gmlp(h, g, w_uv, v_gain, ws, bs_t, w_out, emit_v):
    n_tok = h.shape[0]
    tb = TOKEN_BLOCK
    row = lambda width: pl.BlockSpec((tb, width), lambda i: (i, 0))
    out_specs = [row(D_MODEL)]
    out_shape = [jax.ShapeDtypeStruct(h.shape, _F32)]
    if emit_v:
        out_specs.append(row(GMLP_HALF))
        out_shape.append(jax.ShapeDtypeStruct((n_tok, GMLP_HALF), _F32))
    return pl.pallas_call(
        functools.partial(_gmlp_kernel, emit_v=emit_v),
        grid=(n_tok // tb,),
        in_specs=[row(D_MODEL), _resident((1, D_MODEL), layer=1),
                  _resident((D_MODEL, 2 * GMLP_HALF), layer=0),
                  _resident((1, GMLP_HALF)),
                  _resident((GMLP_GROUPS, GMLP_CHUNK, GMLP_CHUNK)),
                  _resident((GMLP_CHUNK, GMLP_GROUPS)),
                  _resident((GMLP_HALF, D_MODEL), layer=0)],
        out_specs=out_specs,
        out_shape=out_shape,
        scratch_shapes=[pltpu.VMEM((tb, GMLP_HALF), _F32), pltpu.VMEM((tb, GMLP_HALF), _BF16)],
        compiler_params=_params(1),
        name="gmlp_v" if emit_v else "gmlp",
    )(h, g, w_uv, v_gain, ws, bs_t, w_out)


def _gmlp_spatial_weights(w_s, b_s, length):
    tril = jnp.tril(jnp.ones((length, length), w_s.dtype))
    ws = w_s[:, :length, :length] * tril
    reps = GMLP_CHUNK // length
    if reps > 1:
        eye = jnp.eye(reps, dtype=w_s.dtype)
        ws = jnp.einsum("ab,gij->gaibj", eye, ws).reshape(GMLP_GROUPS, GMLP_CHUNK, GMLP_CHUNK)
    bs_t = jnp.tile(b_s[:, :length].T, (reps, 1))
    return ws.astype(_BF16), bs_t


def _to_cache_t(cache):
    _, b, rows, kvh, hd = cache.shape
    return jnp.transpose(cache, (0, 1, 3, 4, 2)).reshape(b * kvh * hd, rows)


def _from_cache_t(cache_t, batch):
    t = cache_t.reshape(1, batch, N_KV_HEADS, HEAD_DIM, cache_t.shape[1])
    return jnp.transpose(t, (0, 1, 4, 2, 3))


def _trunk(x, p, cache, w, batch, seq):
    ffn_ple = lambda h, layer: _ffn_ple(h, p, w["g_ffn"], w["w1"], w["w2"], w["g_ple"],
                                        w["w_gate"], w["w_proj"], layer)
    q, k, v = _qkv_proj(x, w["g_mix"], w["w_qkv"], w["qk_gain"])
    if cache is None:
        h, nkt, nvt = _attn_prompt(w["sinks"], x, q, k, v, w["w_o"], batch, seq)
    else:
        h, nkt, nvt = _attn_sample(w["sinks"], x, q, k, v, cache[0], cache[1], w["w_o"], seq)
    h = ffn_ple(h, 0)
    length = min(seq, GMLP_CHUNK)
    ws, bs_t = _gmlp_spatial_weights(w["w_s"], w["b_s"], length)
    emit_v = cache is not None
    res = _gmlp(h, w["g_mix"], w["w_uv"], w["v_gain"], ws, bs_t, w["w_out"], emit_v)
    h = res[0]
    v_rows = res[1] if emit_v else None
    h = ffn_ple(h, 1)
    return h, nkt, nvt, v_rows


def kernel(x_prompt, x_sample, p_prompt, p_sample, cache_k, cache_v, g_mix, g_ffn, g_ple,
           attn_w_qkv, attn_q_norm, attn_k_norm, attn_sinks, attn_w_o, gmlp_w_uv, gmlp_v_norm,
           gmlp_w_s, gmlp_b_s, gmlp_w_out, ffn_w1, ffn_w2, ple_w_proj, ple_w_gate):
    batch, seq, _ = x_prompt.shape
    dec_batch, dec_seq, _ = x_sample.shape
    depth = g_mix.shape[0]
    cache_rows = cache_k.shape[2]
    assert depth == 2 and attn_w_qkv.shape[0] == 1 and gmlp_w_uv.shape[0] == 1
    assert seq % TOKEN_BLOCK == 0 and (dec_batch * dec_seq) % TOKEN_BLOCK == 0
    assert TOKEN_BLOCK % (2 * dec_seq) == 0 and GMLP_CHUNK % dec_seq == 0 and dec_seq % 16 == 0
    assert cache_rows == WINDOW and dec_seq <= WINDOW

    scale = HEAD_DIM ** -0.5
    w = {
        "g_mix": g_mix.reshape(depth, 1, D_MODEL),
        "g_ffn": g_ffn.reshape(depth, 1, D_MODEL),
        "g_ple": g_ple.reshape(depth, 1, D_MODEL),
        "w_qkv": attn_w_qkv.astype(_BF16),
        "qk_gain": jnp.concatenate([jnp.tile(attn_q_norm[0] * scale, N_HEADS),
                                    jnp.tile(attn_k_norm[0], N_KV_HEADS)]).reshape(1, QK_DIM),
        "sinks": attn_sinks[0].reshape(1, N_HEADS),
        "w_o": attn_w_o.astype(_BF16),
        "w_uv": gmlp_w_uv.astype(_BF16),
        "v_gain": gmlp_v_norm[0].reshape(1, GMLP_HALF),
        "w_s": gmlp_w_s[0],
        "b_s": gmlp_b_s[0],
        "w_out": gmlp_w_out.astype(_BF16),
        "w1": ffn_w1.astype(_BF16),
        "w2": ffn_w2.astype(_BF16),
        "w_gate": ple_w_gate.astype(_BF16),
        "w_proj": ple_w_proj.astype(_BF16),
    }

    n_prompt = batch * seq
    n_sample = dec_batch * dec_seq
    y_p, nkt_p, nvt_p, _ = _trunk(x_prompt.reshape(n_prompt, D_MODEL),
                                  p_prompt.reshape(depth, n_prompt, PLE_DIM), None, w, batch, seq)
    cache = (_to_cache_t(cache_k), _to_cache_t(cache_v))
    y_s, nkt_s, nvt_s, vrows = _trunk(x_sample.reshape(n_sample, D_MODEL),
                                      p_sample.reshape(depth, n_sample, PLE_DIM), cache, w,
                                      dec_batch, dec_seq)
    return (y_p.reshape(batch, seq, D_MODEL), y_s.reshape(dec_batch, dec_seq, D_MODEL),
            _from_cache_t(nkt_p, batch), _from_cache_t(nvt_p, batch),
            _from_cache_t(nkt_s, dec_batch), _from_cache_t(nvt_s, dec_batch),
            vrows.reshape(1, dec_batch, dec_seq, GMLP_HALF))
```

```python
import functools
import math

import jax
import jax.numpy as jnp
from jax import lax
from jax.experimental import pallas as pl
from jax.experimental.pallas import tpu as pltpu

D_MODEL = 1024
HEAD_DIM = 64
N_HEADS = 16
N_KV_HEADS = 4
GQA_GROUP = N_HEADS // N_KV_HEADS
KV_DIM = N_KV_HEADS * HEAD_DIM
QK_DIM = D_MODEL + KV_DIM
QKV_DIM = QK_DIM + KV_DIM
CHUNK = 64
WINDOW = 128
GMLP_CHUNK = 128
GMLP_HALF = 3 * D_MODEL
GMLP_GROUPS = 8
GMLP_GROUP_DIM = GMLP_HALF // GMLP_GROUPS
D_FF = 4 * D_MODEL
PLE_DIM = 256
EPS = 1e-6
NEG_INF = -1e30

LANES = 128
MXU_DIM_V7X = 256
BF16_SUBLANES = 16
VMEM_LIMIT_BYTES_V7X = 56 * 1024 * 1024

TOKEN_BLOCK = 512
FFN_TOKEN_BLOCK = 2 * TOKEN_BLOCK
GMLP_ROW_GROUPS = 1
ATTN_UNIT = 2 * CHUNK
ATTN_KEYS = 2 * WINDOW
PV_ROWS = HEAD_DIM + BF16_SUBLANES
FF_CHUNK = 1024
U_CHUNK = 2 * GMLP_GROUP_DIM

_BF16 = jnp.bfloat16
_F32 = jnp.float32
_CONTRACT_LAST = (((1,), (1,)), ((), ()))


def _resident(shape, layer=None):
    if layer is None:
        return pl.BlockSpec(shape, lambda *_: (0,) * len(shape), pipeline_mode=pl.Buffered(1))
    return pl.BlockSpec((None,) + tuple(shape), lambda *_: (layer,) + (0,) * len(shape),
                        pipeline_mode=pl.Buffered(1))


def _params(n_axes):
    return pltpu.CompilerParams(
        dimension_semantics=("arbitrary",) * n_axes,
        vmem_limit_bytes=VMEM_LIMIT_BYTES_V7X,
    )


def _rms(x, g):
    ms = jnp.mean(x * x, axis=-1, keepdims=True)
    return (x * lax.rsqrt(ms + EPS)) * g


def _dot(a, b):
    return jnp.dot(a, b, preferred_element_type=_F32)


def _qkv_kernel(x_ref, g_ref, w_ref, qkg_ref, q_ref, k_ref, v_ref):
    n = _rms(x_ref[...], g_ref[...]).astype(_BF16)
    qkv = _dot(n, w_ref[...])
    r = lax.broadcasted_iota(jnp.int32, (MXU_DIM_V7X, MXU_DIM_V7X), 0) // HEAD_DIM
    c = lax.broadcasted_iota(jnp.int32, (MXU_DIM_V7X, MXU_DIM_V7X), 1) // HEAD_DIM
    seg = jnp.where(r == c, 1.0, 0.0).astype(_BF16)
    normed = []
    for j in range(QK_DIM // MXU_DIM_V7X):
        cols = slice(j * MXU_DIM_V7X, (j + 1) * MXU_DIM_V7X)
        t = qkv[:, cols]
        ss = _dot((t * t).astype(_BF16), seg)
        inv = lax.rsqrt(ss * (1.0 / HEAD_DIM) + EPS)
        normed.append((t * inv) * qkg_ref[:, cols])
    q_ref[...] = jnp.concatenate(normed[:-1], axis=1).astype(_BF16)
    k_ref[...] = normed[-1]
    v_ref[...] = qkv[:, QK_DIM:]


def _qkv_proj(x, g, w_qkv, qk_gain):
    n_tok = x.shape[0]
    tb = TOKEN_BLOCK
    row = lambda width: pl.BlockSpec((tb, width), lambda i: (i, 0))
    return pl.pallas_call(
        _qkv_kernel,
        grid=(n_tok // tb,),
        in_specs=[row(D_MODEL), _resident((1, D_MODEL), layer=0),
                  _resident((D_MODEL, QKV_DIM), layer=0), _resident((1, QK_DIM))],
        out_specs=[row(D_MODEL), row(KV_DIM), row(KV_DIM)],
        out_shape=[jax.ShapeDtypeStruct((n_tok, D_MODEL), _BF16),
                   jax.ShapeDtypeStruct((n_tok, KV_DIM), _F32),
                   jax.ShapeDtypeStruct((n_tok, KV_DIM), _F32)],
        compiler_params=_params(1),
        name="qkv_proj",
    )(x, g, w_qkv, qk_gain)


def _alibi_slope(head):
    return 2.0 ** (-8.0 * (head + 1) / N_HEADS)


def _fill_bias_t(bias_ref, variant, valid_fn):
    s = lax.broadcasted_iota(jnp.int32, (ATTN_KEYS, ATTN_UNIT), 0)
    t = lax.broadcasted_iota(jnp.int32, (ATTN_KEYS, ATTN_UNIT), 1)
    dist = jnp.abs(t + WINDOW - s).astype(_F32)
    valid = valid_fn(t, s)
    for kvh in range(N_KV_HEADS):
        for p in range(2):
            for g in range(2):
                slope = _alibi_slope(kvh * GQA_GROUP + 2 * g + p)
                bias_ref[variant, kvh, p * ATTN_KEYS:(p + 1) * ATTN_KEYS,
                         g * ATTN_UNIT:(g + 1) * ATTN_UNIT] = jnp.where(valid, -slope * dist, NEG_INF)


def _attn_prompt_kernel(sinks_ref, x_ref, q_ref, k_ref, kh_ref, v_ref, vh_ref, wo_ref,
                        out_ref, nkt_ref, nvt_ref, bias_ref, klo_scr, khi_scr, vt_scr, ot_scr):
    b, i = pl.program_id(0), pl.program_id(1)
    tb = TOKEN_BLOCK

    @pl.when((b == 0) & (i == 0))
    def _():
        in_window = lambda t, s: ((t < CHUNK) & (s < WINDOW + CHUNK)) | ((t >= CHUNK) & (s >= CHUNK))
        _fill_bias_t(bias_ref, 0, in_window)
        _fill_bias_t(bias_ref, 1, lambda t, s: in_window(t, s) & (s >= WINDOW))
        vt_scr[:, HEAD_DIM:, :] = jnp.ones((N_KV_HEADS, PV_ROWS - HEAD_DIM, WINDOW + tb), _BF16)

    kd = jnp.concatenate([kh_ref[...], k_ref[...]], axis=0)
    low_half = lax.broadcasted_iota(jnp.int32, (WINDOW + tb, LANES), 1) < HEAD_DIM
    for pair in range(N_KV_HEADS // 2):
        kk = kd[:, pair * LANES:(pair + 1) * LANES]
        swapped = pltpu.roll(kk, HEAD_DIM, axis=1)
        klo_scr[2 * pair] = jnp.where(low_half, kk, 0.0).astype(_BF16)
        khi_scr[2 * pair] = jnp.where(low_half, 0.0, swapped).astype(_BF16)
        klo_scr[2 * pair + 1] = jnp.where(low_half, swapped, 0.0).astype(_BF16)
        khi_scr[2 * pair + 1] = jnp.where(low_half, 0.0, kk).astype(_BF16)
    vt = jnp.concatenate([vh_ref[...], v_ref[...]], axis=0).T
    for kvh in range(N_KV_HEADS):
        vt_scr[kvh, :HEAD_DIM, :] = vt[kvh * HEAD_DIM:(kvh + 1) * HEAD_DIM, :].astype(_BF16)

    nkt_ref[...] = k_ref[tb - WINDOW:, :].T
    nvt_ref[...] = vt[:, tb:]

    first_in_seq = jnp.where(i == 0, 1, 0)
    lane2 = lax.broadcasted_iota(jnp.int32, (1, 2 * ATTN_UNIT), 1)
    n_units = tb // ATTN_UNIT

    def scores(u):
        rows = slice(u * ATTN_UNIT, (u + 1) * ATTN_UNIT)
        keys = slice(u * ATTN_UNIT, u * ATTN_UNIT + ATTN_KEYS)
        variant = first_in_seq if u == 0 else 0
        out = []
        for kvh in range(N_KV_HEADS):
            k2 = jnp.concatenate([klo_scr[kvh, keys, :], khi_scr[kvh, keys, :]], axis=0)
            q2 = jnp.concatenate(
                [q_ref[rows, (2 * kvh + g) * LANES:(2 * kvh + g + 1) * LANES] for g in range(2)],
                axis=0)
            s_t = lax.dot_general(k2, q2, _CONTRACT_LAST, preferred_element_type=_F32)
            out.append(s_t + bias_ref[variant, kvh])
        return out

    s_next = scores(0)
    for u in range(n_units):
        rows = slice(u * ATTN_UNIT, (u + 1) * ATTN_UNIT)
        keys = slice(u * ATTN_UNIT, u * ATTN_UNIT + ATTN_KEYS)
        s_cur = s_next
        if u + 1 < n_units:
            s_next = scores(u + 1)
        chains = []
        for kvh in range(N_KV_HEADS):
            for p in range(2):
                head0 = kvh * GQA_GROUP + p
                sink = jnp.where(lane2 < ATTN_UNIT, sinks_ref[0, head0], sinks_ref[0, head0 + 2])
                sp = s_cur[kvh][p * ATTN_KEYS:(p + 1) * ATTN_KEYS, :]
                m = jnp.maximum(jnp.max(sp, axis=0, keepdims=True), sink)
                chains.append((head0, kvh, jnp.exp(sp - m).astype(_BF16), jnp.exp(sink - m)))
        results = [(head0, _dot(vt_scr[kvh, :, keys], e), sink_e)
                   for head0, kvh, e, sink_e in chains]
        for head0, r, sink_e in results:
            den = r[HEAD_DIM:HEAD_DIM + 1, :] + sink_e
            o = r[:HEAD_DIM, :] * (1.0 / den)
            for g in range(2):
                head = head0 + 2 * g
                ot_scr[head * HEAD_DIM:(head + 1) * HEAD_DIM, rows] = (
                    o[:, g * ATTN_UNIT:(g + 1) * ATTN_UNIT].astype(_BF16))

    out_ref[...] = x_ref[...] + _dot(ot_scr[...].T, wo_ref[...])


def _attn_prompt(sinks, x, q, k, v, w_o, batch, seq):
    tb = TOKEN_BLOCK
    nt = seq // tb
    halo_per_block = tb // WINDOW
    row = lambda width: pl.BlockSpec((tb, width), lambda b, i: (b * nt + i, 0))
    halo = pl.BlockSpec(
        (WINDOW, KV_DIM), lambda b, i: (jnp.maximum((b * nt + i) * halo_per_block - 1, 0), 0))
    per_seq = pl.BlockSpec((KV_DIM, WINDOW), lambda b, i: (b, 0))
    return pl.pallas_call(
        _attn_prompt_kernel,
        grid=(batch, nt),
        in_specs=[pl.BlockSpec(memory_space=pltpu.SMEM),
                  row(D_MODEL), row(D_MODEL), row(KV_DIM), halo, row(KV_DIM), halo,
                  _resident((D_MODEL, D_MODEL), layer=0)],
        out_specs=[row(D_MODEL), per_seq, per_seq],
        out_shape=[jax.ShapeDtypeStruct(x.shape, _F32),
                   jax.ShapeDtypeStruct((batch * KV_DIM, WINDOW), _F32),
                   jax.ShapeDtypeStruct((batch * KV_DIM, WINDOW), _F32)],
        scratch_shapes=[
            pltpu.VMEM((2, N_KV_HEADS, 2 * ATTN_KEYS, 2 * ATTN_UNIT), _F32),
            pltpu.VMEM((N_KV_HEADS, WINDOW + tb, LANES), _BF16),
            pltpu.VMEM((N_KV_HEADS, WINDOW + tb, LANES), _BF16),
            pltpu.VMEM((N_KV_HEADS, PV_ROWS, WINDOW + tb), _BF16),
            pltpu.VMEM((D_MODEL, tb), _BF16),
        ],
        compiler_params=_params(2),
        name="attn_prompt",
    )(sinks, x, q, k, k, v, v, w_o)


def _fill_bias(bias_c_ref, bias_n_ref, q_rows):
    t_c = lax.broadcasted_iota(jnp.int32, (q_rows, WINDOW), 0)
    s_c = lax.broadcasted_iota(jnp.int32, (q_rows, WINDOW), 1)
    dist_c = jnp.abs(t_c + WINDOW - s_c).astype(_F32)
    t_n = lax.broadcasted_iota(jnp.int32, (q_rows, q_rows), 0)
    s_n = lax.broadcasted_iota(jnp.int32, (q_rows, q_rows), 1)
    dist_n = jnp.abs(t_n - s_n).astype(_F32)
    for kvh in range(N_KV_HEADS):
        for j in range(GQA_GROUP):
            slope = _alibi_slope(kvh * GQA_GROUP + j)
            bias_c_ref[kvh, j * q_rows:(j + 1) * q_rows, :] = -slope * dist_c
            bias_n_ref[kvh, j * q_rows:(j + 1) * q_rows, :] = -slope * dist_n


def _attn_sample_kernel(sinks_ref, x_ref, q_ref, k_ref, ckt_ref, v_ref, cvt_ref, wo_ref,
                        out_ref, nkt_ref, nvt_ref, bias_c_ref, bias_n_ref, o_scr, *, dec_seq):
    kept = WINDOW - dec_seq

    @pl.when(pl.program_id(0) == 0)
    def _():
        _fill_bias(bias_c_ref, bias_n_ref, dec_seq)

    keep_lane = lax.broadcasted_iota(jnp.int32, (HEAD_DIM, WINDOW), 1) < kept
    pad = jnp.zeros((kept, LANES), _F32)

    def new_cols_t(rows_f32):
        return jnp.concatenate([pad, rows_f32], axis=0).T

    def one_batch(bb, carry):
        rows = pl.ds(pl.multiple_of(bb * dec_seq, dec_seq), dec_seq)
        q_u = q_ref[rows, :]
        k_new = k_ref[rows, :]
        v_new = v_ref[rows, :]
        k_new_b = k_new.astype(_BF16)
        v_new_b = v_new.astype(_BF16)
        kt_new = [new_cols_t(k_new[:, g * LANES:(g + 1) * LANES]) for g in range(KV_DIM // LANES)]
        vt_new = [new_cols_t(v_new[:, g * LANES:(g + 1) * LANES]) for g in range(KV_DIM // LANES)]

        staged = []
        for kvh in range(N_KV_HEADS):
            cache = pl.ds(pl.multiple_of(bb * KV_DIM + kvh * HEAD_DIM, HEAD_DIM), HEAD_DIM)
            head_cols = slice(kvh * HEAD_DIM, (kvh + 1) * HEAD_DIM)
            kt_c = ckt_ref[cache, :]
            vt_c = cvt_ref[cache, :]
            half = slice((kvh % 2) * HEAD_DIM, (kvh % 2 + 1) * HEAD_DIM)
            nkt_ref[cache, :] = jnp.where(keep_lane, pltpu.roll(kt_c, kept, axis=1),
                                          kt_new[kvh // 2][half, :])
            nvt_ref[cache, :] = jnp.where(keep_lane, pltpu.roll(vt_c, kept, axis=1),
                                          vt_new[kvh // 2][half, :])
            q4 = jnp.concatenate(
                [q_u[:, (kvh * GQA_GROUP + j) * HEAD_DIM:(kvh * GQA_GROUP + j + 1) * HEAD_DIM]
                 for j in range(GQA_GROUP)], axis=0)
            s_c = _dot(q4, kt_c.astype(_BF16)) + bias_c_ref[kvh]
            s_n = lax.dot_general(q4, k_new_b[:, head_cols], _CONTRACT_LAST,
                                  preferred_element_type=_F32) + bias_n_ref[kvh]
            staged.append((s_c, s_n, vt_c.astype(_BF16), v_new_b[:, head_cols]))

        soft = []
        for kvh, (s_c, s_n, vt_c, v_n) in enumerate(staged):
            sink = jnp.concatenate(
                [jnp.full((dec_seq, 1), sinks_ref[0, kvh * GQA_GROUP + j], _F32)
                 for j in range(GQA_GROUP)], axis=0)
            m = jnp.maximum(jnp.maximum(jnp.max(s_c, axis=-1, keepdims=True),
                                        jnp.max(s_n, axis=-1, keepdims=True)), sink)
            e_c = jnp.exp(s_c - m)
            e_n = jnp.exp(s_n - m)
            den = (jnp.sum(e_c, axis=-1, keepdims=True) + jnp.sum(e_n, axis=-1, keepdims=True)
                   + jnp.exp(sink - m))
            soft.append((e_c.astype(_BF16), e_n.astype(_BF16), den, vt_c, v_n))

        outs = [(lax.dot_general(e_c, vt_c, _CONTRACT_LAST, preferred_element_type=_F32)
                 + _dot(e_n, v_n), den) for e_c, e_n, den, vt_c, v_n in soft]
        for kvh, (o, den) in enumerate(outs):
            o = o / den
            for j in range(GQA_GROUP):
                head = kvh * GQA_GROUP + j
                o_scr[rows, head * HEAD_DIM:(head + 1) * HEAD_DIM] = (
                    o[j * dec_seq:(j + 1) * dec_seq, :].astype(_BF16))
        return carry

    lax.fori_loop(0, TOKEN_BLOCK // dec_seq, one_batch, 0, unroll=2)
    out_ref[...] = x_ref[...] + _dot(o_scr[...], wo_ref[...])


def _attn_sample(sinks, x, q, k, v, cache_kt, cache_vt, w_o, dec_seq):
    n_tok = x.shape[0]
    tb = TOKEN_BLOCK
    batches_per_block = tb // dec_seq
    row = lambda width: pl.BlockSpec((tb, width), lambda i: (i, 0))
    cache = pl.BlockSpec((batches_per_block * KV_DIM, WINDOW), lambda i: (i, 0))
    return pl.pallas_call(
        functools.partial(_attn_sample_kernel, dec_seq=dec_seq),
        grid=(n_tok // tb,),
        in_specs=[pl.BlockSpec(memory_space=pltpu.SMEM),
                  row(D_MODEL), row(D_MODEL), row(KV_DIM), cache, row(KV_DIM), cache,
                  _resident((D_MODEL, D_MODEL), layer=0)],
        out_specs=[row(D_MODEL), cache, cache],
        out_shape=[jax.ShapeDtypeStruct(x.shape, _F32),
                   jax.ShapeDtypeStruct(cache_kt.shape, _F32),
                   jax.ShapeDtypeStruct(cache_vt.shape, _F32)],
        scratch_shapes=[
            pltpu.VMEM((N_KV_HEADS, GQA_GROUP * dec_seq, WINDOW), _F32),
            pltpu.VMEM((N_KV_HEADS, GQA_GROUP * dec_seq, dec_seq), _F32),
            pltpu.VMEM((tb, D_MODEL), _BF16),
        ],
        compiler_params=_params(1),
        name="attn_sample",
    )(sinks, x, q, k, cache_kt, v, cache_vt, w_o)


def _ffn_ple_kernel(h_ref, p_ref, gf_ref, w1_ref, w2_ref, gp_ref, wg_ref, wp_ref, out_ref):
    n_groups = h_ref.shape[0] // TOKEN_BLOCK
    groups = [slice(r * TOKEN_BLOCK, (r + 1) * TOKEN_BLOCK) for r in range(n_groups)]
    hs = [h_ref[rows, :] for rows in groups]
    ns = [_rms(h, gf_ref[...]).astype(_BF16) for h in hs]
    for c in range(D_FF // FF_CHUNK):
        cols = slice(c * FF_CHUNK, (c + 1) * FF_CHUNK)
        acts = [jnp.maximum(_dot(n, w1_ref[:, cols]), 0.0) for n in ns]
        hs = [h + _dot((a * a).astype(_BF16), w2_ref[cols, :]) for h, a in zip(hs, acts)]
    zs = [_dot(_rms(h, gp_ref[...]).astype(_BF16), wg_ref[...]) for h in hs]
    es = [_dot(p_ref[rows, :].astype(_BF16), wp_ref[...]) for rows in groups]
    for rows, h, z, e in zip(groups, hs, zs, es):
        gate = 1.0 / (1.0 + jnp.exp(-z))
        out_ref[rows, :] = h + gate * e


def _ffn_ple(h, p, g_ffn, w1, w2, g_ple, w_gate, w_proj, layer):
    n_tok = h.shape[0]
    tb = FFN_TOKEN_BLOCK
    row = lambda width: pl.BlockSpec((tb, width), lambda i: (i, 0))
    return pl.pallas_call(
        _ffn_ple_kernel,
        grid=(n_tok // tb,),
        in_specs=[row(D_MODEL), pl.BlockSpec((None, tb, PLE_DIM), lambda i: (layer, i, 0)),
                  _resident((1, D_MODEL), layer),
                  _resident((D_MODEL, D_FF), layer), _resident((D_FF, D_MODEL), layer),
                  _resident((1, D_MODEL), layer), _resident((D_MODEL, D_MODEL), layer),
                  _resident((PLE_DIM, D_MODEL), layer)],
        out_specs=row(D_MODEL),
        out_shape=jax.ShapeDtypeStruct(h.shape, _F32),
        compiler_params=_params(1),
        name="ffn_ple",
    )(h, p, g_ffn, w1, w2, g_ple, w_gate, w_proj)


def _gelu_tanh(x):
    c = math.sqrt(2.0 / math.pi)
    return x * (0.5 * (1.0 + jnp.tanh(c * (x + 0.044715 * (x * x * x)))))


def _gmlp_kernel(h_ref, g_ref, wuv_ref, vg_ref, ws_ref, bs_ref, wout_ref, out_ref, *rest,
                 emit_v):
    if emit_v:
        vout_ref, v_scr, vb_scr = rest
    else:
        v_scr, vb_scr = rest
    tb = h_ref.shape[0]
    group_rows = tb // GMLP_ROW_GROUPS
    row_groups = [slice(r * group_rows, (r + 1) * group_rows) for r in range(GMLP_ROW_GROUPS)]
    hs = [h_ref[rows, :] for rows in row_groups]
    ns = [_rms(h, g_ref[...]).astype(_BF16) for h in hs]

    ssq = [jnp.zeros((group_rows, 1), _F32) for _ in row_groups]
    for c in range(GMLP_HALF // FF_CHUNK):
        cols = slice(c * FF_CHUNK, (c + 1) * FF_CHUNK)
        w_cols = slice(GMLP_HALF + c * FF_CHUNK, GMLP_HALF + (c + 1) * FF_CHUNK)
        vcs = [_gelu_tanh(_dot(n, wuv_ref[:, w_cols])) for n in ns]
        for r, (rows, vc) in enumerate(zip(row_groups, vcs)):
            v_scr[rows, cols] = vc
            ssq[r] = ssq[r] + jnp.sum(vc * vc, axis=-1, keepdims=True)
    invs = [lax.rsqrt(s * (1.0 / GMLP_HALF) + EPS) for s in ssq]
    for c in range(GMLP_HALF // FF_CHUNK):
        cols = slice(c * FF_CHUNK, (c + 1) * FF_CHUNK)
        for rows, inv in zip(row_groups, invs):
            vn = (v_scr[rows, cols] * inv) * vg_ref[:, cols]
            if emit_v:
                vout_ref[rows, cols] = vn
            vb_scr[rows, cols] = vn.astype(_BF16)

    groups_per_step = U_CHUNK // GMLP_GROUP_DIM
    for c in range(GMLP_HALF // U_CHUNK):
        us = [_gelu_tanh(_dot(n, wuv_ref[:, c * U_CHUNK:(c + 1) * U_CHUNK])) for n in ns]
        gateds = []
        for r, u in enumerate(us):
            gated_rows = []
            for chunk in range(group_rows // GMLP_CHUNK):
                local = slice(chunk * GMLP_CHUNK, (chunk + 1) * GMLP_CHUNK)
                rows = slice(r * group_rows + chunk * GMLP_CHUNK,
                             r * group_rows + (chunk + 1) * GMLP_CHUNK)
                parts = []
                for gg in range(groups_per_step):
                    grp = c * groups_per_step + gg
                    cols = slice(grp * GMLP_GROUP_DIM, (grp + 1) * GMLP_GROUP_DIM)
                    s = _dot(ws_ref[grp], vb_scr[rows, cols]) + bs_ref[:, grp:grp + 1]
                    parts.append(u[local, gg * GMLP_GROUP_DIM:(gg + 1) * GMLP_GROUP_DIM] * s)
                gated_rows.append(jnp.concatenate(parts, axis=1))
            gateds.append(jnp.concatenate(gated_rows, axis=0).astype(_BF16))
        hs = [h + _dot(gated, wout_ref[c * U_CHUNK:(c + 1) * U_CHUNK, :])
              for h, gated in zip(hs, gateds)]
    for rows, h in zip(row_groups, hs):
        out_ref[rows, :] = h


def _gmlp(h, g, w_uv, v_gain, ws, bs_t, w_out, emit_v):
    n_tok = h.shape[0]
    tb = TOKEN_BLOCK
    row = lambda width: pl.BlockSpec((tb, width), lambda i: (i, 0))
    out_specs = [row(D_MODEL)]
    out_shape = [jax.ShapeDtypeStruct(h.shape, _F32)]
    if emit_v:
        out_specs.append(row(GMLP_HALF))
        out_shape.append(jax.ShapeDtypeStruct((n_tok, GMLP_HALF), _F32))
    return pl.pallas_call(
        functools.partial(_gmlp_kernel, emit_v=emit_v),
        grid=(n_tok // tb,),
        in_specs=[row(D_MODEL), _resident((1, D_MODEL), layer=1),
                  _resident((D_MODEL, 2 * GMLP_HALF), layer=0),
                  _resident((1, GMLP_HALF)),
                  _resident((GMLP_GROUPS, GMLP_CHUNK, GMLP_CHUNK)),
                  _resident((GMLP_CHUNK, GMLP_GROUPS)),
                  _resident((GMLP_HALF, D_MODEL), layer=0)],
        out_specs=out_specs,
        out_shape=out_shape,
        scratch_shapes=[pltpu.VMEM((tb, GMLP_HALF), _F32), pltpu.VMEM((tb, GMLP_HALF), _BF16)],
        compiler_params=_params(1),
        name="gmlp_v" if emit_v else "gmlp",
    )(h, g, w_uv, v_gain, ws, bs_t, w_out)


def _gmlp_spatial_weights(w_s, b_s, length):
    tril = jnp.tril(jnp.ones((length, length), w_s.dtype))
    ws = w_s[:, :length, :length] * tril
    reps = GMLP_CHUNK // length
    if reps > 1:
        eye = jnp.eye(reps, dtype=w_s.dtype)
        ws = jnp.einsum("ab,gij->gaibj", eye, ws).reshape(GMLP_GROUPS, GMLP_CHUNK, GMLP_CHUNK)
    bs_t = jnp.tile(b_s[:, :length].T, (reps, 1))
    return ws.astype(_BF16), bs_t


def _to_cache_t(cache):
    _, b, rows, kvh, hd = cache.shape
    return jnp.transpose(cache, (0, 1, 3, 4, 2)).reshape(b * kvh * hd, rows)


def _from_cache_t(cache_t, batch):
    t = cache_t.reshape(1, batch, N_KV_HEADS, HEAD_DIM, cache_t.shape[1])
    return jnp.transpose(t, (0, 1, 4, 2, 3))


def _trunk(x, p, cache, w, batch, seq):
    ffn_ple = lambda h, layer: _ffn_ple(h, p, w["g_ffn"], w["w1"], w["w2"], w["g_ple"],
                                        w["w_gate"], w["w_proj"], layer)
    q, k, v = _qkv_proj(x, w["g_mix"], w["w_qkv"], w["qk_gain"])
    if cache is None:
        h, nkt, nvt = _attn_prompt(w["sinks"], x, q, k, v, w["w_o"], batch, seq)
    else:
        h, nkt, nvt = _attn_sample(w["sinks"], x, q, k, v, cache[0], cache[1], w["w_o"], seq)
    h = ffn_ple(h, 0)
    length = min(seq, GMLP_CHUNK)
    ws, bs_t = _gmlp_spatial_weights(w["w_s"], w["b_s"], length)
    emit_v = cache is not None
    res = _gmlp(h, w["g_mix"], w["w_uv"], w["v_gain"], ws, bs_t, w["w_out"], emit_v)
    h = res[0]
    v_rows = res[1] if emit_v else None
    h = ffn_ple(h, 1)
    return h, nkt, nvt, v_rows


def kernel(x_prompt, x_sample, p_prompt, p_sample, cache_k, cache_v, g_mix, g_ffn, g_ple,
           attn_w_qkv, attn_q_norm, attn_k_norm, attn_sinks, attn_w_o, gmlp_w_uv, gmlp_v_norm,
           gmlp_w_s, gmlp_b_s, gmlp_w_out, ffn_w1, ffn_w2, ple_w_proj, ple_w_gate):
    batch, seq, _ = x_prompt.shape
    dec_batch, dec_seq, _ = x_sample.shape
    depth = g_mix.shape[0]
    cache_rows = cache_k.shape[2]
    assert depth == 2 and attn_w_qkv.shape[0] == 1 and gmlp_w_uv.shape[0] == 1
    assert seq % FFN_TOKEN_BLOCK == 0 and (dec_batch * dec_seq) % FFN_TOKEN_BLOCK == 0
    assert TOKEN_BLOCK % (2 * dec_seq) == 0 and GMLP_CHUNK % dec_seq == 0 and dec_seq % 16 == 0
    assert cache_rows == WINDOW and dec_seq <= WINDOW

    scale = HEAD_DIM ** -0.5
    w = {
        "g_mix": g_mix.reshape(depth, 1, D_MODEL),
        "g_ffn": g_ffn.reshape(depth, 1, D_MODEL),
        "g_ple": g_ple.reshape(depth, 1, D_MODEL),
        "w_qkv": attn_w_qkv.astype(_BF16),
        "qk_gain": jnp.concatenate([jnp.tile(attn_q_norm[0] * scale, N_HEADS),
                                    jnp.tile(attn_k_norm[0], N_KV_HEADS)]).reshape(1, QK_DIM),
        "sinks": attn_sinks[0].reshape(1, N_HEADS),
        "w_o": attn_w_o.astype(_BF16),
        "w_uv": gmlp_w_uv.astype(_BF16),
        "v_gain": gmlp_v_norm[0].reshape(1, GMLP_HALF),
        "w_s": gmlp_w_s[0],
        "b_s": gmlp_b_s[0],
        "w_out": gmlp_w_out.astype(_BF16),
        "w1": ffn_w1.astype(_BF16),
        "w2": ffn_w2.astype(_BF16),
        "w_gate": ple_w_gate.astype(_BF16),
        "w_proj": ple_w_proj.astype(_BF16),
    }

    n_prompt = batch * seq
    n_sample = dec_batch * dec_seq
    y_p, nkt_p, nvt_p, _ = _trunk(x_prompt.reshape(n_prompt, D_MODEL),
                                  p_prompt.reshape(depth, n_prompt, PLE_DIM), None, w, batch, seq)
    cache = (_to_cache_t(cache_k), _to_cache_t(cache_v))
    y_s, nkt_s, nvt_s, vrows = _trunk(x_sample.reshape(n_sample, D_MODEL),
                                      p_sample.reshape(depth, n_sample, PLE_DIM), cache, w,
                                      dec_batch, dec_seq)
    return (y_p.reshape(batch, seq, D_MODEL), y_s.reshape(dec_batch, dec_seq, D_MODEL),
            _from_cache_t(nkt_p, batch), _from_cache_t(nvt_p, batch),
            _from_cache_t(nkt_s, dec_batch), _from_cache_t(nvt_s, dec_batch),
            vrows.reshape(1, dec_batch, dec_seq, GMLP_HALF))
```

```python
import functools
import math

import jax
import jax.numpy as jnp
from jax import lax
from jax.experimental import pallas as pl
from jax.experimental.pallas import tpu as pltpu

D_MODEL = 1024
HEAD_DIM = 64
N_HEADS = 16
N_KV_HEADS = 4
GQA_GROUP = N_HEADS // N_KV_HEADS
KV_DIM = N_KV_HEADS * HEAD_DIM
QK_DIM = D_MODEL + KV_DIM
QKV_DIM = QK_DIM + KV_DIM
CHUNK = 64
WINDOW = 128
GMLP_CHUNK = 128
GMLP_HALF = 3 * D_MODEL
GMLP_GROUPS = 8
GMLP_GROUP_DIM = GMLP_HALF // GMLP_GROUPS
D_FF = 4 * D_MODEL
PLE_DIM = 256
EPS = 1e-6
NEG_INF = -1e30

LANES = 128
MXU_DIM_V7X = 256
BF16_SUBLANES = 16
VMEM_LIMIT_BYTES_V7X = 56 * 1024 * 1024

TOKEN_BLOCK = 512
FFN_TOKEN_BLOCK = 2 * TOKEN_BLOCK
GMLP_ROW_GROUPS = 1
ATTN_UNIT = 2 * CHUNK
ATTN_KEYS = 2 * WINDOW
PV_ROWS = HEAD_DIM + BF16_SUBLANES
FF_CHUNK = 1024
U_CHUNK = 2 * GMLP_GROUP_DIM

_BF16 = jnp.bfloat16
_F32 = jnp.float32
_CONTRACT_LAST = (((1,), (1,)), ((), ()))


def _resident(shape, layer=None):
    if layer is None:
        return pl.BlockSpec(shape, lambda *_: (0,) * len(shape), pipeline_mode=pl.Buffered(1))
    return pl.BlockSpec((None,) + tuple(shape), lambda *_: (layer,) + (0,) * len(shape),
                        pipeline_mode=pl.Buffered(1))


def _params(n_axes):
    return pltpu.CompilerParams(
        dimension_semantics=("arbitrary",) * n_axes,
        vmem_limit_bytes=VMEM_LIMIT_BYTES_V7X,
    )


def _rms(x, g):
    ms = jnp.mean(x * x, axis=-1, keepdims=True)
    return (x * lax.rsqrt(ms + EPS)) * g


def _dot(a, b):
    return jnp.dot(a, b, preferred_element_type=_F32)


def _qkv_kernel(x_ref, g_ref, w_ref, qkg_ref, q_ref, k_ref, v_ref):
    n = _rms(x_ref[...], g_ref[...]).astype(_BF16)
    qkv = _dot(n, w_ref[...])
    r = lax.broadcasted_iota(jnp.int32, (MXU_DIM_V7X, MXU_DIM_V7X), 0) // HEAD_DIM
    c = lax.broadcasted_iota(jnp.int32, (MXU_DIM_V7X, MXU_DIM_V7X), 1) // HEAD_DIM
    seg = jnp.where(r == c, 1.0, 0.0).astype(_BF16)
    normed = []
    for j in range(QK_DIM // MXU_DIM_V7X):
        cols = slice(j * MXU_DIM_V7X, (j + 1) * MXU_DIM_V7X)
        t = qkv[:, cols]
        ss = _dot((t * t).astype(_BF16), seg)
        inv = lax.rsqrt(ss * (1.0 / HEAD_DIM) + EPS)
        normed.append((t * inv) * qkg_ref[:, cols])
    q_ref[...] = jnp.concatenate(normed[:-1], axis=1).astype(_BF16)
    k_ref[...] = normed[-1]
    v_ref[...] = qkv[:, QK_DIM:]


def _qkv_proj(x, g, w_qkv, qk_gain):
    n_tok = x.shape[0]
    tb = TOKEN_BLOCK
    row = lambda width: pl.BlockSpec((tb, width), lambda i: (i, 0))
    return pl.pallas_call(
        _qkv_kernel,
        grid=(n_tok // tb,),
        in_specs=[row(D_MODEL), _resident((1, D_MODEL), layer=0),
                  _resident((D_MODEL, QKV_DIM), layer=0), _resident((1, QK_DIM))],
        out_specs=[row(D_MODEL), row(KV_DIM), row(KV_DIM)],
        out_shape=[jax.ShapeDtypeStruct((n_tok, D_MODEL), _BF16),
                   jax.ShapeDtypeStruct((n_tok, KV_DIM), _F32),
                   jax.ShapeDtypeStruct((n_tok, KV_DIM), _F32)],
        compiler_params=_params(1),
        name="qkv_proj",
    )(x, g, w_qkv, qk_gain)


def _alibi_slope(head):
    return 2.0 ** (-8.0 * (head + 1) / N_HEADS)


def _fill_bias_t(bias_ref, variant, valid_fn):
    s = lax.broadcasted_iota(jnp.int32, (ATTN_KEYS, ATTN_UNIT), 0)
    t = lax.broadcasted_iota(jnp.int32, (ATTN_KEYS, ATTN_UNIT), 1)
    dist = jnp.abs(t + WINDOW - s).astype(_F32)
    valid = valid_fn(t, s)
    for kvh in range(N_KV_HEADS):
        for p in range(2):
            for g in range(2):
                slope = _alibi_slope(kvh * GQA_GROUP + 2 * g + p)
                bias_ref[variant, kvh, p * ATTN_KEYS:(p + 1) * ATTN_KEYS,
                         g * ATTN_UNIT:(g + 1) * ATTN_UNIT] = jnp.where(valid, -slope * dist, NEG_INF)


def _ffn_ple_steps(h, p_bf16, gf_ref, w1_ref, w2_ref, gp_ref, wg_ref, wp_ref):
    n = _rms(h, gf_ref[...]).astype(_BF16)
    for c in range(D_FF // FF_CHUNK):
        cols = slice(c * FF_CHUNK, (c + 1) * FF_CHUNK)
        a = jnp.maximum(_dot(n, w1_ref[:, cols]), 0.0)
        yield
        h = h + _dot((a * a).astype(_BF16), w2_ref[cols, :])
        yield
    z = _dot(_rms(h, gp_ref[...]).astype(_BF16), wg_ref[...])
    yield
    e = _dot(p_bf16, wp_ref[...])
    yield
    gate = 1.0 / (1.0 + jnp.exp(-z))
    return h + gate * e


def _finish(steps):
    while True:
        try:
            next(steps)
        except StopIteration as stop:
            return stop.value


def _attn_ffn_prompt_kernel(sinks_ref, x_ref, q_ref, k_ref, kh_ref, v_ref, vh_ref, wo_ref, p_ref,
                            gf_ref, w1_ref, w2_ref, gp_ref, wg_ref, wp_ref,
                            out_ref, nkt_ref, nvt_ref,
                            bias_ref, klo_scr, khi_scr, vt_scr, ot_scr, h_scr, *, blocks_per_seq):
    step = pl.program_id(0)
    tb = TOKEN_BLOCK
    block = jnp.minimum(step, pl.num_programs(0) - 2)

    @pl.when(step == 0)
    def _():
        in_window = lambda t, s: ((t < CHUNK) & (s < WINDOW + CHUNK)) | ((t >= CHUNK) & (s >= CHUNK))
        _fill_bias_t(bias_ref, 0, in_window)
        _fill_bias_t(bias_ref, 1, lambda t, s: in_window(t, s) & (s >= WINDOW))
        vt_scr[:, HEAD_DIM:, :] = jnp.ones((N_KV_HEADS, PV_ROWS - HEAD_DIM, WINDOW + tb), _BF16)
        h_scr[...] = jnp.zeros_like(h_scr)

    mlp = _ffn_ple_steps(h_scr[...], p_ref[...].astype(_BF16), gf_ref, w1_ref, w2_ref, gp_ref,
                         wg_ref, wp_ref)
    next(mlp)

    kd = jnp.concatenate([kh_ref[...], k_ref[...]], axis=0)
    low_half = lax.broadcasted_iota(jnp.int32, (WINDOW + tb, LANES), 1) < HEAD_DIM
    for pair in range(N_KV_HEADS // 2):
        kk = kd[:, pair * LANES:(pair + 1) * LANES]
        swapped = pltpu.roll(kk, HEAD_DIM, axis=1)
        klo_scr[2 * pair] = jnp.where(low_half, kk, 0.0).astype(_BF16)
        khi_scr[2 * pair] = jnp.where(low_half, 0.0, swapped).astype(_BF16)
        klo_scr[2 * pair + 1] = jnp.where(low_half, swapped, 0.0).astype(_BF16)
        khi_scr[2 * pair + 1] = jnp.where(low_half, 0.0, kk).astype(_BF16)
    vt = jnp.concatenate([vh_ref[...], v_ref[...]], axis=0).T
    for kvh in range(N_KV_HEADS):
        vt_scr[kvh, :HEAD_DIM, :] = vt[kvh * HEAD_DIM:(kvh + 1) * HEAD_DIM, :].astype(_BF16)

    nkt_ref[...] = k_ref[tb - WINDOW:, :].T
    nvt_ref[...] = vt[:, tb:]

    first_in_seq = jnp.where(block % blocks_per_seq == 0, 1, 0)
    lane2 = lax.broadcasted_iota(jnp.int32, (1, 2 * ATTN_UNIT), 1)
    n_units = tb // ATTN_UNIT

    def scores(u):
        rows = slice(u * ATTN_UNIT, (u + 1) * ATTN_UNIT)
        keys = slice(u * ATTN_UNIT, u * ATTN_UNIT + ATTN_KEYS)
        variant = first_in_seq if u == 0 else 0
        out = []
        for kvh in range(N_KV_HEADS):
            k2 = jnp.concatenate([klo_scr[kvh, keys, :], khi_scr[kvh, keys, :]], axis=0)
            q2 = jnp.concatenate(
                [q_ref[rows, (2 * kvh + g) * LANES:(2 * kvh + g + 1) * LANES] for g in range(2)],
                axis=0)
            s_t = lax.dot_general(k2, q2, _CONTRACT_LAST, preferred_element_type=_F32)
            out.append(s_t + bias_ref[variant, kvh])
        return out

    s_next = scores(0)
    for u in range(n_units):
        rows = slice(u * ATTN_UNIT, (u + 1) * ATTN_UNIT)
        keys = slice(u * ATTN_UNIT, u * ATTN_UNIT + ATTN_KEYS)
        s_cur = s_next
        if u + 1 < n_units:
            s_next = scores(u + 1)
        next(mlp)
        chains = []
        for kvh in range(N_KV_HEADS):
            for p in range(2):
                head0 = kvh * GQA_GROUP + p
                sink = jnp.where(lane2 < ATTN_UNIT, sinks_ref[0, head0], sinks_ref[0, head0 + 2])
                sp = s_cur[kvh][p * ATTN_KEYS:(p + 1) * ATTN_KEYS, :]
                m = jnp.maximum(jnp.max(sp, axis=0, keepdims=True), sink)
                chains.append((head0, kvh, jnp.exp(sp - m).astype(_BF16), jnp.exp(sink - m)))
        results = [(head0, _dot(vt_scr[kvh, :, keys], e), sink_e)
                   for head0, kvh, e, sink_e in chains]
        next(mlp)
        for head0, r, sink_e in results:
            den = r[HEAD_DIM:HEAD_DIM + 1, :] + sink_e
            o = r[:HEAD_DIM, :] * (1.0 / den)
            for g in range(2):
                head = head0 + 2 * g
                ot_scr[head * HEAD_DIM:(head + 1) * HEAD_DIM, rows] = (
                    o[:, g * ATTN_UNIT:(g + 1) * ATTN_UNIT].astype(_BF16))

    out_ref[...] = _finish(mlp)
    h_scr[...] = x_ref[...] + _dot(ot_scr[...].T, wo_ref[...])


def _attn_ffn_prompt(sinks, x, q, k, v, w_o, p, g_ffn, w1, w2, g_ple, w_gate, w_proj, seq):
    tb = TOKEN_BLOCK
    layer = 0
    n_blocks = x.shape[0] // tb
    nt = seq // tb
    halo_per_block = tb // WINDOW
    attn_block = lambda s: jnp.minimum(s, n_blocks - 1)
    mlp_block = lambda s: jnp.maximum(s - 1, 0)
    row = lambda width: pl.BlockSpec((tb, width), lambda s: (attn_block(s), 0))
    halo = pl.BlockSpec(
        (WINDOW, KV_DIM), lambda s: (jnp.maximum(attn_block(s) * halo_per_block - 1, 0), 0))
    per_seq = pl.BlockSpec((KV_DIM, WINDOW), lambda s: (attn_block(s) // nt, 0))
    return pl.pallas_call(
        functools.partial(_attn_ffn_prompt_kernel, blocks_per_seq=nt),
        grid=(n_blocks + 1,),
        in_specs=[pl.BlockSpec(memory_space=pltpu.SMEM),
                  row(D_MODEL), row(D_MODEL), row(KV_DIM), halo, row(KV_DIM), halo,
                  _resident((D_MODEL, D_MODEL), layer),
                  pl.BlockSpec((None, tb, PLE_DIM), lambda s: (layer, mlp_block(s), 0)),
                  _resident((1, D_MODEL), layer),
                  _resident((D_MODEL, D_FF), layer), _resident((D_FF, D_MODEL), layer),
                  _resident((1, D_MODEL), layer), _resident((D_MODEL, D_MODEL), layer),
                  _resident((PLE_DIM, D_MODEL), layer)],
        out_specs=[pl.BlockSpec((tb, D_MODEL), lambda s: (mlp_block(s), 0)), per_seq, per_seq],
        out_shape=[jax.ShapeDtypeStruct(x.shape, _F32),
                   jax.ShapeDtypeStruct((x.shape[0] // seq * KV_DIM, WINDOW), _F32),
                   jax.ShapeDtypeStruct((x.shape[0] // seq * KV_DIM, WINDOW), _F32)],
        scratch_shapes=[
            pltpu.VMEM((2, N_KV_HEADS, 2 * ATTN_KEYS, 2 * ATTN_UNIT), _F32),
            pltpu.VMEM((N_KV_HEADS, WINDOW + tb, LANES), _BF16),
            pltpu.VMEM((N_KV_HEADS, WINDOW + tb, LANES), _BF16),
            pltpu.VMEM((N_KV_HEADS, PV_ROWS, WINDOW + tb), _BF16),
            pltpu.VMEM((D_MODEL, tb), _BF16),
            pltpu.VMEM((tb, D_MODEL), _F32),
        ],
        compiler_params=_params(1),
        name="attn_ffn_prompt",
    )(sinks, x, q, k, k, v, v, w_o, p, g_ffn, w1, w2, g_ple, w_gate, w_proj)


def _fill_bias(bias_c_ref, bias_n_ref, q_rows):
    t_c = lax.broadcasted_iota(jnp.int32, (q_rows, WINDOW), 0)
    s_c = lax.broadcasted_iota(jnp.int32, (q_rows, WINDOW), 1)
    dist_c = jnp.abs(t_c + WINDOW - s_c).astype(_F32)
    t_n = lax.broadcasted_iota(jnp.int32, (q_rows, q_rows), 0)
    s_n = lax.broadcasted_iota(jnp.int32, (q_rows, q_rows), 1)
    dist_n = jnp.abs(t_n - s_n).astype(_F32)
    for kvh in range(N_KV_HEADS):
        for j in range(GQA_GROUP):
            slope = _alibi_slope(kvh * GQA_GROUP + j)
            bias_c_ref[kvh, j * q_rows:(j + 1) * q_rows, :] = -slope * dist_c
            bias_n_ref[kvh, j * q_rows:(j + 1) * q_rows, :] = -slope * dist_n


def _attn_sample_kernel(sinks_ref, x_ref, q_ref, k_ref, ckt_ref, v_ref, cvt_ref, wo_ref,
                        out_ref, nkt_ref, nvt_ref, bias_c_ref, bias_n_ref, o_scr, *, dec_seq):
    kept = WINDOW - dec_seq

    @pl.when(pl.program_id(0) == 0)
    def _():
        _fill_bias(bias_c_ref, bias_n_ref, dec_seq)

    keep_lane = lax.broadcasted_iota(jnp.int32, (HEAD_DIM, WINDOW), 1) < kept
    pad = jnp.zeros((kept, LANES), _F32)

    def new_cols_t(rows_f32):
        return jnp.concatenate([pad, rows_f32], axis=0).T

    def one_batch(bb, carry):
        rows = pl.ds(pl.multiple_of(bb * dec_seq, dec_seq), dec_seq)
        q_u = q_ref[rows, :]
        k_new = k_ref[rows, :]
        v_new = v_ref[rows, :]
        k_new_b = k_new.astype(_BF16)
        v_new_b = v_new.astype(_BF16)
        kt_new = [new_cols_t(k_new[:, g * LANES:(g + 1) * LANES]) for g in range(KV_DIM // LANES)]
        vt_new = [new_cols_t(v_new[:, g * LANES:(g + 1) * LANES]) for g in range(KV_DIM // LANES)]

        staged = []
        for kvh in range(N_KV_HEADS):
            cache = pl.ds(pl.multiple_of(bb * KV_DIM + kvh * HEAD_DIM, HEAD_DIM), HEAD_DIM)
            head_cols = slice(kvh * HEAD_DIM, (kvh + 1) * HEAD_DIM)
            kt_c = ckt_ref[cache, :]
            vt_c = cvt_ref[cache, :]
            half = slice((kvh % 2) * HEAD_DIM, (kvh % 2 + 1) * HEAD_DIM)
            nkt_ref[cache, :] = jnp.where(keep_lane, pltpu.roll(kt_c, kept, axis=1),
                                          kt_new[kvh // 2][half, :])
            nvt_ref[cache, :] = jnp.where(keep_lane, pltpu.roll(vt_c, kept, axis=1),
                                          vt_new[kvh // 2][half, :])
            q4 = jnp.concatenate(
                [q_u[:, (kvh * GQA_GROUP + j) * HEAD_DIM:(kvh * GQA_GROUP + j + 1) * HEAD_DIM]
                 for j in range(GQA_GROUP)], axis=0)
            s_c = _dot(q4, kt_c.astype(_BF16)) + bias_c_ref[kvh]
            s_n = lax.dot_general(q4, k_new_b[:, head_cols], _CONTRACT_LAST,
                                  preferred_element_type=_F32) + bias_n_ref[kvh]
            staged.append((s_c, s_n, vt_c.astype(_BF16), v_new_b[:, head_cols]))

        soft = []
        for kvh, (s_c, s_n, vt_c, v_n) in enumerate(staged):
            sink = jnp.concatenate(
                [jnp.full((dec_seq, 1), sinks_ref[0, kvh * GQA_GROUP + j], _F32)
                 for j in range(GQA_GROUP)], axis=0)
            m = jnp.maximum(jnp.maximum(jnp.max(s_c, axis=-1, keepdims=True),
                                        jnp.max(s_n, axis=-1, keepdims=True)), sink)
            e_c = jnp.exp(s_c - m)
            e_n = jnp.exp(s_n - m)
            den = (jnp.sum(e_c, axis=-1, keepdims=True) + jnp.sum(e_n, axis=-1, keepdims=True)
                   + jnp.exp(sink - m))
            soft.append((e_c.astype(_BF16), e_n.astype(_BF16), den, vt_c, v_n))

        outs = [(lax.dot_general(e_c, vt_c, _CONTRACT_LAST, preferred_element_type=_F32)
                 + _dot(e_n, v_n), den) for e_c, e_n, den, vt_c, v_n in soft]
        for kvh, (o, den) in enumerate(outs):
            o = o / den
            for j in range(GQA_GROUP):
                head = kvh * GQA_GROUP + j
                o_scr[rows, head * HEAD_DIM:(head + 1) * HEAD_DIM] = (
                    o[j * dec_seq:(j + 1) * dec_seq, :].astype(_BF16))
        return carry

    lax.fori_loop(0, TOKEN_BLOCK // dec_seq, one_batch, 0, unroll=2)
    out_ref[...] = x_ref[...] + _dot(o_scr[...], wo_ref[...])


def _attn_sample(sinks, x, q, k, v, cache_kt, cache_vt, w_o, dec_seq):
    n_tok = x.shape[0]
    tb = TOKEN_BLOCK
    batches_per_block = tb // dec_seq
    row = lambda width: pl.BlockSpec((tb, width), lambda i: (i, 0))
    cache = pl.BlockSpec((batches_per_block * KV_DIM, WINDOW), lambda i: (i, 0))
    return pl.pallas_call(
        functools.partial(_attn_sample_kernel, dec_seq=dec_seq),
        grid=(n_tok // tb,),
        in_specs=[pl.BlockSpec(memory_space=pltpu.SMEM),
                  row(D_MODEL), row(D_MODEL), row(KV_DIM), cache, row(KV_DIM), cache,
                  _resident((D_MODEL, D_MODEL), layer=0)],
        out_specs=[row(D_MODEL), cache, cache],
        out_shape=[jax.ShapeDtypeStruct(x.shape, _F32),
                   jax.ShapeDtypeStruct(cache_kt.shape, _F32),
                   jax.ShapeDtypeStruct(cache_vt.shape, _F32)],
        scratch_shapes=[
            pltpu.VMEM((N_KV_HEADS, GQA_GROUP * dec_seq, WINDOW), _F32),
            pltpu.VMEM((N_KV_HEADS, GQA_GROUP * dec_seq, dec_seq), _F32),
            pltpu.VMEM((tb, D_MODEL), _BF16),
        ],
        compiler_params=_params(1),
        name="attn_sample",
    )(sinks, x, q, k, cache_kt, v, cache_vt, w_o)


def _ffn_ple_kernel(h_ref, p_ref, gf_ref, w1_ref, w2_ref, gp_ref, wg_ref, wp_ref, out_ref):
    n_groups = h_ref.shape[0] // TOKEN_BLOCK
    groups = [slice(r * TOKEN_BLOCK, (r + 1) * TOKEN_BLOCK) for r in range(n_groups)]
    runs = [_ffn_ple_steps(h_ref[rows, :], p_ref[rows, :].astype(_BF16), gf_ref, w1_ref, w2_ref,
                           gp_ref, wg_ref, wp_ref) for rows in groups]
    results = {}
    while len(results) < n_groups:
        for rows, run in zip(groups, runs):
            try:
                next(run)
            except StopIteration as stop:
                results[rows.start] = stop.value
    for rows in groups:
        out_ref[rows, :] = results[rows.start]


def _ffn_ple(h, p, g_ffn, w1, w2, g_ple, w_gate, w_proj, layer):
    n_tok = h.shape[0]
    tb = FFN_TOKEN_BLOCK
    row = lambda width: pl.BlockSpec((tb, width), lambda i: (i, 0))
    return pl.pallas_call(
        _ffn_ple_kernel,
        grid=(n_tok // tb,),
        in_specs=[row(D_MODEL), pl.BlockSpec((None, tb, PLE_DIM), lambda i: (layer, i, 0)),
                  _resident((1, D_MODEL), layer),
                  _resident((D_MODEL, D_FF), layer), _resident((D_FF, D_MODEL), layer),
                  _resident((1, D_MODEL), layer), _resident((D_MODEL, D_MODEL), layer),
                  _resident((PLE_DIM, D_MODEL), layer)],
        out_specs=row(D_MODEL),
        out_shape=jax.ShapeDtypeStruct(h.shape, _F32),
        compiler_params=_params(1),
        name="ffn_ple",
    )(h, p, g_ffn, w1, w2, g_ple, w_gate, w_proj)


def _gelu_tanh(x):
    c = math.sqrt(2.0 / math.pi)
    return x * (0.5 * (1.0 + jnp.tanh(c * (x + 0.044715 * (x * x * x)))))


def _gmlp_kernel(h_ref, g_ref, wuv_ref, vg_ref, ws_ref, bs_ref, wout_ref, out_ref, *rest,
                 emit_v):
    if emit_v:
        vout_ref, v_scr, vb_scr = rest
    else:
        v_scr, vb_scr = rest
    tb = h_ref.shape[0]
    group_rows = tb // GMLP_ROW_GROUPS
    row_groups = [slice(r * group_rows, (r + 1) * group_rows) for r in range(GMLP_ROW_GROUPS)]
    hs = [h_ref[rows, :] for rows in row_groups]
    ns = [_rms(h, g_ref[...]).astype(_BF16) for h in hs]

    ssq = [jnp.zeros((group_rows, 1), _F32) for _ in row_groups]
    for c in range(GMLP_HALF // FF_CHUNK):
        cols = slice(c * FF_CHUNK, (c + 1) * FF_CHUNK)
        w_cols = slice(GMLP_HALF + c * FF_CHUNK, GMLP_HALF + (c + 1) * FF_CHUNK)
        vcs = [_gelu_tanh(_dot(n, wuv_ref[:, w_cols])) for n in ns]
        for r, (rows, vc) in enumerate(zip(row_groups, vcs)):
            v_scr[rows, cols] = vc
            ssq[r] = ssq[r] + jnp.sum(vc * vc, axis=-1, keepdims=True)
    invs = [lax.rsqrt(s * (1.0 / GMLP_HALF) + EPS) for s in ssq]
    for c in range(GMLP_HALF // FF_CHUNK):
        cols = slice(c * FF_CHUNK, (c + 1) * FF_CHUNK)
        for rows, inv in zip(row_groups, invs):
            vn = (v_scr[rows, cols] * inv) * vg_ref[:, cols]
            if emit_v:
                vout_ref[rows, cols] = vn
            vb_scr[rows, cols] = vn.astype(_BF16)

    groups_per_step = U_CHUNK // GMLP_GROUP_DIM
    for c in range(GMLP_HALF // U_CHUNK):
        us = [_gelu_tanh(_dot(n, wuv_ref[:, c * U_CHUNK:(c + 1) * U_CHUNK])) for n in ns]
        gateds = []
        for r, u in enumerate(us):
            gated_rows = []
            for chunk in range(group_rows // GMLP_CHUNK):
                local = slice(chunk * GMLP_CHUNK, (chunk + 1) * GMLP_CHUNK)
                rows = slice(r * group_rows + chunk * GMLP_CHUNK,
                             r * group_rows + (chunk + 1) * GMLP_CHUNK)
                parts = []
                for gg in range(groups_per_step):
                    grp = c * groups_per_step + gg
                    cols = slice(grp * GMLP_GROUP_DIM, (grp + 1) * GMLP_GROUP_DIM)
                    s = _dot(ws_ref[grp], vb_scr[rows, cols]) + bs_ref[:, grp:grp + 1]
                    parts.append(u[local, gg * GMLP_GROUP_DIM:(gg + 1) * GMLP_GROUP_DIM] * s)
                gated_rows.append(jnp.concatenate(parts, axis=1))
            gateds.append(jnp.concatenate(gated_rows, axis=0).astype(_BF16))
        hs = [h + _dot(gated, wout_ref[c * U_CHUNK:(c + 1) * U_CHUNK, :])
              for h, gated in zip(hs, gateds)]
    for rows, h in zip(row_groups, hs):
        out_ref[rows, :] = h


def _gmlp(h, g, w_uv, v_gain, ws, bs_t, w_out, emit_v):
    n_tok = h.shape[0]
    tb = TOKEN_BLOCK
    row = lambda width: pl.BlockSpec((tb, width), lambda i: (i, 0))
    out_specs = [row(D_MODEL)]
    out_shape = [jax.ShapeDtypeStruct(h.shape, _F32)]
    if emit_v:
        out_specs.append(row(GMLP_HALF))
        out_shape.append(jax.ShapeDtypeStruct((n_tok, GMLP_HALF), _F32))
    return pl.pallas_call(
        functools.partial(_gmlp_kernel, emit_v=emit_v),
        grid=(n_tok // tb,),
        in_specs=[row(D_MODEL), _resident((1, D_MODEL), layer=1),
                  _resident((D_MODEL, 2 * GMLP_HALF), layer=0),
                  _resident((1, GMLP_HALF)),
                  _resident((GMLP_GROUPS, GMLP_CHUNK, GMLP_CHUNK)),
                  _resident((GMLP_CHUNK, GMLP_GROUPS)),
                  _resident((GMLP_HALF, D_MODEL), layer=0)],
        out_specs=out_specs,
        out_shape=out_shape,
        scratch_shapes=[pltpu.VMEM((tb, GMLP_HALF), _F32), pltpu.VMEM((tb, GMLP_HALF), _BF16)],
        compiler_params=_params(1),
        name="gmlp_v" if emit_v else "gmlp",
    )(h, g, w_uv, v_gain, ws, bs_t, w_out)


def _gmlp_spatial_weights(w_s, b_s, length):
    tril = jnp.tril(jnp.ones((length, length), w_s.dtype))
    ws = w_s[:, :length, :length] * tril
    reps = GMLP_CHUNK // length
    if reps > 1:
        eye = jnp.eye(reps, dtype=w_s.dtype)
        ws = jnp.einsum("ab,gij->gaibj", eye, ws).reshape(GMLP_GROUPS, GMLP_CHUNK, GMLP_CHUNK)
    bs_t = jnp.tile(b_s[:, :length].T, (reps, 1))
    return ws.astype(_BF16), bs_t


def _to_cache_t(cache):
    _, b, rows, kvh, hd = cache.shape
    return jnp.transpose(cache, (0, 1, 3, 4, 2)).reshape(b * kvh * hd, rows)


def _from_cache_t(cache_t, batch):
    t = cache_t.reshape(1, batch, N_KV_HEADS, HEAD_DIM, cache_t.shape[1])
    return jnp.transpose(t, (0, 1, 4, 2, 3))


def _trunk(x, p, cache, w, batch, seq):
    ffn_ple = lambda h, layer: _ffn_ple(h, p, w["g_ffn"], w["w1"], w["w2"], w["g_ple"],
                                        w["w_gate"], w["w_proj"], layer)
    q, k, v = _qkv_proj(x, w["g_mix"], w["w_qkv"], w["qk_gain"])
    if cache is None:
        h, nkt, nvt = _attn_ffn_prompt(w["sinks"], x, q, k, v, w["w_o"], p, w["g_ffn"], w["w1"],
                                       w["w2"], w["g_ple"], w["w_gate"], w["w_proj"], seq)
    else:
        h, nkt, nvt = _attn_sample(w["sinks"], x, q, k, v, cache[0], cache[1], w["w_o"], seq)
        h = ffn_ple(h, 0)
    length = min(seq, GMLP_CHUNK)
    ws, bs_t = _gmlp_spatial_weights(w["w_s"], w["b_s"], length)
    emit_v = cache is not None
    res = _gmlp(h, w["g_mix"], w["w_uv"], w["v_gain"], ws, bs_t, w["w_out"], emit_v)
    h = res[0]
    v_rows = res[1] if emit_v else None
    h = ffn_ple(h, 1)
    return h, nkt, nvt, v_rows


def kernel(x_prompt, x_sample, p_prompt, p_sample, cache_k, cache_v, g_mix, g_ffn, g_ple,
           attn_w_qkv, attn_q_norm, attn_k_norm, attn_sinks, attn_w_o, gmlp_w_uv, gmlp_v_norm,
           gmlp_w_s, gmlp_b_s, gmlp_w_out, ffn_w1, ffn_w2, ple_w_proj, ple_w_gate):
    batch, seq, _ = x_prompt.shape
    dec_batch, dec_seq, _ = x_sample.shape
    depth = g_mix.shape[0]
    cache_rows = cache_k.shape[2]
    assert depth == 2 and attn_w_qkv.shape[0] == 1 and gmlp_w_uv.shape[0] == 1
    assert seq % FFN_TOKEN_BLOCK == 0 and (dec_batch * dec_seq) % FFN_TOKEN_BLOCK == 0
    assert TOKEN_BLOCK % (2 * dec_seq) == 0 and GMLP_CHUNK % dec_seq == 0 and dec_seq % 16 == 0
    assert cache_rows == WINDOW and dec_seq <= WINDOW

    scale = HEAD_DIM ** -0.5
    w = {
        "g_mix": g_mix.reshape(depth, 1, D_MODEL),
        "g_ffn": g_ffn.reshape(depth, 1, D_MODEL),
        "g_ple": g_ple.reshape(depth, 1, D_MODEL),
        "w_qkv": attn_w_qkv.astype(_BF16),
        "qk_gain": jnp.concatenate([jnp.tile(attn_q_norm[0] * scale, N_HEADS),
                                    jnp.tile(attn_k_norm[0], N_KV_HEADS)]).reshape(1, QK_DIM),
        "sinks": attn_sinks[0].reshape(1, N_HEADS),
        "w_o": attn_w_o.astype(_BF16),
        "w_uv": gmlp_w_uv.astype(_BF16),
        "v_gain": gmlp_v_norm[0].reshape(1, GMLP_HALF),
        "w_s": gmlp_w_s[0],
        "b_s": gmlp_b_s[0],
        "w_out": gmlp_w_out.astype(_BF16),
        "w1": ffn_w1.astype(_BF16),
        "w2": ffn_w2.astype(_BF16),
        "w_gate": ple_w_gate.astype(_BF16),
        "w_proj": ple_w_proj.astype(_BF16),
    }

    n_prompt = batch * seq
    n_sample = dec_batch * dec_seq
    y_p, nkt_p, nvt_p, _ = _trunk(x_prompt.reshape(n_prompt, D_MODEL),
                                  p_prompt.reshape(depth, n_prompt, PLE_DIM), None, w, batch, seq)
    cache = (_to_cache_t(cache_k), _to_cache_t(cache_v))
    y_s, nkt_s, nvt_s, vrows = _trunk(x_sample.reshape(n_sample, D_MODEL),
                                      p_sample.reshape(depth, n_sample, PLE_DIM), cache, w,
                                      dec_batch, dec_seq)
    return (y_p.reshape(batch, seq, D_MODEL), y_s.reshape(dec_batch, dec_seq, D_MODEL),
            _from_cache_t(nkt_p, batch), _from_cache_t(nvt_p, batch),
            _from_cache_t(nkt_s, dec_batch), _from_cache_t(nvt_s, dec_batch),
            vrows.reshape(1, dec_batch, dec_seq, GMLP_HALF))
```

```python
import functools
import math

import jax
import jax.numpy as jnp
from jax import lax
from jax.experimental import pallas as pl
from jax.experimental.pallas import tpu as pltpu

D_MODEL = 1024
HEAD_DIM = 64
N_HEADS = 16
N_KV_HEADS = 4
GQA_GROUP = N_HEADS // N_KV_HEADS
KV_DIM = N_KV_HEADS * HEAD_DIM
QK_DIM = D_MODEL + KV_DIM
QKV_DIM = QK_DIM + KV_DIM
CHUNK = 64
WINDOW = 128
GMLP_CHUNK = 128
GMLP_HALF = 3 * D_MODEL
GMLP_GROUPS = 8
GMLP_GROUP_DIM = GMLP_HALF // GMLP_GROUPS
D_FF = 4 * D_MODEL
PLE_DIM = 256
EPS = 1e-6
NEG_INF = -1e30

LANES = 128
MXU_DIM_V7X = 256
BF16_SUBLANES = 16
VMEM_LIMIT_BYTES_V7X = 56 * 1024 * 1024

TOKEN_BLOCK = 512
FFN_TOKEN_BLOCK = 2 * TOKEN_BLOCK
GMLP_ROW_GROUPS = 1
ATTN_UNIT = 2 * CHUNK
ATTN_KEYS = 2 * WINDOW
PV_ROWS = HEAD_DIM + BF16_SUBLANES
FF_CHUNK = 1024
U_CHUNK = 2 * GMLP_GROUP_DIM

_BF16 = jnp.bfloat16
_F32 = jnp.float32
_CONTRACT_LAST = (((1,), (1,)), ((), ()))


def _resident(shape, layer=None):
    if layer is None:
        return pl.BlockSpec(shape, lambda *_: (0,) * len(shape), pipeline_mode=pl.Buffered(1))
    return pl.BlockSpec((None,) + tuple(shape), lambda *_: (layer,) + (0,) * len(shape),
                        pipeline_mode=pl.Buffered(1))


def _params(n_axes):
    return pltpu.CompilerParams(
        dimension_semantics=("arbitrary",) * n_axes,
        vmem_limit_bytes=VMEM_LIMIT_BYTES_V7X,
    )


def _rms(x, g):
    ms = jnp.mean(x * x, axis=-1, keepdims=True)
    return (x * lax.rsqrt(ms + EPS)) * g


def _dot(a, b):
    return jnp.dot(a, b, preferred_element_type=_F32)


def _qkv_kernel(x_ref, g_ref, w_ref, qkg_ref, q_ref, k_ref, v_ref):
    n = _rms(x_ref[...], g_ref[...]).astype(_BF16)
    qkv = _dot(n, w_ref[...])
    r = lax.broadcasted_iota(jnp.int32, (MXU_DIM_V7X, MXU_DIM_V7X), 0) // HEAD_DIM
    c = lax.broadcasted_iota(jnp.int32, (MXU_DIM_V7X, MXU_DIM_V7X), 1) // HEAD_DIM
    seg = jnp.where(r == c, 1.0, 0.0).astype(_BF16)
    normed = []
    for j in range(QK_DIM // MXU_DIM_V7X):
        cols = slice(j * MXU_DIM_V7X, (j + 1) * MXU_DIM_V7X)
        t = qkv[:, cols]
        ss = _dot((t * t).astype(_BF16), seg)
        inv = lax.rsqrt(ss * (1.0 / HEAD_DIM) + EPS)
        normed.append((t * inv) * qkg_ref[:, cols])
    q_ref[...] = jnp.concatenate(normed[:-1], axis=1).astype(_BF16)
    k_ref[...] = normed[-1]
    v_ref[...] = qkv[:, QK_DIM:]


def _qkv_proj(x, g, w_qkv, qk_gain):
    n_tok = x.shape[0]
    tb = TOKEN_BLOCK
    row = lambda width: pl.BlockSpec((tb, width), lambda i: (i, 0))
    return pl.pallas_call(
        _qkv_kernel,
        grid=(n_tok // tb,),
        in_specs=[row(D_MODEL), _resident((1, D_MODEL), layer=0),
                  _resident((D_MODEL, QKV_DIM), layer=0), _resident((1, QK_DIM))],
        out_specs=[row(D_MODEL), row(KV_DIM), row(KV_DIM)],
        out_shape=[jax.ShapeDtypeStruct((n_tok, D_MODEL), _BF16),
                   jax.ShapeDtypeStruct((n_tok, KV_DIM), _F32),
                   jax.ShapeDtypeStruct((n_tok, KV_DIM), _F32)],
        compiler_params=_params(1),
        name="qkv_proj",
    )(x, g, w_qkv, qk_gain)


def _alibi_slope(head):
    return 2.0 ** (-8.0 * (head + 1) / N_HEADS)


def _fill_bias_t(bias_ref, variant, valid_fn):
    s = lax.broadcasted_iota(jnp.int32, (ATTN_KEYS, ATTN_UNIT), 0)
    t = lax.broadcasted_iota(jnp.int32, (ATTN_KEYS, ATTN_UNIT), 1)
    dist = jnp.abs(t + WINDOW - s).astype(_F32)
    valid = valid_fn(t, s)
    for kvh in range(N_KV_HEADS):
        for p in range(2):
            for g in range(2):
                slope = _alibi_slope(kvh * GQA_GROUP + 2 * g + p)
                bias_ref[variant, kvh, p * ATTN_KEYS:(p + 1) * ATTN_KEYS,
                         g * ATTN_UNIT:(g + 1) * ATTN_UNIT] = jnp.where(valid, -slope * dist, NEG_INF)


def _ffn_ple_steps(h, p_bf16, gf_ref, w1_ref, w2_ref, gp_ref, wg_ref, wp_ref):
    n = _rms(h, gf_ref[...]).astype(_BF16)
    for c in range(D_FF // FF_CHUNK):
        cols = slice(c * FF_CHUNK, (c + 1) * FF_CHUNK)
        a = jnp.maximum(_dot(n, w1_ref[:, cols]), 0.0)
        yield
        h = h + _dot((a * a).astype(_BF16), w2_ref[cols, :])
        yield
    z = _dot(_rms(h, gp_ref[...]).astype(_BF16), wg_ref[...])
    yield
    e = _dot(p_bf16, wp_ref[...])
    yield
    gate = 1.0 / (1.0 + jnp.exp(-z))
    return h + gate * e


def _finish(steps):
    while True:
        try:
            next(steps)
        except StopIteration as stop:
            return stop.value


def _attn_ffn_prompt_kernel(sinks_ref, x_ref, q_ref, k_ref, kh_ref, v_ref, vh_ref, wo_ref, p_ref,
                            gf_ref, w1_ref, w2_ref, gp_ref, wg_ref, wp_ref,
                            out_ref, nkt_ref, nvt_ref,
                            bias_ref, klo_scr, khi_scr, vt_scr, ot_scr, h_scr, *, blocks_per_seq):
    step = pl.program_id(0)
    tb = TOKEN_BLOCK
    block = jnp.minimum(step, pl.num_programs(0) - 2)

    @pl.when(step == 0)
    def _():
        in_window = lambda t, s: ((t < CHUNK) & (s < WINDOW + CHUNK)) | ((t >= CHUNK) & (s >= CHUNK))
        _fill_bias_t(bias_ref, 0, in_window)
        _fill_bias_t(bias_ref, 1, lambda t, s: in_window(t, s) & (s >= WINDOW))
        vt_scr[:, HEAD_DIM:, :] = jnp.ones((N_KV_HEADS, PV_ROWS - HEAD_DIM, WINDOW + tb), _BF16)
        h_scr[...] = jnp.zeros_like(h_scr)

    mlp = _ffn_ple_steps(h_scr[...], p_ref[...].astype(_BF16), gf_ref, w1_ref, w2_ref, gp_ref,
                         wg_ref, wp_ref)
    next(mlp)

    kd = jnp.concatenate([kh_ref[...], k_ref[...]], axis=0)
    low_half = lax.broadcasted_iota(jnp.int32, (WINDOW + tb, LANES), 1) < HEAD_DIM
    for pair in range(N_KV_HEADS // 2):
        kk = kd[:, pair * LANES:(pair + 1) * LANES]
        swapped = pltpu.roll(kk, HEAD_DIM, axis=1)
        klo_scr[2 * pair] = jnp.where(low_half, kk, 0.0).astype(_BF16)
        khi_scr[2 * pair] = jnp.where(low_half, 0.0, swapped).astype(_BF16)
        klo_scr[2 * pair + 1] = jnp.where(low_half, swapped, 0.0).astype(_BF16)
        khi_scr[2 * pair + 1] = jnp.where(low_half, 0.0, kk).astype(_BF16)
    vt = jnp.concatenate([vh_ref[...], v_ref[...]], axis=0).T
    for kvh in range(N_KV_HEADS):
        vt_scr[kvh, :HEAD_DIM, :] = vt[kvh * HEAD_DIM:(kvh + 1) * HEAD_DIM, :].astype(_BF16)

    nkt_ref[...] = k_ref[tb - WINDOW:, :].T
    nvt_ref[...] = vt[:, tb:]

    first_in_seq = jnp.where(block % blocks_per_seq == 0, 1, 0)
    lane2 = lax.broadcasted_iota(jnp.int32, (1, 2 * ATTN_UNIT), 1)
    n_units = tb // ATTN_UNIT

    def scores(u):
        rows = slice(u * ATTN_UNIT, (u + 1) * ATTN_UNIT)
        keys = slice(u * ATTN_UNIT, u * ATTN_UNIT + ATTN_KEYS)
        variant = first_in_seq if u == 0 else 0
        out = []
        for kvh in range(N_KV_HEADS):
            k2 = jnp.concatenate([klo_scr[kvh, keys, :], khi_scr[kvh, keys, :]], axis=0)
            q2 = jnp.concatenate(
                [q_ref[rows, (2 * kvh + g) * LANES:(2 * kvh + g + 1) * LANES] for g in range(2)],
                axis=0)
            s_t = lax.dot_general(k2, q2, _CONTRACT_LAST, preferred_element_type=_F32)
            out.append(s_t + bias_ref[variant, kvh])
        return out

    s_next = scores(0)
    for u in range(n_units):
        rows = slice(u * ATTN_UNIT, (u + 1) * ATTN_UNIT)
        keys = slice(u * ATTN_UNIT, u * ATTN_UNIT + ATTN_KEYS)
        s_cur = s_next
        if u + 1 < n_units:
            s_next = scores(u + 1)
        next(mlp)
        chains = []
        for kvh in range(N_KV_HEADS):
            for p in range(2):
                head0 = kvh * GQA_GROUP + p
                sink = jnp.where(lane2 < ATTN_UNIT, sinks_ref[0, head0], sinks_ref[0, head0 + 2])
                sp = s_cur[kvh][p * ATTN_KEYS:(p + 1) * ATTN_KEYS, :]
                m = jnp.maximum(jnp.max(sp, axis=0, keepdims=True), sink)
                chains.append((head0, kvh, jnp.exp(sp - m).astype(_BF16), jnp.exp(sink - m)))
        results = [(head0, _dot(vt_scr[kvh, :, keys], e), sink_e)
                   for head0, kvh, e, sink_e in chains]
        next(mlp)
        for head0, r, sink_e in results:
            den = r[HEAD_DIM:HEAD_DIM + 1, :] + sink_e
            o = r[:HEAD_DIM, :] * (1.0 / den)
            for g in range(2):
                head = head0 + 2 * g
                ot_scr[head * HEAD_DIM:(head + 1) * HEAD_DIM, rows] = (
                    o[:, g * ATTN_UNIT:(g + 1) * ATTN_UNIT].astype(_BF16))

    out_ref[...] = _finish(mlp)
    h_scr[...] = x_ref[...] + _dot(ot_scr[...].T, wo_ref[...])


def _attn_ffn_prompt(sinks, x, q, k, v, w_o, p, g_ffn, w1, w2, g_ple, w_gate, w_proj, seq):
    tb = TOKEN_BLOCK
    layer = 0
    n_blocks = x.shape[0] // tb
    nt = seq // tb
    halo_per_block = tb // WINDOW
    attn_block = lambda s: jnp.minimum(s, n_blocks - 1)
    mlp_block = lambda s: jnp.maximum(s - 1, 0)
    row = lambda width: pl.BlockSpec((tb, width), lambda s: (attn_block(s), 0))
    halo = pl.BlockSpec(
        (WINDOW, KV_DIM), lambda s: (jnp.maximum(attn_block(s) * halo_per_block - 1, 0), 0))
    per_seq = pl.BlockSpec((KV_DIM, WINDOW), lambda s: (attn_block(s) // nt, 0))
    return pl.pallas_call(
        functools.partial(_attn_ffn_prompt_kernel, blocks_per_seq=nt),
        grid=(n_blocks + 1,),
        in_specs=[pl.BlockSpec(memory_space=pltpu.SMEM),
                  row(D_MODEL), row(D_MODEL), row(KV_DIM), halo, row(KV_DIM), halo,
                  _resident((D_MODEL, D_MODEL), layer),
                  pl.BlockSpec((None, tb, PLE_DIM), lambda s: (layer, mlp_block(s), 0)),
                  _resident((1, D_MODEL), layer),
                  _resident((D_MODEL, D_FF), layer), _resident((D_FF, D_MODEL), layer),
                  _resident((1, D_MODEL), layer), _resident((D_MODEL, D_MODEL), layer),
                  _resident((PLE_DIM, D_MODEL), layer)],
        out_specs=[pl.BlockSpec((tb, D_MODEL), lambda s: (mlp_block(s), 0)), per_seq, per_seq],
        out_shape=[jax.ShapeDtypeStruct(x.shape, _F32),
                   jax.ShapeDtypeStruct((x.shape[0] // seq * KV_DIM, WINDOW), _F32),
                   jax.ShapeDtypeStruct((x.shape[0] // seq * KV_DIM, WINDOW), _F32)],
        scratch_shapes=[
            pltpu.VMEM((2, N_KV_HEADS, 2 * ATTN_KEYS, 2 * ATTN_UNIT), _F32),
            pltpu.VMEM((N_KV_HEADS, WINDOW + tb, LANES), _BF16),
            pltpu.VMEM((N_KV_HEADS, WINDOW + tb, LANES), _BF16),
            pltpu.VMEM((N_KV_HEADS, PV_ROWS, WINDOW + tb), _BF16),
            pltpu.VMEM((D_MODEL, tb), _BF16),
            pltpu.VMEM((tb, D_MODEL), _F32),
        ],
        compiler_params=_params(1),
        name="attn_ffn_prompt",
    )(sinks, x, q, k, k, v, v, w_o, p, g_ffn, w1, w2, g_ple, w_gate, w_proj)


def _fill_bias(bias_new_ref, bias_old_ref, q_rows):
    t = lax.broadcasted_iota(jnp.int32, (q_rows, WINDOW), 0)
    s = lax.broadcasted_iota(jnp.int32, (q_rows, WINDOW), 1)
    dist_new = jnp.abs(t + WINDOW - (s + q_rows)).astype(_F32)
    dist_old = jnp.abs(t + WINDOW - s).astype(_F32)
    for kvh in range(N_KV_HEADS):
        for g in range(2):
            for p in range(2):
                slope = _alibi_slope(kvh * GQA_GROUP + 2 * g + p)
                rows = slice(g * q_rows, (g + 1) * q_rows)
                cols = slice(p * WINDOW, (p + 1) * WINDOW)
                bias_new_ref[kvh, rows, cols] = -slope * dist_new
                bias_old_ref[kvh, rows, cols] = jnp.where(s < q_rows, -slope * dist_old, NEG_INF)


def _block_diag2(a):
    z = jnp.zeros_like(a)
    return jnp.concatenate([jnp.concatenate([a, z], axis=1), jnp.concatenate([z, a], axis=1)],
                           axis=0)


def _attn_sample_kernel(sinks_ref, x_ref, q_ref, k_ref, ckt_ref, v_ref, cvt_ref, wo_ref,
                        out_ref, nkt_ref, nvt_ref, bias_new_ref, bias_old_ref, o_scr, *, dec_seq):
    kept = WINDOW - dec_seq

    @pl.when(pl.program_id(0) == 0)
    def _():
        _fill_bias(bias_new_ref, bias_old_ref, dec_seq)

    keep_lane = lax.broadcasted_iota(jnp.int32, (HEAD_DIM, WINDOW), 1) < kept
    pad = jnp.zeros((kept, LANES), _F32)
    ones_col = lax.broadcasted_iota(jnp.int32, (2 * BF16_SUBLANES, 2 * WINDOW), 1) // WINDOW
    ones_row = lax.broadcasted_iota(jnp.int32, (2 * BF16_SUBLANES, 2 * WINDOW), 0) // BF16_SUBLANES
    sum_rows = jnp.where(ones_col == ones_row, 1.0, 0.0).astype(_BF16)
    lane2 = lax.broadcasted_iota(jnp.int32, (2 * dec_seq, 2 * WINDOW), 1)
    lane1 = lax.broadcasted_iota(jnp.int32, (2 * dec_seq, LANES), 1)
    row2 = lax.broadcasted_iota(jnp.int32, (2 * dec_seq, 1), 0)

    def new_cols_t(rows_f32):
        return jnp.concatenate([pad, rows_f32], axis=0).T

    def one_batch(bb, carry):
        rows = pl.ds(pl.multiple_of(bb * dec_seq, dec_seq), dec_seq)
        k_new = k_ref[rows, :]
        v_new = v_ref[rows, :]
        kt_new = [new_cols_t(k_new[:, g * LANES:(g + 1) * LANES]) for g in range(KV_DIM // LANES)]
        vt_new = [new_cols_t(v_new[:, g * LANES:(g + 1) * LANES]) for g in range(KV_DIM // LANES)]

        staged = []
        for kvh in range(N_KV_HEADS):
            cache = pl.ds(pl.multiple_of(bb * KV_DIM + kvh * HEAD_DIM, HEAD_DIM), HEAD_DIM)
            kt_old = ckt_ref[cache, :]
            vt_old = cvt_ref[cache, :]
            half = slice((kvh % 2) * HEAD_DIM, (kvh % 2 + 1) * HEAD_DIM)
            kt = jnp.where(keep_lane, pltpu.roll(kt_old, kept, axis=1), kt_new[kvh // 2][half, :])
            vt = jnp.where(keep_lane, pltpu.roll(vt_old, kept, axis=1), vt_new[kvh // 2][half, :])
            nkt_ref[cache, :] = kt
            nvt_ref[cache, :] = vt
            q2 = jnp.concatenate(
                [q_ref[rows, (2 * kvh + g) * LANES:(2 * kvh + g + 1) * LANES] for g in range(2)],
                axis=0)
            s_new = _dot(q2, _block_diag2(kt.astype(_BF16))) + bias_new_ref[kvh]
            s_old = _dot(q2, _block_diag2(kt_old.astype(_BF16))) + bias_old_ref[kvh]
            v2_new = jnp.concatenate([_block_diag2(vt.astype(_BF16)), sum_rows], axis=0)
            v2_old = jnp.concatenate([_block_diag2(vt_old.astype(_BF16)), sum_rows], axis=0)
            staged.append((s_new, s_old, v2_new, v2_old))

        soft = []
        for kvh, (s_new, s_old, v2_new, v2_old) in enumerate(staged):
            s_max = jnp.maximum(s_new, s_old)
            head0 = kvh * GQA_GROUP
            ms, sink_es = [], []
            for p in range(2):
                sink = jnp.where(row2 < dec_seq, sinks_ref[0, head0 + p], sinks_ref[0, head0 + 2 + p])
                m_p = jnp.maximum(
                    jnp.max(s_max[:, p * WINDOW:(p + 1) * WINDOW], axis=-1, keepdims=True), sink)
                ms.append(m_p)
                sink_es.append(jnp.exp(sink - m_p))
            m = jnp.where(lane2 < WINDOW, ms[0], ms[1])
            soft.append((jnp.exp(s_new - m).astype(_BF16), jnp.exp(s_old - m).astype(_BF16),
                         sink_es, v2_new, v2_old))

        outs = [(lax.dot_general(e_new, v2_new, _CONTRACT_LAST, preferred_element_type=_F32)
                 + lax.dot_general(e_old, v2_old, _CONTRACT_LAST, preferred_element_type=_F32),
                 sink_es) for e_new, e_old, sink_es, v2_new, v2_old in soft]
        for kvh, (r, sink_es) in enumerate(outs):
            dens = [r[:, LANES + p * BF16_SUBLANES:LANES + p * BF16_SUBLANES + 1] + sink_es[p]
                    for p in range(2)]
            inv = jnp.where(lane1 < HEAD_DIM, 1.0 / dens[0], 1.0 / dens[1])
            o = (r[:, :LANES] * inv).astype(_BF16)
            for g in range(2):
                o_scr[rows, (2 * kvh + g) * LANES:(2 * kvh + g + 1) * LANES] = (
                    o[g * dec_seq:(g + 1) * dec_seq, :])
        return carry

    lax.fori_loop(0, TOKEN_BLOCK // dec_seq, one_batch, 0, unroll=2)
    out_ref[...] = x_ref[...] + _dot(o_scr[...], wo_ref[...])


def _attn_sample(sinks, x, q, k, v, cache_kt, cache_vt, w_o, dec_seq):
    n_tok = x.shape[0]
    tb = TOKEN_BLOCK
    batches_per_block = tb // dec_seq
    row = lambda width: pl.BlockSpec((tb, width), lambda i: (i, 0))
    cache = pl.BlockSpec((batches_per_block * KV_DIM, WINDOW), lambda i: (i, 0))
    return pl.pallas_call(
        functools.partial(_attn_sample_kernel, dec_seq=dec_seq),
        grid=(n_tok // tb,),
        in_specs=[pl.BlockSpec(memory_space=pltpu.SMEM),
                  row(D_MODEL), row(D_MODEL), row(KV_DIM), cache, row(KV_DIM), cache,
                  _resident((D_MODEL, D_MODEL), layer=0)],
        out_specs=[row(D_MODEL), cache, cache],
        out_shape=[jax.ShapeDtypeStruct(x.shape, _F32),
                   jax.ShapeDtypeStruct(cache_kt.shape, _F32),
                   jax.ShapeDtypeStruct(cache_vt.shape, _F32)],
        scratch_shapes=[
            pltpu.VMEM((N_KV_HEADS, 2 * dec_seq, 2 * WINDOW), _F32),
            pltpu.VMEM((N_KV_HEADS, 2 * dec_seq, 2 * WINDOW), _F32),
            pltpu.VMEM((tb, D_MODEL), _BF16),
        ],
        compiler_params=_params(1),
        name="attn_sample",
    )(sinks, x, q, k, cache_kt, v, cache_vt, w_o)


def _ffn_ple_kernel(h_ref, p_ref, gf_ref, w1_ref, w2_ref, gp_ref, wg_ref, wp_ref, out_ref):
    n_groups = h_ref.shape[0] // TOKEN_BLOCK
    groups = [slice(r * TOKEN_BLOCK, (r + 1) * TOKEN_BLOCK) for r in range(n_groups)]
    runs = [_ffn_ple_steps(h_ref[rows, :], p_ref[rows, :].astype(_BF16), gf_ref, w1_ref, w2_ref,
                           gp_ref, wg_ref, wp_ref) for rows in groups]
    results = {}
    while len(results) < n_groups:
        for rows, run in zip(groups, runs):
            try:
                next(run)
            except StopIteration as stop:
                results[rows.start] = stop.value
    for rows in groups:
        out_ref[rows, :] = results[rows.start]


def _ffn_ple(h, p, g_ffn, w1, w2, g_ple, w_gate, w_proj, layer):
    n_tok = h.shape[0]
    tb = FFN_TOKEN_BLOCK
    row = lambda width: pl.BlockSpec((tb, width), lambda i: (i, 0))
    return pl.pallas_call(
        _ffn_ple_kernel,
        grid=(n_tok // tb,),
        in_specs=[row(D_MODEL), pl.BlockSpec((None, tb, PLE_DIM), lambda i: (layer, i, 0)),
                  _resident((1, D_MODEL), layer),
                  _resident((D_MODEL, D_FF), layer), _resident((D_FF, D_MODEL), layer),
                  _resident((1, D_MODEL), layer), _resident((D_MODEL, D_MODEL), layer),
                  _resident((PLE_DIM, D_MODEL), layer)],
        out_specs=row(D_MODEL),
        out_shape=jax.ShapeDtypeStruct(h.shape, _F32),
        compiler_params=_params(1),
        name="ffn_ple",
    )(h, p, g_ffn, w1, w2, g_ple, w_gate, w_proj)


def _gelu_tanh(x):
    c = math.sqrt(2.0 / math.pi)
    return x * (0.5 * (1.0 + jnp.tanh(c * (x + 0.044715 * (x * x * x)))))


def _gmlp_kernel(h_ref, g_ref, wuv_ref, vg_ref, ws_ref, bs_ref, wout_ref, out_ref, *rest,
                 emit_v):
    if emit_v:
        vout_ref, v_scr, vb_scr = rest
    else:
        v_scr, vb_scr = rest
    tb = h_ref.shape[0]
    group_rows = tb // GMLP_ROW_GROUPS
    row_groups = [slice(r * group_rows, (r + 1) * group_rows) for r in range(GMLP_ROW_GROUPS)]
    hs = [h_ref[rows, :] for rows in row_groups]
    ns = [_rms(h, g_ref[...]).astype(_BF16) for h in hs]

    ssq = [jnp.zeros((group_rows, 1), _F32) for _ in row_groups]
    for c in range(GMLP_HALF // FF_CHUNK):
        cols = slice(c * FF_CHUNK, (c + 1) * FF_CHUNK)
        w_cols = slice(GMLP_HALF + c * FF_CHUNK, GMLP_HALF + (c + 1) * FF_CHUNK)
        vcs = [_gelu_tanh(_dot(n, wuv_ref[:, w_cols])) for n in ns]
        for r, (rows, vc) in enumerate(zip(row_groups, vcs)):
            v_scr[rows, cols] = vc
            ssq[r] = ssq[r] + jnp.sum(vc * vc, axis=-1, keepdims=True)
    invs = [lax.rsqrt(s * (1.0 / GMLP_HALF) + EPS) for s in ssq]
    for c in range(GMLP_HALF // FF_CHUNK):
        cols = slice(c * FF_CHUNK, (c + 1) * FF_CHUNK)
        for rows, inv in zip(row_groups, invs):
            vn = (v_scr[rows, cols] * inv) * vg_ref[:, cols]
            if emit_v:
                vout_ref[rows, cols] = vn
            vb_scr[rows, cols] = vn.astype(_BF16)

    groups_per_step = U_CHUNK // GMLP_GROUP_DIM
    for c in range(GMLP_HALF // U_CHUNK):
        us = [_gelu_tanh(_dot(n, wuv_ref[:, c * U_CHUNK:(c + 1) * U_CHUNK])) for n in ns]
        gateds = []
        for r, u in enumerate(us):
            gated_rows = []
            for chunk in range(group_rows // GMLP_CHUNK):
                local = slice(chunk * GMLP_CHUNK, (chunk + 1) * GMLP_CHUNK)
                rows = slice(r * group_rows + chunk * GMLP_CHUNK,
                             r * group_rows + (chunk + 1) * GMLP_CHUNK)
                parts = []
                for gg in range(groups_per_step):
                    grp = c * groups_per_step + gg
                    cols = slice(grp * GMLP_GROUP_DIM, (grp + 1) * GMLP_GROUP_DIM)
                    s = _dot(ws_ref[grp], vb_scr[rows, cols]) + bs_ref[:, grp:grp + 1]
                    parts.append(u[local, gg * GMLP_GROUP_DIM:(gg + 1) * GMLP_GROUP_DIM] * s)
                gated_rows.append(jnp.concatenate(parts, axis=1))
            gateds.append(jnp.concatenate(gated_rows, axis=0).astype(_BF16))
        hs = [h + _dot(gated, wout_ref[c * U_CHUNK:(c + 1) * U_CHUNK, :])
              for h, gated in zip(hs, gateds)]
    for rows, h in zip(row_groups, hs):
        out_ref[rows, :] = h


def _gmlp(h, g, w_uv, v_gain, ws, bs_t, w_out, emit_v):
    n_tok = h.shape[0]
    tb = TOKEN_BLOCK
    row = lambda width: pl.BlockSpec((tb, width), lambda i: (i, 0))
    out_specs = [row(D_MODEL)]
    out_shape = [jax.ShapeDtypeStruct(h.shape, _F32)]
    if emit_v:
        out_specs.append(row(GMLP_HALF))
        out_shape.append(jax.ShapeDtypeStruct((n_tok, GMLP_HALF), _F32))
    return pl.pallas_call(
        functools.partial(_gmlp_kernel, emit_v=emit_v),
        grid=(n_tok // tb,),
        in_specs=[row(D_MODEL), _resident((1, D_MODEL), layer=1),
                  _resident((D_MODEL, 2 * GMLP_HALF), layer=0),
                  _resident((1, GMLP_HALF)),
                  _resident((GMLP_GROUPS, GMLP_CHUNK, GMLP_CHUNK)),
                  _resident((GMLP_CHUNK, GMLP_GROUPS)),
                  _resident((GMLP_HALF, D_MODEL), layer=0)],
        out_specs=out_specs,
        out_shape=out_shape,
        scratch_shapes=[pltpu.VMEM((tb, GMLP_HALF), _F32), pltpu.VMEM((tb, GMLP_HALF), _BF16)],
        compiler_params=_params(1),
        name="gmlp_v" if emit_v else "gmlp",
    )(h, g, w_uv, v_gain, ws, bs_t, w_out)


def _gmlp_spatial_weights(w_s, b_s, length):
    tril = jnp.tril(jnp.ones((length, length), w_s.dtype))
    ws = w_s[:, :length, :length] * tril
    reps = GMLP_CHUNK // length
    if reps > 1:
        eye = jnp.eye(reps, dtype=w_s.dtype)
        ws = jnp.einsum("ab,gij->gaibj", eye, ws).reshape(GMLP_GROUPS, GMLP_CHUNK, GMLP_CHUNK)
    bs_t = jnp.tile(b_s[:, :length].T, (reps, 1))
    return ws.astype(_BF16), bs_t


def _to_cache_t(cache):
    _, b, rows, kvh, hd = cache.shape
    return jnp.transpose(cache, (0, 1, 3, 4, 2)).reshape(b * kvh * hd, rows)


def _from_cache_t(cache_t, batch):
    t = cache_t.reshape(1, batch, N_KV_HEADS, HEAD_DIM, cache_t.shape[1])
    return jnp.transpose(t, (0, 1, 4, 2, 3))


def _trunk(x, p, cache, w, batch, seq):
    ffn_ple = lambda h, layer: _ffn_ple(h, p, w["g_ffn"], w["w1"], w["w2"], w["g_ple"],
                                        w["w_gate"], w["w_proj"], layer)
    q, k, v = _qkv_proj(x, w["g_mix"], w["w_qkv"], w["qk_gain"])
    if cache is None:
        h, nkt, nvt = _attn_ffn_prompt(w["sinks"], x, q, k, v, w["w_o"], p, w["g_ffn"], w["w1"],
                                       w["w2"], w["g_ple"], w["w_gate"], w["w_proj"], seq)
    else:
        h, nkt, nvt = _attn_sample(w["sinks"], x, q, k, v, cache[0], cache[1], w["w_o"], seq)
        h = ffn_ple(h, 0)
    length = min(seq, GMLP_CHUNK)
    ws, bs_t = _gmlp_spatial_weights(w["w_s"], w["b_s"], length)
    emit_v = cache is not None
    res = _gmlp(h, w["g_mix"], w["w_uv"], w["v_gain"], ws, bs_t, w["w_out"], emit_v)
    h = res[0]
    v_rows = res[1] if emit_v else None
    h = ffn_ple(h, 1)
    return h, nkt, nvt, v_rows


def kernel(x_prompt, x_sample, p_prompt, p_sample, cache_k, cache_v, g_mix, g_ffn, g_ple,
           attn_w_qkv, attn_q_norm, attn_k_norm, attn_sinks, attn_w_o, gmlp_w_uv, gmlp_v_norm,
           gmlp_w_s, gmlp_b_s, gmlp_w_out, ffn_w1, ffn_w2, ple_w_proj, ple_w_gate):
    batch, seq, _ = x_prompt.shape
    dec_batch, dec_seq, _ = x_sample.shape
    depth = g_mix.shape[0]
    cache_rows = cache_k.shape[2]
    assert depth == 2 and attn_w_qkv.shape[0] == 1 and gmlp_w_uv.shape[0] == 1
    assert seq % FFN_TOKEN_BLOCK == 0 and (dec_batch * dec_seq) % FFN_TOKEN_BLOCK == 0
    assert TOKEN_BLOCK % (2 * dec_seq) == 0 and GMLP_CHUNK % dec_seq == 0 and dec_seq % 16 == 0
    assert cache_rows == WINDOW and dec_seq <= WINDOW

    scale = HEAD_DIM ** -0.5
    w = {
        "g_mix": g_mix.reshape(depth, 1, D_MODEL),
        "g_ffn": g_ffn.reshape(depth, 1, D_MODEL),
        "g_ple": g_ple.reshape(depth, 1, D_MODEL),
        "w_qkv": attn_w_qkv.astype(_BF16),
        "qk_gain": jnp.concatenate([jnp.tile(attn_q_norm[0] * scale, N_HEADS),
                                    jnp.tile(attn_k_norm[0], N_KV_HEADS)]).reshape(1, QK_DIM),
        "sinks": attn_sinks[0].reshape(1, N_HEADS),
        "w_o": attn_w_o.astype(_BF16),
        "w_uv": gmlp_w_uv.astype(_BF16),
        "v_gain": gmlp_v_norm[0].reshape(1, GMLP_HALF),
        "w_s": gmlp_w_s[0],
        "b_s": gmlp_b_s[0],
        "w_out": gmlp_w_out.astype(_BF16),
        "w1": ffn_w1.astype(_BF16),
        "w2": ffn_w2.astype(_BF16),
        "w_gate": ple_w_gate.astype(_BF16),
        "w_proj": ple_w_proj.astype(_BF16),
    }

    n_prompt = batch * seq
    n_sample = dec_batch * dec_seq
    y_p, nkt_p, nvt_p, _ = _trunk(x_prompt.reshape(n_prompt, D_MODEL),
                                  p_prompt.reshape(depth, n_prompt, PLE_DIM), None, w, batch, seq)
    cache = (_to_cache_t(cache_k), _to_cache_t(cache_v))
    y_s, nkt_s, nvt_s, vrows = _trunk(x_sample.reshape(n_sample, D_MODEL),
                                      p_sample.reshape(depth, n_sample, PLE_DIM), cache, w,
                                      dec_batch, dec_seq)
    return (y_p.reshape(batch, seq, D_MODEL), y_s.reshape(dec_batch, dec_seq, D_MODEL),
            _from_cache_t(nkt_p, batch), _from_cache_t(nvt_p, batch),
            _from_cache_t(nkt_s, dec_batch), _from_cache_t(nvt_s, dec_batch),
            vrows.reshape(1, dec_batch, dec_seq, GMLP_HALF))
```

```python
import functools
import math

import jax
import jax.numpy as jnp
from jax import lax
from jax.experimental import pallas as pl
from jax.experimental.pallas import tpu as pltpu

D_MODEL = 1024
HEAD_DIM = 64
N_HEADS = 16
N_KV_HEADS = 4
GQA_GROUP = N_HEADS // N_KV_HEADS
KV_DIM = N_KV_HEADS * HEAD_DIM
QK_DIM = D_MODEL + KV_DIM
QKV_DIM = QK_DIM + KV_DIM
CHUNK = 64
WINDOW = 128
GMLP_CHUNK = 128
GMLP_HALF = 3 * D_MODEL
GMLP_GROUPS = 8
GMLP_GROUP_DIM = GMLP_HALF // GMLP_GROUPS
D_FF = 4 * D_MODEL
PLE_DIM = 256
EPS = 1e-6
NEG_INF = -1e30

LANES = 128
MXU_DIM_V7X = 256
BF16_SUBLANES = 16
VMEM_LIMIT_BYTES_V7X = 56 * 1024 * 1024

TOKEN_BLOCK = 512
FFN_TOKEN_BLOCK = 2 * TOKEN_BLOCK
GMLP_ROW_GROUPS = 1
ATTN_UNIT = 2 * CHUNK
ATTN_KEYS = 2 * WINDOW
PV_ROWS = HEAD_DIM + BF16_SUBLANES
FF_CHUNK = 1024
U_CHUNK = 2 * GMLP_GROUP_DIM

_BF16 = jnp.bfloat16
_F32 = jnp.float32
_CONTRACT_LAST = (((1,), (1,)), ((), ()))


def _resident(shape, layer=None):
    if layer is None:
        return pl.BlockSpec(shape, lambda *_: (0,) * len(shape), pipeline_mode=pl.Buffered(1))
    return pl.BlockSpec((None,) + tuple(shape), lambda *_: (layer,) + (0,) * len(shape),
                        pipeline_mode=pl.Buffered(1))


def _params(n_axes):
    return pltpu.CompilerParams(
        dimension_semantics=("arbitrary",) * n_axes,
        vmem_limit_bytes=VMEM_LIMIT_BYTES_V7X,
    )


def _rms(x, g):
    ms = jnp.mean(x * x, axis=-1, keepdims=True)
    return (x * lax.rsqrt(ms + EPS)) * g


def _dot(a, b):
    return jnp.dot(a, b, preferred_element_type=_F32)


class _WeightCasts:
    def __init__(self, weights, n_slabs):
        self.arrays = [arr for arr, _ in weights]
        self.in_specs, self.out_specs, self.out_shape = [], [], []
        for arr, layer in weights:
            rows, cols = arr.shape[-2:]
            slab = rows // n_slabs
            if rows % n_slabs or slab % BF16_SUBLANES:
                slab, n_slabs_w = rows, 1
            else:
                n_slabs_w = n_slabs
            row_block = lambda s, last=n_slabs_w - 1: jnp.minimum(s, last)
            self.in_specs.append(pl.BlockSpec(
                (None, slab, cols), lambda s, f=row_block, l=layer: (l, f(s), 0)))
            self.out_specs.append(pl.BlockSpec((slab, cols), lambda s, f=row_block: (f(s), 0)))
            self.out_shape.append(jax.ShapeDtypeStruct((rows, cols), _BF16))

    def __len__(self):
        return len(self.arrays)


def _cast_slabs(in_refs, out_refs):
    for src, dst in zip(in_refs, out_refs):
        dst[...] = src[...].astype(_BF16)


def _qkv_kernel(*refs, n_casts):
    x_ref, g_ref, w_ref, qkg_ref = refs[:4]
    q_ref, k_ref, v_ref = refs[4 + n_casts:7 + n_casts]
    _cast_slabs(refs[4:4 + n_casts], refs[7 + n_casts:])
    n = _rms(x_ref[...], g_ref[...]).astype(_BF16)
    qkv = _dot(n, w_ref[...])
    r = lax.broadcasted_iota(jnp.int32, (MXU_DIM_V7X, MXU_DIM_V7X), 0) // HEAD_DIM
    c = lax.broadcasted_iota(jnp.int32, (MXU_DIM_V7X, MXU_DIM_V7X), 1) // HEAD_DIM
    seg = jnp.where(r == c, 1.0, 0.0).astype(_BF16)
    normed = []
    for j in range(QK_DIM // MXU_DIM_V7X):
        cols = slice(j * MXU_DIM_V7X, (j + 1) * MXU_DIM_V7X)
        t = qkv[:, cols]
        ss = _dot((t * t).astype(_BF16), seg)
        inv = lax.rsqrt(ss * (1.0 / HEAD_DIM) + EPS)
        normed.append((t * inv) * qkg_ref[:, cols])
    q_ref[...] = jnp.concatenate(normed[:-1], axis=1).astype(_BF16)
    k_ref[...] = normed[-1]
    v_ref[...] = qkv[:, QK_DIM:]


def _qkv_proj(x, g, w_qkv, qk_gain, cast_weights=()):
    n_tok = x.shape[0]
    tb = TOKEN_BLOCK
    casts = _WeightCasts(cast_weights, n_tok // tb)
    row = lambda width: pl.BlockSpec((tb, width), lambda i: (i, 0))
    return pl.pallas_call(
        functools.partial(_qkv_kernel, n_casts=len(casts)),
        grid=(n_tok // tb,),
        in_specs=[row(D_MODEL), _resident((1, D_MODEL), layer=0),
                  _resident((D_MODEL, QKV_DIM), layer=0), _resident((1, QK_DIM))] + casts.in_specs,
        out_specs=[row(D_MODEL), row(KV_DIM), row(KV_DIM)] + casts.out_specs,
        out_shape=[jax.ShapeDtypeStruct((n_tok, D_MODEL), _BF16),
                   jax.ShapeDtypeStruct((n_tok, KV_DIM), _F32),
                   jax.ShapeDtypeStruct((n_tok, KV_DIM), _F32)] + casts.out_shape,
        compiler_params=_params(1),
        name="qkv_proj",
    )(x, g, w_qkv, qk_gain, *casts.arrays)


def _alibi_slope(head):
    return 2.0 ** (-8.0 * (head + 1) / N_HEADS)


def _fill_bias_t(bias_ref, variant, valid_fn):
    s = lax.broadcasted_iota(jnp.int32, (ATTN_KEYS, ATTN_UNIT), 0)
    t = lax.broadcasted_iota(jnp.int32, (ATTN_KEYS, ATTN_UNIT), 1)
    dist = jnp.abs(t + WINDOW - s).astype(_F32)
    valid = valid_fn(t, s)
    for kvh in range(N_KV_HEADS):
        for p in range(2):
            for g in range(2):
                slope = _alibi_slope(kvh * GQA_GROUP + 2 * g + p)
                bias_ref[variant, kvh, p * ATTN_KEYS:(p + 1) * ATTN_KEYS,
                         g * ATTN_UNIT:(g + 1) * ATTN_UNIT] = jnp.where(valid, -slope * dist, NEG_INF)


def _ffn_ple_steps(h, p_bf16, gf_ref, w1_ref, w2_ref, gp_ref, wg_ref, wp_ref):
    n = _rms(h, gf_ref[...]).astype(_BF16)
    for c in range(D_FF // FF_CHUNK):
        cols = slice(c * FF_CHUNK, (c + 1) * FF_CHUNK)
        a = jnp.maximum(_dot(n, w1_ref[:, cols]), 0.0)
        yield
        h = h + _dot((a * a).astype(_BF16), w2_ref[cols, :])
        yield
    z = _dot(_rms(h, gp_ref[...]).astype(_BF16), wg_ref[...])
    yield
    e = _dot(p_bf16, wp_ref[...])
    yield
    gate = 1.0 / (1.0 + jnp.exp(-z))
    return h + gate * e


def _finish(steps):
    while True:
        try:
            next(steps)
        except StopIteration as stop:
            return stop.value


def _attn_ffn_prompt_kernel(*refs, blocks_per_seq, n_casts):
    (sinks_ref, x_ref, q_ref, k_ref, kh_ref, v_ref, vh_ref, wo_ref, p_ref,
     gf_ref, w1_ref, w2_ref, gp_ref, wg_ref, wp_ref) = refs[:15]
    out_ref, nkt_ref, nvt_ref = refs[15 + n_casts:18 + n_casts]
    bias_ref, klo_scr, khi_scr, vt_scr, ot_scr, h_scr = refs[18 + 2 * n_casts:]
    _cast_slabs(refs[15:15 + n_casts], refs[18 + n_casts:18 + 2 * n_casts])
    step = pl.program_id(0)
    tb = TOKEN_BLOCK
    block = jnp.minimum(step, pl.num_programs(0) - 2)

    @pl.when(step == 0)
    def _():
        in_window = lambda t, s: ((t < CHUNK) & (s < WINDOW + CHUNK)) | ((t >= CHUNK) & (s >= CHUNK))
        _fill_bias_t(bias_ref, 0, in_window)
        _fill_bias_t(bias_ref, 1, lambda t, s: in_window(t, s) & (s >= WINDOW))
        vt_scr[:, HEAD_DIM:, :] = jnp.ones((N_KV_HEADS, PV_ROWS - HEAD_DIM, WINDOW + tb), _BF16)
        h_scr[...] = jnp.zeros_like(h_scr)

    mlp = _ffn_ple_steps(h_scr[...], p_ref[...].astype(_BF16), gf_ref, w1_ref, w2_ref, gp_ref,
                         wg_ref, wp_ref)
    next(mlp)

    kd = jnp.concatenate([kh_ref[...], k_ref[...]], axis=0)
    low_half = lax.broadcasted_iota(jnp.int32, (WINDOW + tb, LANES), 1) < HEAD_DIM
    for pair in range(N_KV_HEADS // 2):
        kk = kd[:, pair * LANES:(pair + 1) * LANES]
        swapped = pltpu.roll(kk, HEAD_DIM, axis=1)
        klo_scr[2 * pair] = jnp.where(low_half, kk, 0.0).astype(_BF16)
        khi_scr[2 * pair] = jnp.where(low_half, 0.0, swapped).astype(_BF16)
        klo_scr[2 * pair + 1] = jnp.where(low_half, swapped, 0.0).astype(_BF16)
        khi_scr[2 * pair + 1] = jnp.where(low_half, 0.0, kk).astype(_BF16)
    vt = jnp.concatenate([vh_ref[...], v_ref[...]], axis=0).T
    for kvh in range(N_KV_HEADS):
        vt_scr[kvh, :HEAD_DIM, :] = vt[kvh * HEAD_DIM:(kvh + 1) * HEAD_DIM, :].astype(_BF16)

    nkt_ref[...] = k_ref[tb - WINDOW:, :].T
    nvt_ref[...] = vt[:, tb:]

    first_in_seq = jnp.where(block % blocks_per_seq == 0, 1, 0)
    lane2 = lax.broadcasted_iota(jnp.int32, (1, 2 * ATTN_UNIT), 1)
    n_units = tb // ATTN_UNIT

    def scores(u):
        rows = slice(u * ATTN_UNIT, (u + 1) * ATTN_UNIT)
        keys = slice(u * ATTN_UNIT, u * ATTN_UNIT + ATTN_KEYS)
        variant = first_in_seq if u == 0 else 0
        out = []
        for kvh in range(N_KV_HEADS):
            k2 = jnp.concatenate([klo_scr[kvh, keys, :], khi_scr[kvh, keys, :]], axis=0)
            q2 = jnp.concatenate(
                [q_ref[rows, (2 * kvh + g) * LANES:(2 * kvh + g + 1) * LANES] for g in range(2)],
                axis=0)
            s_t = lax.dot_general(k2, q2, _CONTRACT_LAST, preferred_element_type=_F32)
            out.append(s_t + bias_ref[variant, kvh])
        return out

    s_next = scores(0)
    for u in range(n_units):
        rows = slice(u * ATTN_UNIT, (u + 1) * ATTN_UNIT)
        keys = slice(u * ATTN_UNIT, u * ATTN_UNIT + ATTN_KEYS)
        s_cur = s_next
        if u + 1 < n_units:
            s_next = scores(u + 1)
        next(mlp)
        chains = []
        for kvh in range(N_KV_HEADS):
            for p in range(2):
                head0 = kvh * GQA_GROUP + p
                sink = jnp.where(lane2 < ATTN_UNIT, sinks_ref[0, head0], sinks_ref[0, head0 + 2])
                sp = s_cur[kvh][p * ATTN_KEYS:(p + 1) * ATTN_KEYS, :]
                m = jnp.maximum(jnp.max(sp, axis=0, keepdims=True), sink)
                chains.append((head0, kvh, jnp.exp(sp - m).astype(_BF16), jnp.exp(sink - m)))
        results = [(head0, _dot(vt_scr[kvh, :, keys], e), sink_e)
                   for head0, kvh, e, sink_e in chains]
        next(mlp)
        for head0, r, sink_e in results:
            den = r[HEAD_DIM:HEAD_DIM + 1, :] + sink_e
            o = r[:HEAD_DIM, :] * (1.0 / den)
            for g in range(2):
                head = head0 + 2 * g
                ot_scr[head * HEAD_DIM:(head + 1) * HEAD_DIM, rows] = (
                    o[:, g * ATTN_UNIT:(g + 1) * ATTN_UNIT].astype(_BF16))

    out_ref[...] = _finish(mlp)
    h_scr[...] = x_ref[...] + _dot(ot_scr[...].T, wo_ref[...])


def _attn_ffn_prompt(sinks, x, q, k, v, w_o, p, g_ffn, w1, w2, g_ple, w_gate, w_proj, seq,
                     cast_weights=()):
    tb = TOKEN_BLOCK
    layer = 0
    n_blocks = x.shape[0] // tb
    casts = _WeightCasts(cast_weights, n_blocks)
    nt = seq // tb
    halo_per_block = tb // WINDOW
    attn_block = lambda s: jnp.minimum(s, n_blocks - 1)
    mlp_block = lambda s: jnp.maximum(s - 1, 0)
    row = lambda width: pl.BlockSpec((tb, width), lambda s: (attn_block(s), 0))
    halo = pl.BlockSpec(
        (WINDOW, KV_DIM), lambda s: (jnp.maximum(attn_block(s) * halo_per_block - 1, 0), 0))
    per_seq = pl.BlockSpec((KV_DIM, WINDOW), lambda s: (attn_block(s) // nt, 0))
    return pl.pallas_call(
        functools.partial(_attn_ffn_prompt_kernel, blocks_per_seq=nt, n_casts=len(casts)),
        grid=(n_blocks + 1,),
        in_specs=[pl.BlockSpec(memory_space=pltpu.SMEM),
                  row(D_MODEL), row(D_MODEL), row(KV_DIM), halo, row(KV_DIM), halo,
                  _resident((D_MODEL, D_MODEL)),
                  pl.BlockSpec((None, tb, PLE_DIM), lambda s: (layer, mlp_block(s), 0)),
                  _resident((1, D_MODEL), layer),
                  _resident((D_MODEL, D_FF)), _resident((D_FF, D_MODEL)),
                  _resident((1, D_MODEL), layer), _resident((D_MODEL, D_MODEL)),
                  _resident((PLE_DIM, D_MODEL))] + casts.in_specs,
        out_specs=[pl.BlockSpec((tb, D_MODEL), lambda s: (mlp_block(s), 0)), per_seq,
                   per_seq] + casts.out_specs,
        out_shape=[jax.ShapeDtypeStruct(x.shape, _F32),
                   jax.ShapeDtypeStruct((x.shape[0] // seq * KV_DIM, WINDOW), _F32),
                   jax.ShapeDtypeStruct((x.shape[0] // seq * KV_DIM, WINDOW), _F32)]
        + casts.out_shape,
        scratch_shapes=[
            pltpu.VMEM((2, N_KV_HEADS, 2 * ATTN_KEYS, 2 * ATTN_UNIT), _F32),
            pltpu.VMEM((N_KV_HEADS, WINDOW + tb, LANES), _BF16),
            pltpu.VMEM((N_KV_HEADS, WINDOW + tb, LANES), _BF16),
            pltpu.VMEM((N_KV_HEADS, PV_ROWS, WINDOW + tb), _BF16),
            pltpu.VMEM((D_MODEL, tb), _BF16),
            pltpu.VMEM((tb, D_MODEL), _F32),
        ],
        compiler_params=_params(1),
        name="attn_ffn_prompt",
    )(sinks, x, q, k, k, v, v, w_o, p, g_ffn, w1, w2, g_ple, w_gate, w_proj, *casts.arrays)


def _fill_bias(bias_new_ref, bias_old_ref, q_rows):
    t = lax.broadcasted_iota(jnp.int32, (q_rows, WINDOW), 0)
    s = lax.broadcasted_iota(jnp.int32, (q_rows, WINDOW), 1)
    dist_new = jnp.abs(t + WINDOW - (s + q_rows)).astype(_F32)
    dist_old = jnp.abs(t + WINDOW - s).astype(_F32)
    for kvh in range(N_KV_HEADS):
        for g in range(2):
            for p in range(2):
                slope = _alibi_slope(kvh * GQA_GROUP + 2 * g + p)
                rows = slice(g * q_rows, (g + 1) * q_rows)
                cols = slice(p * WINDOW, (p + 1) * WINDOW)
                bias_new_ref[kvh, rows, cols] = -slope * dist_new
                bias_old_ref[kvh, rows, cols] = jnp.where(s < q_rows, -slope * dist_old, NEG_INF)


def _block_diag2(a):
    z = jnp.zeros_like(a)
    return jnp.concatenate([jnp.concatenate([a, z], axis=1), jnp.concatenate([z, a], axis=1)],
                           axis=0)


def _attn_sample_kernel(sinks_ref, x_ref, q_ref, k_ref, ckt_ref, v_ref, cvt_ref, wo_ref,
                        out_ref, nkt_ref, nvt_ref, bias_new_ref, bias_old_ref, o_scr, *, dec_seq):
    kept = WINDOW - dec_seq

    @pl.when(pl.program_id(0) == 0)
    def _():
        _fill_bias(bias_new_ref, bias_old_ref, dec_seq)

    keep_lane = lax.broadcasted_iota(jnp.int32, (HEAD_DIM, WINDOW), 1) < kept
    pad = jnp.zeros((kept, LANES), _F32)
    ones_col = lax.broadcasted_iota(jnp.int32, (2 * BF16_SUBLANES, 2 * WINDOW), 1) // WINDOW
    ones_row = lax.broadcasted_iota(jnp.int32, (2 * BF16_SUBLANES, 2 * WINDOW), 0) // BF16_SUBLANES
    sum_rows = jnp.where(ones_col == ones_row, 1.0, 0.0).astype(_BF16)
    lane2 = lax.broadcasted_iota(jnp.int32, (2 * dec_seq, 2 * WINDOW), 1)
    lane1 = lax.broadcasted_iota(jnp.int32, (2 * dec_seq, LANES), 1)
    row2 = lax.broadcasted_iota(jnp.int32, (2 * dec_seq, 1), 0)

    def new_cols_t(rows_f32):
        return jnp.concatenate([pad, rows_f32], axis=0).T

    def one_batch(bb, carry):
        rows = pl.ds(pl.multiple_of(bb * dec_seq, dec_seq), dec_seq)
        k_new = k_ref[rows, :]
        v_new = v_ref[rows, :]
        kt_new = [new_cols_t(k_new[:, g * LANES:(g + 1) * LANES]) for g in range(KV_DIM // LANES)]
        vt_new = [new_cols_t(v_new[:, g * LANES:(g + 1) * LANES]) for g in range(KV_DIM // LANES)]

        staged = []
        for kvh in range(N_KV_HEADS):
            cache = pl.ds(pl.multiple_of(bb * KV_DIM + kvh * HEAD_DIM, HEAD_DIM), HEAD_DIM)
            kt_old = ckt_ref[cache, :]
            vt_old = cvt_ref[cache, :]
            half = slice((kvh % 2) * HEAD_DIM, (kvh % 2 + 1) * HEAD_DIM)
            kt = jnp.where(keep_lane, pltpu.roll(kt_old, kept, axis=1), kt_new[kvh // 2][half, :])
            vt = jnp.where(keep_lane, pltpu.roll(vt_old, kept, axis=1), vt_new[kvh // 2][half, :])
            nkt_ref[cache, :] = kt
            nvt_ref[cache, :] = vt
            q2 = jnp.concatenate(
                [q_ref[rows, (2 * kvh + g) * LANES:(2 * kvh + g + 1) * LANES] for g in range(2)],
                axis=0)
            s_new = _dot(q2, _block_diag2(kt.astype(_BF16))) + bias_new_ref[kvh]
            s_old = _dot(q2, _block_diag2(kt_old.astype(_BF16))) + bias_old_ref[kvh]
            v2_new = jnp.concatenate([_block_diag2(vt.astype(_BF16)), sum_rows], axis=0)
            v2_old = jnp.concatenate([_block_diag2(vt_old.astype(_BF16)), sum_rows], axis=0)
            staged.append((s_new, s_old, v2_new, v2_old))

        soft = []
        for kvh, (s_new, s_old, v2_new, v2_old) in enumerate(staged):
            s_max = jnp.maximum(s_new, s_old)
            head0 = kvh * GQA_GROUP
            ms, sink_es = [], []
            for p in range(2):
                sink = jnp.where(row2 < dec_seq, sinks_ref[0, head0 + p], sinks_ref[0, head0 + 2 + p])
                m_p = jnp.maximum(
                    jnp.max(s_max[:, p * WINDOW:(p + 1) * WINDOW], axis=-1, keepdims=True), sink)
                ms.append(m_p)
                sink_es.append(jnp.exp(sink - m_p))
            m = jnp.where(lane2 < WINDOW, ms[0], ms[1])
            soft.append((jnp.exp(s_new - m).astype(_BF16), jnp.exp(s_old - m).astype(_BF16),
                         sink_es, v2_new, v2_old))

        outs = [(lax.dot_general(e_new, v2_new, _CONTRACT_LAST, preferred_element_type=_F32)
                 + lax.dot_general(e_old, v2_old, _CONTRACT_LAST, preferred_element_type=_F32),
                 sink_es) for e_new, e_old, sink_es, v2_new, v2_old in soft]
        for kvh, (r, sink_es) in enumerate(outs):
            dens = [r[:, LANES + p * BF16_SUBLANES:LANES + p * BF16_SUBLANES + 1] + sink_es[p]
                    for p in range(2)]
            inv = jnp.where(lane1 < HEAD_DIM, 1.0 / dens[0], 1.0 / dens[1])
            o = (r[:, :LANES] * inv).astype(_BF16)
            for g in range(2):
                o_scr[rows, (2 * kvh + g) * LANES:(2 * kvh + g + 1) * LANES] = (
                    o[g * dec_seq:(g + 1) * dec_seq, :])
        return carry

    lax.fori_loop(0, TOKEN_BLOCK // dec_seq, one_batch, 0, unroll=2)
    out_ref[...] = x_ref[...] + _dot(o_scr[...], wo_ref[...])


def _attn_sample(sinks, x, q, k, v, cache_kt, cache_vt, w_o, dec_seq):
    n_tok = x.shape[0]
    tb = TOKEN_BLOCK
    batches_per_block = tb // dec_seq
    row = lambda width: pl.BlockSpec((tb, width), lambda i: (i, 0))
    cache = pl.BlockSpec((batches_per_block * KV_DIM, WINDOW), lambda i: (i, 0))
    return pl.pallas_call(
        functools.partial(_attn_sample_kernel, dec_seq=dec_seq),
        grid=(n_tok // tb,),
        in_specs=[pl.BlockSpec(memory_space=pltpu.SMEM),
                  row(D_MODEL), row(D_MODEL), row(KV_DIM), cache, row(KV_DIM), cache,
                  _resident((D_MODEL, D_MODEL))],
        out_specs=[row(D_MODEL), cache, cache],
        out_shape=[jax.ShapeDtypeStruct(x.shape, _F32),
                   jax.ShapeDtypeStruct(cache_kt.shape, _F32),
                   jax.ShapeDtypeStruct(cache_vt.shape, _F32)],
        scratch_shapes=[
            pltpu.VMEM((N_KV_HEADS, 2 * dec_seq, 2 * WINDOW), _F32),
            pltpu.VMEM((N_KV_HEADS, 2 * dec_seq, 2 * WINDOW), _F32),
            pltpu.VMEM((tb, D_MODEL), _BF16),
        ],
        compiler_params=_params(1),
        name="attn_sample",
    )(sinks, x, q, k, cache_kt, v, cache_vt, w_o)


def _ffn_ple_kernel(h_ref, p_ref, gf_ref, w1_ref, w2_ref, gp_ref, wg_ref, wp_ref, out_ref):
    n_groups = h_ref.shape[0] // TOKEN_BLOCK
    groups = [slice(r * TOKEN_BLOCK, (r + 1) * TOKEN_BLOCK) for r in range(n_groups)]
    runs = [_ffn_ple_steps(h_ref[rows, :], p_ref[rows, :].astype(_BF16), gf_ref, w1_ref, w2_ref,
                           gp_ref, wg_ref, wp_ref) for rows in groups]
    results = {}
    while len(results) < n_groups:
        for rows, run in zip(groups, runs):
            try:
                next(run)
            except StopIteration as stop:
                results[rows.start] = stop.value
    for rows in groups:
        out_ref[rows, :] = results[rows.start]


def _ffn_ple(h, p, g_ffn, w1, w2, g_ple, w_gate, w_proj, layer):
    n_tok = h.shape[0]
    tb = FFN_TOKEN_BLOCK
    row = lambda width: pl.BlockSpec((tb, width), lambda i: (i, 0))
    return pl.pallas_call(
        _ffn_ple_kernel,
        grid=(n_tok // tb,),
        in_specs=[row(D_MODEL), pl.BlockSpec((None, tb, PLE_DIM), lambda i: (layer, i, 0)),
                  _resident((1, D_MODEL), layer),
                  _resident((D_MODEL, D_FF)), _resident((D_FF, D_MODEL)),
                  _resident((1, D_MODEL), layer), _resident((D_MODEL, D_MODEL)),
                  _resident((PLE_DIM, D_MODEL))],
        out_specs=row(D_MODEL),
        out_shape=jax.ShapeDtypeStruct(h.shape, _F32),
        compiler_params=_params(1),
        name="ffn_ple",
    )(h, p, g_ffn, w1, w2, g_ple, w_gate, w_proj)


def _gelu_tanh(x):
    c = math.sqrt(2.0 / math.pi)
    return x * (0.5 * (1.0 + jnp.tanh(c * (x + 0.044715 * (x * x * x)))))


def _gmlp_kernel(*refs, emit_v, n_casts):
    h_ref, g_ref, wuv_ref, vg_ref, ws_ref, bs_ref, wout_ref = refs[:7]
    n_out = 2 if emit_v else 1
    out_ref = refs[7 + n_casts]
    vout_ref = refs[8 + n_casts] if emit_v else None
    v_scr, vb_scr = refs[7 + 2 * n_casts + n_out:]
    _cast_slabs(refs[7:7 + n_casts], refs[7 + n_casts + n_out:7 + 2 * n_casts + n_out])
    tb = h_ref.shape[0]
    group_rows = tb // GMLP_ROW_GROUPS
    row_groups = [slice(r * group_rows, (r + 1) * group_rows) for r in range(GMLP_ROW_GROUPS)]
    hs = [h_ref[rows, :] for rows in row_groups]
    ns = [_rms(h, g_ref[...]).astype(_BF16) for h in hs]

    ssq = [jnp.zeros((group_rows, 1), _F32) for _ in row_groups]
    for c in range(GMLP_HALF // FF_CHUNK):
        cols = slice(c * FF_CHUNK, (c + 1) * FF_CHUNK)
        w_cols = slice(GMLP_HALF + c * FF_CHUNK, GMLP_HALF + (c + 1) * FF_CHUNK)
        vcs = [_gelu_tanh(_dot(n, wuv_ref[:, w_cols])) for n in ns]
        for r, (rows, vc) in enumerate(zip(row_groups, vcs)):
            v_scr[rows, cols] = vc
            ssq[r] = ssq[r] + jnp.sum(vc * vc, axis=-1, keepdims=True)
    invs = [lax.rsqrt(s * (1.0 / GMLP_HALF) + EPS) for s in ssq]
    for c in range(GMLP_HALF // FF_CHUNK):
        cols = slice(c * FF_CHUNK, (c + 1) * FF_CHUNK)
        for rows, inv in zip(row_groups, invs):
            vn = (v_scr[rows, cols] * inv) * vg_ref[:, cols]
            if emit_v:
                vout_ref[rows, cols] = vn
            vb_scr[rows, cols] = vn.astype(_BF16)

    groups_per_step = U_CHUNK // GMLP_GROUP_DIM
    for c in range(GMLP_HALF // U_CHUNK):
        us = [_gelu_tanh(_dot(n, wuv_ref[:, c * U_CHUNK:(c + 1) * U_CHUNK])) for n in ns]
        gateds = []
        for r, u in enumerate(us):
            gated_rows = []
            for chunk in range(group_rows // GMLP_CHUNK):
                local = slice(chunk * GMLP_CHUNK, (chunk + 1) * GMLP_CHUNK)
                rows = slice(r * group_rows + chunk * GMLP_CHUNK,
                             r * group_rows + (chunk + 1) * GMLP_CHUNK)
                parts = []
                for gg in range(groups_per_step):
                    grp = c * groups_per_step + gg
                    cols = slice(grp * GMLP_GROUP_DIM, (grp + 1) * GMLP_GROUP_DIM)
                    s = _dot(ws_ref[grp], vb_scr[rows, cols]) + bs_ref[:, grp:grp + 1]
                    parts.append(u[local, gg * GMLP_GROUP_DIM:(gg + 1) * GMLP_GROUP_DIM] * s)
                gated_rows.append(jnp.concatenate(parts, axis=1))
            gateds.append(jnp.concatenate(gated_rows, axis=0).astype(_BF16))
        hs = [h + _dot(gated, wout_ref[c * U_CHUNK:(c + 1) * U_CHUNK, :])
              for h, gated in zip(hs, gateds)]
    for rows, h in zip(row_groups, hs):
        out_ref[rows, :] = h


def _gmlp(h, g, w_uv, v_gain, ws, bs_t, w_out, emit_v, cast_weights=()):
    n_tok = h.shape[0]
    tb = TOKEN_BLOCK
    casts = _WeightCasts(cast_weights, n_tok // tb)
    row = lambda width: pl.BlockSpec((tb, width), lambda i: (i, 0))
    out_specs = [row(D_MODEL)]
    out_shape = [jax.ShapeDtypeStruct(h.shape, _F32)]
    if emit_v:
        out_specs.append(row(GMLP_HALF))
        out_shape.append(jax.ShapeDtypeStruct((n_tok, GMLP_HALF), _F32))
    return pl.pallas_call(
        functools.partial(_gmlp_kernel, emit_v=emit_v, n_casts=len(casts)),
        grid=(n_tok // tb,),
        in_specs=[row(D_MODEL), _resident((1, D_MODEL), layer=1),
                  _resident((D_MODEL, 2 * GMLP_HALF)),
                  _resident((1, GMLP_HALF)),
                  _resident((GMLP_GROUPS, GMLP_CHUNK, GMLP_CHUNK)),
                  _resident((GMLP_CHUNK, GMLP_GROUPS)),
                  _resident((GMLP_HALF, D_MODEL))] + casts.in_specs,
        out_specs=out_specs + casts.out_specs,
        out_shape=out_shape + casts.out_shape,
        scratch_shapes=[pltpu.VMEM((tb, GMLP_HALF), _F32), pltpu.VMEM((tb, GMLP_HALF), _BF16)],
        compiler_params=_params(1),
        name="gmlp_v" if emit_v else "gmlp",
    )(h, g, w_uv, v_gain, ws, bs_t, w_out, *casts.arrays)


def _gmlp_spatial_weights(w_s, b_s, length):
    tril = jnp.tril(jnp.ones((length, length), w_s.dtype))
    ws = w_s[:, :length, :length] * tril
    reps = GMLP_CHUNK // length
    if reps > 1:
        eye = jnp.eye(reps, dtype=w_s.dtype)
        ws = jnp.einsum("ab,gij->gaibj", eye, ws).reshape(GMLP_GROUPS, GMLP_CHUNK, GMLP_CHUNK)
    bs_t = jnp.tile(b_s[:, :length].T, (reps, 1))
    return ws.astype(_BF16), bs_t


def _to_cache_t(cache):
    _, b, rows, kvh, hd = cache.shape
    return jnp.transpose(cache, (0, 1, 3, 4, 2)).reshape(b * kvh * hd, rows)


def _from_cache_t(cache_t, batch):
    t = cache_t.reshape(1, batch, N_KV_HEADS, HEAD_DIM, cache_t.shape[1])
    return jnp.transpose(t, (0, 1, 4, 2, 3))


def kernel(x_prompt, x_sample, p_prompt, p_sample, cache_k, cache_v, g_mix, g_ffn, g_ple,
           attn_w_qkv, attn_q_norm, attn_k_norm, attn_sinks, attn_w_o, gmlp_w_uv, gmlp_v_norm,
           gmlp_w_s, gmlp_b_s, gmlp_w_out, ffn_w1, ffn_w2, ple_w_proj, ple_w_gate):
    batch, seq, _ = x_prompt.shape
    dec_batch, dec_seq, _ = x_sample.shape
    depth = g_mix.shape[0]
    cache_rows = cache_k.shape[2]
    assert depth == 2 and attn_w_qkv.shape[0] == 1 and gmlp_w_uv.shape[0] == 1
    assert seq % FFN_TOKEN_BLOCK == 0 and (dec_batch * dec_seq) % FFN_TOKEN_BLOCK == 0
    assert TOKEN_BLOCK % (2 * dec_seq) == 0 and GMLP_CHUNK % dec_seq == 0 and dec_seq % 16 == 0
    assert cache_rows == WINDOW and dec_seq <= WINDOW

    scale = HEAD_DIM ** -0.5
    g_mix = g_mix.reshape(depth, 1, D_MODEL)
    g_ffn = g_ffn.reshape(depth, 1, D_MODEL)
    g_ple = g_ple.reshape(depth, 1, D_MODEL)
    w_qkv = attn_w_qkv.astype(_BF16)
    qk_gain = jnp.concatenate([jnp.tile(attn_q_norm[0] * scale, N_HEADS),
                               jnp.tile(attn_k_norm[0], N_KV_HEADS)]).reshape(1, QK_DIM)
    sinks = attn_sinks[0].reshape(1, N_HEADS)
    v_gain = gmlp_v_norm[0].reshape(1, GMLP_HALF)
    ws_p, bs_p = _gmlp_spatial_weights(gmlp_w_s[0], gmlp_b_s[0], min(seq, GMLP_CHUNK))
    ws_s, bs_s = _gmlp_spatial_weights(gmlp_w_s[0], gmlp_b_s[0], min(dec_seq, GMLP_CHUNK))

    n_prompt = batch * seq
    n_sample = dec_batch * dec_seq
    x_p = x_prompt.reshape(n_prompt, D_MODEL)
    x_s = x_sample.reshape(n_sample, D_MODEL)
    p_p = p_prompt.reshape(depth, n_prompt, PLE_DIM)
    p_s = p_sample.reshape(depth, n_sample, PLE_DIM)

    layer_weights = lambda layer: [(ffn_w1, layer), (ffn_w2, layer), (ple_w_gate, layer),
                                   (ple_w_proj, layer)]
    q, k, v, w_o, *mlp0 = _qkv_proj(x_p, g_mix, w_qkv, qk_gain,
                                    cast_weights=[(attn_w_o, 0)] + layer_weights(0))
    h_p, nkt_p, nvt_p, w_uv, w_out = _attn_ffn_prompt(
        sinks, x_p, q, k, v, w_o, p_p, g_ffn, mlp0[0], mlp0[1], g_ple, mlp0[2], mlp0[3], seq,
        cast_weights=[(gmlp_w_uv, 0), (gmlp_w_out, 0)])
    h_p, *mlp1 = _gmlp(h_p, g_mix, w_uv, v_gain, ws_p, bs_p, w_out, False,
                       cast_weights=layer_weights(1))
    y_p = _ffn_ple(h_p, p_p, g_ffn, mlp1[0], mlp1[1], g_ple, mlp1[2], mlp1[3], 1)

    q, k, v = _qkv_proj(x_s, g_mix, w_qkv, qk_gain)
    h_s, nkt_s, nvt_s = _attn_sample(sinks, x_s, q, k, v, _to_cache_t(cache_k),
                                     _to_cache_t(cache_v), w_o, dec_seq)
    h_s = _ffn_ple(h_s, p_s, g_ffn, mlp0[0], mlp0[1], g_ple, mlp0[2], mlp0[3], 0)
    h_s, vrows = _gmlp(h_s, g_mix, w_uv, v_gain, ws_s, bs_s, w_out, True)
    y_s = _ffn_ple(h_s, p_s, g_ffn, mlp1[0], mlp1[1], g_ple, mlp1[2], mlp1[3], 1)
    return (y_p.reshape(batch, seq, D_MODEL), y_s.reshape(dec_batch, dec_seq, D_MODEL),
            _from_cache_t(nkt_p, batch), _from_cache_t(nvt_p, batch),
            _from_cache_t(nkt_s, dec_batch), _from_cache_t(nvt_s, dec_batch),
            vrows.reshape(1, dec_batch, dec_seq, GMLP_HALF))
```

```python
import functools
import math

import jax
import jax.numpy as jnp
from jax import lax
from jax.experimental import pallas as pl
from jax.experimental.pallas import tpu as pltpu

D_MODEL = 1024
HEAD_DIM = 64
N_HEADS = 16
N_KV_HEADS = 4
GQA_GROUP = N_HEADS // N_KV_HEADS
KV_DIM = N_KV_HEADS * HEAD_DIM
QK_DIM = D_MODEL + KV_DIM
QKV_DIM = QK_DIM + KV_DIM
CHUNK = 64
WINDOW = 128
GMLP_CHUNK = 128
GMLP_HALF = 3 * D_MODEL
GMLP_GROUPS = 8
GMLP_GROUP_DIM = GMLP_HALF // GMLP_GROUPS
D_FF = 4 * D_MODEL
PLE_DIM = 256
EPS = 1e-6
NEG_INF = -1e30

LANES = 128
MXU_DIM_V7X = 256
BF16_SUBLANES = 16
VMEM_LIMIT_BYTES_V7X = 56 * 1024 * 1024

TOKEN_BLOCK = 512
FFN_TOKEN_BLOCK = 2 * TOKEN_BLOCK
GMLP_ROW_GROUPS = 1
ATTN_UNIT = 2 * CHUNK
ATTN_KEYS = 2 * WINDOW
PV_ROWS = HEAD_DIM + BF16_SUBLANES
FF_CHUNK = 1024
U_CHUNK = 2 * GMLP_GROUP_DIM

_BF16 = jnp.bfloat16
_F32 = jnp.float32
_CONTRACT_LAST = (((1,), (1,)), ((), ()))


def _resident(shape, layer=None):
    if layer is None:
        return pl.BlockSpec(shape, lambda *_: (0,) * len(shape), pipeline_mode=pl.Buffered(1))
    return pl.BlockSpec((None,) + tuple(shape), lambda *_: (layer,) + (0,) * len(shape),
                        pipeline_mode=pl.Buffered(1))


def _params(n_axes):
    return pltpu.CompilerParams(
        dimension_semantics=("arbitrary",) * n_axes,
        vmem_limit_bytes=VMEM_LIMIT_BYTES_V7X,
    )


def _rms(x, g):
    ms = jnp.mean(x * x, axis=-1, keepdims=True)
    return (x * lax.rsqrt(ms + EPS)) * g


def _dot(a, b):
    return jnp.dot(a, b, preferred_element_type=_F32)


class _WeightCasts:
    def __init__(self, weights, n_slabs):
        self.arrays = [arr for arr, _ in weights]
        self.in_specs, self.out_specs, self.out_shape = [], [], []
        for arr, layer in weights:
            rows, cols = arr.shape[-2:]
            slab = rows // n_slabs
            if rows % n_slabs or slab % BF16_SUBLANES:
                slab, n_slabs_w = rows, 1
            else:
                n_slabs_w = n_slabs
            row_block = lambda s, last=n_slabs_w - 1: jnp.minimum(s, last)
            self.in_specs.append(pl.BlockSpec(
                (None, slab, cols), lambda s, f=row_block, l=layer: (l, f(s), 0)))
            self.out_specs.append(pl.BlockSpec((slab, cols), lambda s, f=row_block: (f(s), 0)))
            self.out_shape.append(jax.ShapeDtypeStruct((rows, cols), _BF16))

    def __len__(self):
        return len(self.arrays)


def _cast_slabs(in_refs, out_refs):
    for src, dst in zip(in_refs, out_refs):
        dst[...] = src[...].astype(_BF16)


def _qkv_kernel(*refs, n_casts):
    x_ref, g_ref, w_ref, qkg_ref = refs[:4]
    q_ref, k_ref, v_ref = refs[4 + n_casts:7 + n_casts]
    _cast_slabs(refs[4:4 + n_casts], refs[7 + n_casts:])
    n_groups = x_ref.shape[0] // TOKEN_BLOCK
    groups = [slice(r * TOKEN_BLOCK, (r + 1) * TOKEN_BLOCK) for r in range(n_groups)]
    qkvs = [_dot(_rms(x_ref[rows, :], g_ref[...]).astype(_BF16), w_ref[...])
            for rows in groups]
    r = lax.broadcasted_iota(jnp.int32, (MXU_DIM_V7X, MXU_DIM_V7X), 0) // HEAD_DIM
    c = lax.broadcasted_iota(jnp.int32, (MXU_DIM_V7X, MXU_DIM_V7X), 1) // HEAD_DIM
    seg = jnp.where(r == c, 1.0, 0.0).astype(_BF16)
    n_q = D_MODEL // MXU_DIM_V7X
    for j in range(QK_DIM // MXU_DIM_V7X):
        cols = slice(j * MXU_DIM_V7X, (j + 1) * MXU_DIM_V7X)
        for rows, qkv in zip(groups, qkvs):
            t = qkv[:, cols]
            ss = _dot((t * t).astype(_BF16), seg)
            inv = lax.rsqrt(ss * (1.0 / HEAD_DIM) + EPS)
            normed = (t * inv) * qkg_ref[:, cols]
            if j < n_q:
                q_ref[rows, cols] = normed.astype(_BF16)
            else:
                k_ref[rows, :] = normed
    for rows, qkv in zip(groups, qkvs):
        v_ref[rows, :] = qkv[:, QK_DIM:]


def _qkv_proj(x, g, w_qkv, qk_gain, cast_weights=()):
    n_tok = x.shape[0]
    tb = FFN_TOKEN_BLOCK
    casts = _WeightCasts(cast_weights, n_tok // tb)
    row = lambda width: pl.BlockSpec((tb, width), lambda i: (i, 0))
    return pl.pallas_call(
        functools.partial(_qkv_kernel, n_casts=len(casts)),
        grid=(n_tok // tb,),
        in_specs=[row(D_MODEL), _resident((1, D_MODEL), layer=0),
                  _resident((D_MODEL, QKV_DIM), layer=0), _resident((1, QK_DIM))] + casts.in_specs,
        out_specs=[row(D_MODEL), row(KV_DIM), row(KV_DIM)] + casts.out_specs,
        out_shape=[jax.ShapeDtypeStruct((n_tok, D_MODEL), _BF16),
                   jax.ShapeDtypeStruct((n_tok, KV_DIM), _F32),
                   jax.ShapeDtypeStruct((n_tok, KV_DIM), _F32)] + casts.out_shape,
        compiler_params=_params(1),
        name="qkv_proj",
    )(x, g, w_qkv, qk_gain, *casts.arrays)


def _alibi_slope(head):
    return 2.0 ** (-8.0 * (head + 1) / N_HEADS)


def _fill_bias_t(bias_ref, variant, valid_fn):
    s = lax.broadcasted_iota(jnp.int32, (ATTN_KEYS, ATTN_UNIT), 0)
    t = lax.broadcasted_iota(jnp.int32, (ATTN_KEYS, ATTN_UNIT), 1)
    dist = jnp.abs(t + WINDOW - s).astype(_F32)
    valid = valid_fn(t, s)
    for kvh in range(N_KV_HEADS):
        for p in range(2):
            for g in range(2):
                slope = _alibi_slope(kvh * GQA_GROUP + 2 * g + p)
                bias_ref[variant, kvh, p * ATTN_KEYS:(p + 1) * ATTN_KEYS,
                         g * ATTN_UNIT:(g + 1) * ATTN_UNIT] = jnp.where(valid, -slope * dist, NEG_INF)


def _ffn_ple_steps(h, p_bf16, gf_ref, w1_ref, w2_ref, gp_ref, wg_ref, wp_ref):
    n = _rms(h, gf_ref[...]).astype(_BF16)
    for c in range(D_FF // FF_CHUNK):
        cols = slice(c * FF_CHUNK, (c + 1) * FF_CHUNK)
        a = jnp.maximum(_dot(n, w1_ref[:, cols]), 0.0)
        yield
        h = h + _dot((a * a).astype(_BF16), w2_ref[cols, :])
        yield
    z = _dot(_rms(h, gp_ref[...]).astype(_BF16), wg_ref[...])
    yield
    e = _dot(p_bf16, wp_ref[...])
    yield
    gate = 1.0 / (1.0 + jnp.exp(-z))
    return h + gate * e


def _finish(steps):
    while True:
        try:
            next(steps)
        except StopIteration as stop:
            return stop.value


def _attn_ffn_prompt_kernel(*refs, blocks_per_seq, n_casts):
    (sinks_ref, x_ref, q_ref, k_ref, kh_ref, v_ref, vh_ref, wo_ref, p_ref,
     gf_ref, w1_ref, w2_ref, gp_ref, wg_ref, wp_ref) = refs[:15]
    out_ref, nkt_ref, nvt_ref = refs[15 + n_casts:18 + n_casts]
    bias_ref, klo_scr, khi_scr, vt_scr, ot_scr, h_scr = refs[18 + 2 * n_casts:]
    _cast_slabs(refs[15:15 + n_casts], refs[18 + n_casts:18 + 2 * n_casts])
    step = pl.program_id(0)
    tb = TOKEN_BLOCK
    block = jnp.minimum(step, pl.num_programs(0) - 2)

    @pl.when(step == 0)
    def _():
        in_window = lambda t, s: ((t < CHUNK) & (s < WINDOW + CHUNK)) | ((t >= CHUNK) & (s >= CHUNK))
        _fill_bias_t(bias_ref, 0, in_window)
        _fill_bias_t(bias_ref, 1, lambda t, s: in_window(t, s) & (s >= WINDOW))
        vt_scr[:, HEAD_DIM:, :] = jnp.ones((N_KV_HEADS, PV_ROWS - HEAD_DIM, WINDOW + tb), _BF16)
        h_scr[...] = jnp.zeros_like(h_scr)

    mlp = _ffn_ple_steps(h_scr[...], p_ref[...].astype(_BF16), gf_ref, w1_ref, w2_ref, gp_ref,
                         wg_ref, wp_ref)
    next(mlp)

    kd = jnp.concatenate([kh_ref[...], k_ref[...]], axis=0)
    low_half = lax.broadcasted_iota(jnp.int32, (WINDOW + tb, LANES), 1) < HEAD_DIM
    for pair in range(N_KV_HEADS // 2):
        kk = kd[:, pair * LANES:(pair + 1) * LANES]
        swapped = pltpu.roll(kk, HEAD_DIM, axis=1)
        klo_scr[2 * pair] = jnp.where(low_half, kk, 0.0).astype(_BF16)
        khi_scr[2 * pair] = jnp.where(low_half, 0.0, swapped).astype(_BF16)
        klo_scr[2 * pair + 1] = jnp.where(low_half, swapped, 0.0).astype(_BF16)
        khi_scr[2 * pair + 1] = jnp.where(low_half, 0.0, kk).astype(_BF16)
    vt = jnp.concatenate([vh_ref[...], v_ref[...]], axis=0).T
    for kvh in range(N_KV_HEADS):
        vt_scr[kvh, :HEAD_DIM, :] = vt[kvh * HEAD_DIM:(kvh + 1) * HEAD_DIM, :].astype(_BF16)

    nkt_ref[...] = k_ref[tb - WINDOW:, :].T
    nvt_ref[...] = vt[:, tb:]

    first_in_seq = jnp.where(block % blocks_per_seq == 0, 1, 0)
    lane2 = lax.broadcasted_iota(jnp.int32, (1, 2 * ATTN_UNIT), 1)
    n_units = tb // ATTN_UNIT

    def scores(u):
        rows = slice(u * ATTN_UNIT, (u + 1) * ATTN_UNIT)
        keys = slice(u * ATTN_UNIT, u * ATTN_UNIT + ATTN_KEYS)
        variant = first_in_seq if u == 0 else 0
        out = []
        for kvh in range(N_KV_HEADS):
            k2 = jnp.concatenate([klo_scr[kvh, keys, :], khi_scr[kvh, keys, :]], axis=0)
            q2 = jnp.concatenate(
                [q_ref[rows, (2 * kvh + g) * LANES:(2 * kvh + g + 1) * LANES] for g in range(2)],
                axis=0)
            s_t = lax.dot_general(k2, q2, _CONTRACT_LAST, preferred_element_type=_F32)
            out.append(s_t + bias_ref[variant, kvh])
        return out

    s_next = scores(0)
    for u in range(n_units):
        rows = slice(u * ATTN_UNIT, (u + 1) * ATTN_UNIT)
        keys = slice(u * ATTN_UNIT, u * ATTN_UNIT + ATTN_KEYS)
        s_cur = s_next
        if u + 1 < n_units:
            s_next = scores(u + 1)
        next(mlp)
        chains = []
        for kvh in range(N_KV_HEADS):
            for p in range(2):
                head0 = kvh * GQA_GROUP + p
                sink = jnp.where(lane2 < ATTN_UNIT, sinks_ref[0, head0], sinks_ref[0, head0 + 2])
                sp = s_cur[kvh][p * ATTN_KEYS:(p + 1) * ATTN_KEYS, :]
                m = jnp.maximum(jnp.max(sp, axis=0, keepdims=True), sink)
                chains.append((head0, kvh, jnp.exp(sp - m).astype(_BF16), jnp.exp(sink - m)))
        results = [(head0, _dot(vt_scr[kvh, :, keys], e), sink_e)
                   for head0, kvh, e, sink_e in chains]
        next(mlp)
        for head0, r, sink_e in results:
            den = r[HEAD_DIM:HEAD_DIM + 1, :] + sink_e
            o = r[:HEAD_DIM, :] * (1.0 / den)
            for g in range(2):
                head = head0 + 2 * g
                ot_scr[head * HEAD_DIM:(head + 1) * HEAD_DIM, rows] = (
                    o[:, g * ATTN_UNIT:(g + 1) * ATTN_UNIT].astype(_BF16))

    out_ref[...] = _finish(mlp)
    h_scr[...] = x_ref[...] + _dot(ot_scr[...].T, wo_ref[...])


def _attn_ffn_prompt(sinks, x, q, k, v, w_o, p, g_ffn, w1, w2, g_ple, w_gate, w_proj, seq,
                     cast_weights=()):
    tb = TOKEN_BLOCK
    layer = 0
    n_blocks = x.shape[0] // tb
    casts = _WeightCasts(cast_weights, n_blocks)
    nt = seq // tb
    halo_per_block = tb // WINDOW
    attn_block = lambda s: jnp.minimum(s, n_blocks - 1)
    mlp_block = lambda s: jnp.maximum(s - 1, 0)
    row = lambda width: pl.BlockSpec((tb, width), lambda s: (attn_block(s), 0))
    halo = pl.BlockSpec(
        (WINDOW, KV_DIM), lambda s: (jnp.maximum(attn_block(s) * halo_per_block - 1, 0), 0))
    per_seq = pl.BlockSpec((KV_DIM, WINDOW), lambda s: (attn_block(s) // nt, 0))
    return pl.pallas_call(
        functools.partial(_attn_ffn_prompt_kernel, blocks_per_seq=nt, n_casts=len(casts)),
        grid=(n_blocks + 1,),
        in_specs=[pl.BlockSpec(memory_space=pltpu.SMEM),
                  row(D_MODEL), row(D_MODEL), row(KV_DIM), halo, row(KV_DIM), halo,
                  _resident((D_MODEL, D_MODEL)),
                  pl.BlockSpec((None, tb, PLE_DIM), lambda s: (layer, mlp_block(s), 0)),
                  _resident((1, D_MODEL), layer),
                  _resident((D_MODEL, D_FF)), _resident((D_FF, D_MODEL)),
                  _resident((1, D_MODEL), layer), _resident((D_MODEL, D_MODEL)),
                  _resident((PLE_DIM, D_MODEL))] + casts.in_specs,
        out_specs=[pl.BlockSpec((tb, D_MODEL), lambda s: (mlp_block(s), 0)), per_seq,
                   per_seq] + casts.out_specs,
        out_shape=[jax.ShapeDtypeStruct(x.shape, _F32),
                   jax.ShapeDtypeStruct((x.shape[0] // seq * KV_DIM, WINDOW), _F32),
                   jax.ShapeDtypeStruct((x.shape[0] // seq * KV_DIM, WINDOW), _F32)]
        + casts.out_shape,
        scratch_shapes=[
            pltpu.VMEM((2, N_KV_HEADS, 2 * ATTN_KEYS, 2 * ATTN_UNIT), _F32),
            pltpu.VMEM((N_KV_HEADS, WINDOW + tb, LANES), _BF16),
            pltpu.VMEM((N_KV_HEADS, WINDOW + tb, LANES), _BF16),
            pltpu.VMEM((N_KV_HEADS, PV_ROWS, WINDOW + tb), _BF16),
            pltpu.VMEM((D_MODEL, tb), _BF16),
            pltpu.VMEM((tb, D_MODEL), _F32),
        ],
        compiler_params=_params(1),
        name="attn_ffn_prompt",
    )(sinks, x, q, k, k, v, v, w_o, p, g_ffn, w1, w2, g_ple, w_gate, w_proj, *casts.arrays)


def _fill_bias(bias_new_ref, bias_old_ref, q_rows):
    t = lax.broadcasted_iota(jnp.int32, (q_rows, WINDOW), 0)
    s = lax.broadcasted_iota(jnp.int32, (q_rows, WINDOW), 1)
    dist_new = jnp.abs(t + WINDOW - (s + q_rows)).astype(_F32)
    dist_old = jnp.abs(t + WINDOW - s).astype(_F32)
    for kvh in range(N_KV_HEADS):
        for g in range(2):
            for p in range(2):
                slope = _alibi_slope(kvh * GQA_GROUP + 2 * g + p)
                rows = slice(g * q_rows, (g + 1) * q_rows)
                cols = slice(p * WINDOW, (p + 1) * WINDOW)
                bias_new_ref[kvh, rows, cols] = -slope * dist_new
                bias_old_ref[kvh, rows, cols] = jnp.where(s < q_rows, -slope * dist_old, NEG_INF)


def _block_diag2(a):
    z = jnp.zeros_like(a)
    return jnp.concatenate([jnp.concatenate([a, z], axis=1), jnp.concatenate([z, a], axis=1)],
                           axis=0)


def _attn_sample_kernel(sinks_ref, x_ref, q_ref, k_ref, ckt_ref, v_ref, cvt_ref, wo_ref,
                        out_ref, nkt_ref, nvt_ref, bias_new_ref, bias_old_ref, o_scr, *, dec_seq):
    kept = WINDOW - dec_seq

    @pl.when(pl.program_id(0) == 0)
    def _():
        _fill_bias(bias_new_ref, bias_old_ref, dec_seq)

    keep_lane = lax.broadcasted_iota(jnp.int32, (HEAD_DIM, WINDOW), 1) < kept
    pad = jnp.zeros((kept, LANES), _F32)
    ones_col = lax.broadcasted_iota(jnp.int32, (2 * BF16_SUBLANES, 2 * WINDOW), 1) // WINDOW
    ones_row = lax.broadcasted_iota(jnp.int32, (2 * BF16_SUBLANES, 2 * WINDOW), 0) // BF16_SUBLANES
    sum_rows = jnp.where(ones_col == ones_row, 1.0, 0.0).astype(_BF16)
    lane2 = lax.broadcasted_iota(jnp.int32, (2 * dec_seq, 2 * WINDOW), 1)
    lane1 = lax.broadcasted_iota(jnp.int32, (2 * dec_seq, LANES), 1)
    row2 = lax.broadcasted_iota(jnp.int32, (2 * dec_seq, 1), 0)

    def new_cols_t(rows_f32):
        return jnp.concatenate([pad, rows_f32], axis=0).T

    def one_batch(bb, carry):
        rows = pl.ds(pl.multiple_of(bb * dec_seq, dec_seq), dec_seq)
        k_new = k_ref[rows, :]
        v_new = v_ref[rows, :]
        kt_new = [new_cols_t(k_new[:, g * LANES:(g + 1) * LANES]) for g in range(KV_DIM // LANES)]
        vt_new = [new_cols_t(v_new[:, g * LANES:(g + 1) * LANES]) for g in range(KV_DIM // LANES)]

        staged = []
        for kvh in range(N_KV_HEADS):
            cache = pl.ds(pl.multiple_of(bb * KV_DIM + kvh * HEAD_DIM, HEAD_DIM), HEAD_DIM)
            kt_old = ckt_ref[cache, :]
            vt_old = cvt_ref[cache, :]
            half = slice((kvh % 2) * HEAD_DIM, (kvh % 2 + 1) * HEAD_DIM)
            kt = jnp.where(keep_lane, pltpu.roll(kt_old, kept, axis=1), kt_new[kvh // 2][half, :])
            vt = jnp.where(keep_lane, pltpu.roll(vt_old, kept, axis=1), vt_new[kvh // 2][half, :])
            nkt_ref[cache, :] = kt
            nvt_ref[cache, :] = vt
            q2 = jnp.concatenate(
                [q_ref[rows, (2 * kvh + g) * LANES:(2 * kvh + g + 1) * LANES] for g in range(2)],
                axis=0)
            s_new = _dot(q2, _block_diag2(kt.astype(_BF16))) + bias_new_ref[kvh]
            s_old = _dot(q2, _block_diag2(kt_old.astype(_BF16))) + bias_old_ref[kvh]
            v2_new = jnp.concatenate([_block_diag2(vt.astype(_BF16)), sum_rows], axis=0)
            v2_old = jnp.concatenate([_block_diag2(vt_old.astype(_BF16)), sum_rows], axis=0)
            staged.append((s_new, s_old, v2_new, v2_old))

        soft = []
        for kvh, (s_new, s_old, v2_new, v2_old) in enumerate(staged):
            s_max = jnp.maximum(s_new, s_old)
            head0 = kvh * GQA_GROUP
            ms, sink_es = [], []
            for p in range(2):
                sink = jnp.where(row2 < dec_seq, sinks_ref[0, head0 + p], sinks_ref[0, head0 + 2 + p])
                m_p = jnp.maximum(
                    jnp.max(s_max[:, p * WINDOW:(p + 1) * WINDOW], axis=-1, keepdims=True), sink)
                ms.append(m_p)
                sink_es.append(jnp.exp(sink - m_p))
            m = jnp.where(lane2 < WINDOW, ms[0], ms[1])
            soft.append((jnp.exp(s_new - m).astype(_BF16), jnp.exp(s_old - m).astype(_BF16),
                         sink_es, v2_new, v2_old))

        outs = [(lax.dot_general(e_new, v2_new, _CONTRACT_LAST, preferred_element_type=_F32)
                 + lax.dot_general(e_old, v2_old, _CONTRACT_LAST, preferred_element_type=_F32),
                 sink_es) for e_new, e_old, sink_es, v2_new, v2_old in soft]
        for kvh, (r, sink_es) in enumerate(outs):
            dens = [r[:, LANES + p * BF16_SUBLANES:LANES + p * BF16_SUBLANES + 1] + sink_es[p]
                    for p in range(2)]
            inv = jnp.where(lane1 < HEAD_DIM, 1.0 / dens[0], 1.0 / dens[1])
            o = (r[:, :LANES] * inv).astype(_BF16)
            for g in range(2):
                o_scr[rows, (2 * kvh + g) * LANES:(2 * kvh + g + 1) * LANES] = (
                    o[g * dec_seq:(g + 1) * dec_seq, :])
        return carry

    lax.fori_loop(0, TOKEN_BLOCK // dec_seq, one_batch, 0, unroll=2)
    out_ref[...] = x_ref[...] + _dot(o_scr[...], wo_ref[...])


def _attn_sample(sinks, x, q, k, v, cache_kt, cache_vt, w_o, dec_seq):
    n_tok = x.shape[0]
    tb = TOKEN_BLOCK
    batches_per_block = tb // dec_seq
    row = lambda width: pl.BlockSpec((tb, width), lambda i: (i, 0))
    cache = pl.BlockSpec((batches_per_block * KV_DIM, WINDOW), lambda i: (i, 0))
    return pl.pallas_call(
        functools.partial(_attn_sample_kernel, dec_seq=dec_seq),
        grid=(n_tok // tb,),
        in_specs=[pl.BlockSpec(memory_space=pltpu.SMEM),
                  row(D_MODEL), row(D_MODEL), row(KV_DIM), cache, row(KV_DIM), cache,
                  _resident((D_MODEL, D_MODEL))],
        out_specs=[row(D_MODEL), cache, cache],
        out_shape=[jax.ShapeDtypeStruct(x.shape, _F32),
                   jax.ShapeDtypeStruct(cache_kt.shape, _F32),
                   jax.ShapeDtypeStruct(cache_vt.shape, _F32)],
        scratch_shapes=[
            pltpu.VMEM((N_KV_HEADS, 2 * dec_seq, 2 * WINDOW), _F32),
            pltpu.VMEM((N_KV_HEADS, 2 * dec_seq, 2 * WINDOW), _F32),
            pltpu.VMEM((tb, D_MODEL), _BF16),
        ],
        compiler_params=_params(1),
        name="attn_sample",
    )(sinks, x, q, k, cache_kt, v, cache_vt, w_o)


def _ffn_ple_kernel(h_ref, p_ref, gf_ref, w1_ref, w2_ref, gp_ref, wg_ref, wp_ref, out_ref):
    n_groups = h_ref.shape[0] // TOKEN_BLOCK
    groups = [slice(r * TOKEN_BLOCK, (r + 1) * TOKEN_BLOCK) for r in range(n_groups)]
    runs = [_ffn_ple_steps(h_ref[rows, :], p_ref[rows, :].astype(_BF16), gf_ref, w1_ref, w2_ref,
                           gp_ref, wg_ref, wp_ref) for rows in groups]
    results = {}
    while len(results) < n_groups:
        for rows, run in zip(groups, runs):
            try:
                next(run)
            except StopIteration as stop:
                results[rows.start] = stop.value
    for rows in groups:
        out_ref[rows, :] = results[rows.start]


def _ffn_ple(h, p, g_ffn, w1, w2, g_ple, w_gate, w_proj, layer):
    n_tok = h.shape[0]
    tb = FFN_TOKEN_BLOCK
    row = lambda width: pl.BlockSpec((tb, width), lambda i: (i, 0))
    return pl.pallas_call(
        _ffn_ple_kernel,
        grid=(n_tok // tb,),
        in_specs=[row(D_MODEL), pl.BlockSpec((None, tb, PLE_DIM), lambda i: (layer, i, 0)),
                  _resident((1, D_MODEL), layer),
                  _resident((D_MODEL, D_FF)), _resident((D_FF, D_MODEL)),
                  _resident((1, D_MODEL), layer), _resident((D_MODEL, D_MODEL)),
                  _resident((PLE_DIM, D_MODEL))],
        out_specs=row(D_MODEL),
        out_shape=jax.ShapeDtypeStruct(h.shape, _F32),
        compiler_params=_params(1),
        name="ffn_ple",
    )(h, p, g_ffn, w1, w2, g_ple, w_gate, w_proj)


def _gelu_tanh(x):
    c = math.sqrt(2.0 / math.pi)
    return x * (0.5 * (1.0 + jnp.tanh(c * (x + 0.044715 * (x * x * x)))))


def _gmlp_kernel(*refs, emit_v, n_casts):
    h_ref, g_ref, wuv_ref, vg_ref, ws_ref, bs_ref, wout_ref = refs[:7]
    n_out = 2 if emit_v else 1
    out_ref = refs[7 + n_casts]
    vout_ref = refs[8 + n_casts] if emit_v else None
    v_scr, vb_scr = refs[7 + 2 * n_casts + n_out:]
    _cast_slabs(refs[7:7 + n_casts], refs[7 + n_casts + n_out:7 + 2 * n_casts + n_out])
    tb = h_ref.shape[0]
    group_rows = tb // GMLP_ROW_GROUPS
    row_groups = [slice(r * group_rows, (r + 1) * group_rows) for r in range(GMLP_ROW_GROUPS)]
    hs = [h_ref[rows, :] for rows in row_groups]
    ns = [_rms(h, g_ref[...]).astype(_BF16) for h in hs]

    ssq = [jnp.zeros((group_rows, 1), _F32) for _ in row_groups]
    for c in range(GMLP_HALF // FF_CHUNK):
        cols = slice(c * FF_CHUNK, (c + 1) * FF_CHUNK)
        w_cols = slice(GMLP_HALF + c * FF_CHUNK, GMLP_HALF + (c + 1) * FF_CHUNK)
        vcs = [_gelu_tanh(_dot(n, wuv_ref[:, w_cols])) for n in ns]
        for r, (rows, vc) in enumerate(zip(row_groups, vcs)):
            v_scr[rows, cols] = vc
            ssq[r] = ssq[r] + jnp.sum(vc * vc, axis=-1, keepdims=True)
    invs = [lax.rsqrt(s * (1.0 / GMLP_HALF) + EPS) for s in ssq]
    for c in range(GMLP_HALF // FF_CHUNK):
        cols = slice(c * FF_CHUNK, (c + 1) * FF_CHUNK)
        for rows, inv in zip(row_groups, invs):
            vn = (v_scr[rows, cols] * inv) * vg_ref[:, cols]
            if emit_v:
                vout_ref[rows, cols] = vn
            vb_scr[rows, cols] = vn.astype(_BF16)

    groups_per_step = U_CHUNK // GMLP_GROUP_DIM
    n_chunks = GMLP_HALF // U_CHUNK
    u_proj = lambda c: [_gelu_tanh(_dot(n, wuv_ref[:, c * U_CHUNK:(c + 1) * U_CHUNK])) for n in ns]
    us_next = u_proj(0)
    for c in range(n_chunks):
        us = us_next
        if c + 1 < n_chunks:
            us_next = u_proj(c + 1)
        gateds = []
        for r, u in enumerate(us):
            gated_rows = []
            for chunk in range(group_rows // GMLP_CHUNK):
                local = slice(chunk * GMLP_CHUNK, (chunk + 1) * GMLP_CHUNK)
                rows = slice(r * group_rows + chunk * GMLP_CHUNK,
                             r * group_rows + (chunk + 1) * GMLP_CHUNK)
                parts = []
                for gg in range(groups_per_step):
                    grp = c * groups_per_step + gg
                    cols = slice(grp * GMLP_GROUP_DIM, (grp + 1) * GMLP_GROUP_DIM)
                    s = _dot(ws_ref[grp], vb_scr[rows, cols]) + bs_ref[:, grp:grp + 1]
                    parts.append(u[local, gg * GMLP_GROUP_DIM:(gg + 1) * GMLP_GROUP_DIM] * s)
                gated_rows.append(jnp.concatenate(parts, axis=1))
            gateds.append(jnp.concatenate(gated_rows, axis=0).astype(_BF16))
        hs = [h + _dot(gated, wout_ref[c * U_CHUNK:(c + 1) * U_CHUNK, :])
              for h, gated in zip(hs, gateds)]
    for rows, h in zip(row_groups, hs):
        out_ref[rows, :] = h


def _gmlp(h, g, w_uv, v_gain, ws, bs_t, w_out, emit_v, cast_weights=()):
    n_tok = h.shape[0]
    tb = TOKEN_BLOCK
    casts = _WeightCasts(cast_weights, n_tok // tb)
    row = lambda width: pl.BlockSpec((tb, width), lambda i: (i, 0))
    out_specs = [row(D_MODEL)]
    out_shape = [jax.ShapeDtypeStruct(h.shape, _F32)]
    if emit_v:
        out_specs.append(row(GMLP_HALF))
        out_shape.append(jax.ShapeDtypeStruct((n_tok, GMLP_HALF), _F32))
    return pl.pallas_call(
        functools.partial(_gmlp_kernel, emit_v=emit_v, n_casts=len(casts)),
        grid=(n_tok // tb,),
        in_specs=[row(D_MODEL), _resident((1, D_MODEL), layer=1),
                  _resident((D_MODEL, 2 * GMLP_HALF)),
                  _resident((1, GMLP_HALF)),
                  _resident((GMLP_GROUPS, GMLP_CHUNK, GMLP_CHUNK)),
                  _resident((GMLP_CHUNK, GMLP_GROUPS)),
                  _resident((GMLP_HALF, D_MODEL))] + casts.in_specs,
        out_specs=out_specs + casts.out_specs,
        out_shape=out_shape + casts.out_shape,
        scratch_shapes=[pltpu.VMEM((tb, GMLP_HALF), _F32), pltpu.VMEM((tb, GMLP_HALF), _BF16)],
        compiler_params=_params(1),
        name="gmlp_v" if emit_v else "gmlp",
    )(h, g, w_uv, v_gain, ws, bs_t, w_out, *casts.arrays)


def _gmlp_spatial_weights(w_s, b_s, length):
    tril = jnp.tril(jnp.ones((length, length), w_s.dtype))
    ws = w_s[:, :length, :length] * tril
    reps = GMLP_CHUNK // length
    if reps > 1:
        eye = jnp.eye(reps, dtype=w_s.dtype)
        ws = jnp.einsum("ab,gij->gaibj", eye, ws).reshape(GMLP_GROUPS, GMLP_CHUNK, GMLP_CHUNK)
    bs_t = jnp.tile(b_s[:, :length].T, (reps, 1))
    return ws.astype(_BF16), bs_t


def _to_cache_t(cache):
    _, b, rows, kvh, hd = cache.shape
    return jnp.transpose(cache, (0, 1, 3, 4, 2)).reshape(b * kvh * hd, rows)


def _from_cache_t(cache_t, batch):
    t = cache_t.reshape(1, batch, N_KV_HEADS, HEAD_DIM, cache_t.shape[1])
    return jnp.transpose(t, (0, 1, 4, 2, 3))


def kernel(x_prompt, x_sample, p_prompt, p_sample, cache_k, cache_v, g_mix, g_ffn, g_ple,
           attn_w_qkv, attn_q_norm, attn_k_norm, attn_sinks, attn_w_o, gmlp_w_uv, gmlp_v_norm,
           gmlp_w_s, gmlp_b_s, gmlp_w_out, ffn_w1, ffn_w2, ple_w_proj, ple_w_gate):
    batch, seq, _ = x_prompt.shape
    dec_batch, dec_seq, _ = x_sample.shape
    depth = g_mix.shape[0]
    cache_rows = cache_k.shape[2]
    assert depth == 2 and attn_w_qkv.shape[0] == 1 and gmlp_w_uv.shape[0] == 1
    assert seq % FFN_TOKEN_BLOCK == 0 and (dec_batch * dec_seq) % FFN_TOKEN_BLOCK == 0
    assert TOKEN_BLOCK % (2 * dec_seq) == 0 and GMLP_CHUNK % dec_seq == 0 and dec_seq % 16 == 0
    assert cache_rows == WINDOW and dec_seq <= WINDOW

    scale = HEAD_DIM ** -0.5
    g_mix = g_mix.reshape(depth, 1, D_MODEL)
    g_ffn = g_ffn.reshape(depth, 1, D_MODEL)
    g_ple = g_ple.reshape(depth, 1, D_MODEL)
    w_qkv = attn_w_qkv.astype(_BF16)
    qk_gain = jnp.concatenate([jnp.tile(attn_q_norm[0] * scale, N_HEADS),
                               jnp.tile(attn_k_norm[0], N_KV_HEADS)]).reshape(1, QK_DIM)
    sinks = attn_sinks[0].reshape(1, N_HEADS)
    v_gain = gmlp_v_norm[0].reshape(1, GMLP_HALF)
    ws_p, bs_p = _gmlp_spatial_weights(gmlp_w_s[0], gmlp_b_s[0], min(seq, GMLP_CHUNK))
    ws_s, bs_s = _gmlp_spatial_weights(gmlp_w_s[0], gmlp_b_s[0], min(dec_seq, GMLP_CHUNK))

    n_prompt = batch * seq
    n_sample = dec_batch * dec_seq
    x_p = x_prompt.reshape(n_prompt, D_MODEL)
    x_s = x_sample.reshape(n_sample, D_MODEL)
    p_p = p_prompt.reshape(depth, n_prompt, PLE_DIM)
    p_s = p_sample.reshape(depth, n_sample, PLE_DIM)

    layer_weights = lambda layer: [(ffn_w1, layer), (ffn_w2, layer), (ple_w_gate, layer),
                                   (ple_w_proj, layer)]
    q, k, v, w_o, *mlp0 = _qkv_proj(x_p, g_mix, w_qkv, qk_gain,
                                    cast_weights=[(attn_w_o, 0)] + layer_weights(0))
    h_p, nkt_p, nvt_p, w_uv, w_out = _attn_ffn_prompt(
        sinks, x_p, q, k, v, w_o, p_p, g_ffn, mlp0[0], mlp0[1], g_ple, mlp0[2], mlp0[3], seq,
        cast_weights=[(gmlp_w_uv, 0), (gmlp_w_out, 0)])
    h_p, *mlp1 = _gmlp(h_p, g_mix, w_uv, v_gain, ws_p, bs_p, w_out, False,
                       cast_weights=layer_weights(1))
    y_p = _ffn_ple(h_p, p_p, g_ffn, mlp1[0], mlp1[1], g_ple, mlp1[2], mlp1[3], 1)

    q, k, v = _qkv_proj(x_s, g_mix, w_qkv, qk_gain)
    h_s, nkt_s, nvt_s = _attn_sample(sinks, x_s, q, k, v, _to_cache_t(cache_k),
                                     _to_cache_t(cache_v), w_o, dec_seq)
    h_s = _ffn_ple(h_s, p_s, g_ffn, mlp0[0], mlp0[1], g_ple, mlp0[2], mlp0[3], 0)
    h_s, vrows = _gmlp(h_s, g_mix, w_uv, v_gain, ws_s, bs_s, w_out, True)
    y_s = _ffn_ple(h_s, p_s, g_ffn, mlp1[0], mlp1[1], g_ple, mlp1[2], mlp1[3], 1)
    return (y_p.reshape(batch, seq, D_MODEL), y_s.reshape(dec_batch, dec_seq, D_MODEL),
            _from_cache_t(nkt_p, batch), _from_cache_t(nvt_p, batch),
            _from_cache_t(nkt_s, dec_batch), _from_cache_t(nvt_s, dec_batch),
            vrows.reshape(1, dec_batch, dec_seq, GMLP_HALF))
```

```python
import functools
import math

import jax
import jax.numpy as jnp
from jax import lax
from jax.experimental import pallas as pl
from jax.experimental.pallas import tpu as pltpu

D_MODEL = 1024
HEAD_DIM = 64
N_HEADS = 16
N_KV_HEADS = 4
GQA_GROUP = N_HEADS // N_KV_HEADS
KV_DIM = N_KV_HEADS * HEAD_DIM
QK_DIM = D_MODEL + KV_DIM
QKV_DIM = QK_DIM + KV_DIM
CHUNK = 64
WINDOW = 128
GMLP_CHUNK = 128
GMLP_HALF = 3 * D_MODEL
GMLP_GROUPS = 8
GMLP_GROUP_DIM = GMLP_HALF // GMLP_GROUPS
D_FF = 4 * D_MODEL
PLE_DIM = 256
EPS = 1e-6
NEG_INF = -1e30

LANES = 128
MXU_DIM_V7X = 256
BF16_SUBLANES = 16
VMEM_LIMIT_BYTES_V7X = 56 * 1024 * 1024

TOKEN_BLOCK = 512
PAIRED_TOKEN_BLOCK = 2 * TOKEN_BLOCK
ATTN_UNIT = 2 * CHUNK
ATTN_KEYS = 2 * WINDOW
PV_ROWS = HEAD_DIM + BF16_SUBLANES
FF_CHUNK = 1024
GMLP_V_CHUNK = 1024
U_CHUNK = 2 * GMLP_GROUP_DIM

_BF16 = jnp.bfloat16
_F32 = jnp.float32
_CONTRACT_LAST = (((1,), (1,)), ((), ()))


def _resident(shape, layer=None):
    if layer is None:
        return pl.BlockSpec(shape, lambda *_: (0,) * len(shape), pipeline_mode=pl.Buffered(1))
    return pl.BlockSpec((None,) + tuple(shape), lambda *_: (layer,) + (0,) * len(shape),
                        pipeline_mode=pl.Buffered(1))


def _params(n_axes):
    return pltpu.CompilerParams(
        dimension_semantics=("arbitrary",) * n_axes,
        vmem_limit_bytes=VMEM_LIMIT_BYTES_V7X,
    )


def _rms(x, g):
    ms = jnp.mean(x * x, axis=-1, keepdims=True)
    return (x * lax.rsqrt(ms + EPS)) * g


def _dot(a, b):
    return jnp.dot(a, b, preferred_element_type=_F32)


class _WeightCasts:
    def __init__(self, weights, n_slabs):
        self.arrays = [arr for arr, _ in weights]
        self.in_specs, self.out_specs, self.out_shape = [], [], []
        for arr, layer in weights:
            rows, cols = arr.shape[-2:]
            slab = rows // n_slabs
            if rows % n_slabs or slab % BF16_SUBLANES:
                slab, n_slabs_w = rows, 1
            else:
                n_slabs_w = n_slabs
            row_block = lambda s, last=n_slabs_w - 1: jnp.minimum(s, last)
            self.in_specs.append(pl.BlockSpec(
                (None, slab, cols), lambda s, f=row_block, l=layer: (l, f(s), 0)))
            self.out_specs.append(pl.BlockSpec((slab, cols), lambda s, f=row_block: (f(s), 0)))
            self.out_shape.append(jax.ShapeDtypeStruct((rows, cols), _BF16))

    def __len__(self):
        return len(self.arrays)


def _cast_slabs(in_refs, out_refs):
    for src, dst in zip(in_refs, out_refs):
        dst[...] = src[...].astype(_BF16)


def _qkv_kernel(*refs, n_casts):
    x_ref, g_ref, w_ref, qkg_ref = refs[:4]
    q_ref, k_ref, v_ref = refs[4 + n_casts:7 + n_casts]
    _cast_slabs(refs[4:4 + n_casts], refs[7 + n_casts:])
    n_groups = x_ref.shape[0] // TOKEN_BLOCK
    groups = [slice(r * TOKEN_BLOCK, (r + 1) * TOKEN_BLOCK) for r in range(n_groups)]
    qkvs = [_dot(_rms(x_ref[rows, :], g_ref[...]).astype(_BF16), w_ref[...])
            for rows in groups]
    r = lax.broadcasted_iota(jnp.int32, (MXU_DIM_V7X, MXU_DIM_V7X), 0) // HEAD_DIM
    c = lax.broadcasted_iota(jnp.int32, (MXU_DIM_V7X, MXU_DIM_V7X), 1) // HEAD_DIM
    seg = jnp.where(r == c, 1.0, 0.0).astype(_BF16)
    n_q = D_MODEL // MXU_DIM_V7X
    for j in range(QK_DIM // MXU_DIM_V7X):
        cols = slice(j * MXU_DIM_V7X, (j + 1) * MXU_DIM_V7X)
        for rows, qkv in zip(groups, qkvs):
            t = qkv[:, cols]
            ss = _dot((t * t).astype(_BF16), seg)
            inv = lax.rsqrt(ss * (1.0 / HEAD_DIM) + EPS)
            normed = (t * inv) * qkg_ref[:, cols]
            if j < n_q:
                q_ref[rows, cols] = normed.astype(_BF16)
            else:
                k_ref[rows, :] = normed
    for rows, qkv in zip(groups, qkvs):
        v_ref[rows, :] = qkv[:, QK_DIM:]


def _qkv_proj(x, g, w_qkv, qk_gain, cast_weights=()):
    n_tok = x.shape[0]
    tb = PAIRED_TOKEN_BLOCK
    casts = _WeightCasts(cast_weights, n_tok // tb)
    row = lambda width: pl.BlockSpec((tb, width), lambda i: (i, 0))
    return pl.pallas_call(
        functools.partial(_qkv_kernel, n_casts=len(casts)),
        grid=(n_tok // tb,),
        in_specs=[row(D_MODEL), _resident((1, D_MODEL), layer=0),
                  _resident((D_MODEL, QKV_DIM), layer=0), _resident((1, QK_DIM))] + casts.in_specs,
        out_specs=[row(D_MODEL), row(KV_DIM), row(KV_DIM)] + casts.out_specs,
        out_shape=[jax.ShapeDtypeStruct((n_tok, D_MODEL), _BF16),
                   jax.ShapeDtypeStruct((n_tok, KV_DIM), _F32),
                   jax.ShapeDtypeStruct((n_tok, KV_DIM), _F32)] + casts.out_shape,
        compiler_params=_params(1),
        name="qkv_proj",
    )(x, g, w_qkv, qk_gain, *casts.arrays)


def _alibi_slope(head):
    return 2.0 ** (-8.0 * (head + 1) / N_HEADS)


def _fill_bias_t(bias_ref, variant, valid_fn):
    s = lax.broadcasted_iota(jnp.int32, (ATTN_KEYS, ATTN_UNIT), 0)
    t = lax.broadcasted_iota(jnp.int32, (ATTN_KEYS, ATTN_UNIT), 1)
    dist = jnp.abs(t + WINDOW - s).astype(_F32)
    valid = valid_fn(t, s)
    for kvh in range(N_KV_HEADS):
        for p in range(2):
            for g in range(2):
                slope = _alibi_slope(kvh * GQA_GROUP + 2 * g + p)
                bias_ref[variant, kvh, p * ATTN_KEYS:(p + 1) * ATTN_KEYS,
                         g * ATTN_UNIT:(g + 1) * ATTN_UNIT] = jnp.where(valid, -slope * dist, NEG_INF)


def _ffn_ple_steps(h, p_bf16, gf_ref, w1_ref, w2_ref, gp_ref, wg_ref, wp_ref):
    n = _rms(h, gf_ref[...]).astype(_BF16)
    for c in range(D_FF // FF_CHUNK):
        cols = slice(c * FF_CHUNK, (c + 1) * FF_CHUNK)
        a = jnp.maximum(_dot(n, w1_ref[:, cols]), 0.0)
        yield
        h = h + _dot((a * a).astype(_BF16), w2_ref[cols, :])
        yield
    z = _dot(_rms(h, gp_ref[...]).astype(_BF16), wg_ref[...])
    yield
    e = _dot(p_bf16, wp_ref[...])
    yield
    gate = 1.0 / (1.0 + jnp.exp(-z))
    return h + gate * e


class _Staged:
    def __init__(self, steps):
        self._steps = steps
        self.done = False
        self.result = None

    def step(self):
        if not self.done:
            try:
                next(self._steps)
            except StopIteration as stop:
                self.done, self.result = True, stop.value

    def finish(self):
        while not self.done:
            self.step()
        return self.result


def _attn_ffn_prompt_kernel(*refs, blocks_per_seq, n_casts):
    (sinks_ref, x_ref, q_ref, k_ref, kh_ref, v_ref, vh_ref, wo_ref, p_ref,
     gf_ref, w1_ref, w2_ref, gp_ref, wg_ref, wp_ref) = refs[:15]
    out_ref, nkt_ref, nvt_ref = refs[15 + n_casts:18 + n_casts]
    bias_ref, klo_scr, khi_scr, vt_scr, ot_scr, h_scr = refs[18 + 2 * n_casts:]
    _cast_slabs(refs[15:15 + n_casts], refs[18 + n_casts:18 + 2 * n_casts])
    step = pl.program_id(0)
    tb = TOKEN_BLOCK
    block = jnp.minimum(step, pl.num_programs(0) - 2)

    @pl.when(step == 0)
    def _():
        in_window = lambda t, s: ((t < CHUNK) & (s < WINDOW + CHUNK)) | ((t >= CHUNK) & (s >= CHUNK))
        _fill_bias_t(bias_ref, 0, in_window)
        _fill_bias_t(bias_ref, 1, lambda t, s: in_window(t, s) & (s >= WINDOW))
        vt_scr[:, HEAD_DIM:, :] = jnp.ones((N_KV_HEADS, PV_ROWS - HEAD_DIM, WINDOW + tb), _BF16)
        h_scr[...] = jnp.zeros_like(h_scr)

    mlp = _Staged(_ffn_ple_steps(h_scr[...], p_ref[...].astype(_BF16), gf_ref, w1_ref, w2_ref,
                                 gp_ref, wg_ref, wp_ref))
    mlp.step()

    kd = jnp.concatenate([kh_ref[...], k_ref[...]], axis=0)
    low_half = lax.broadcasted_iota(jnp.int32, (WINDOW + tb, LANES), 1) < HEAD_DIM
    for pair in range(N_KV_HEADS // 2):
        kk = kd[:, pair * LANES:(pair + 1) * LANES]
        swapped = pltpu.roll(kk, HEAD_DIM, axis=1)
        klo_scr[2 * pair] = jnp.where(low_half, kk, 0.0).astype(_BF16)
        khi_scr[2 * pair] = jnp.where(low_half, 0.0, swapped).astype(_BF16)
        klo_scr[2 * pair + 1] = jnp.where(low_half, swapped, 0.0).astype(_BF16)
        khi_scr[2 * pair + 1] = jnp.where(low_half, 0.0, kk).astype(_BF16)
    vt = jnp.concatenate([vh_ref[...], v_ref[...]], axis=0).T
    for kvh in range(N_KV_HEADS):
        vt_scr[kvh, :HEAD_DIM, :] = vt[kvh * HEAD_DIM:(kvh + 1) * HEAD_DIM, :].astype(_BF16)

    nkt_ref[...] = k_ref[tb - WINDOW:, :].T
    nvt_ref[...] = vt[:, tb:]

    first_in_seq = jnp.where(block % blocks_per_seq == 0, 1, 0)
    lane2 = lax.broadcasted_iota(jnp.int32, (1, 2 * ATTN_UNIT), 1)
    n_units = tb // ATTN_UNIT

    def scores(u):
        rows = slice(u * ATTN_UNIT, (u + 1) * ATTN_UNIT)
        keys = slice(u * ATTN_UNIT, u * ATTN_UNIT + ATTN_KEYS)
        variant = first_in_seq if u == 0 else 0
        out = []
        for kvh in range(N_KV_HEADS):
            k2 = jnp.concatenate([klo_scr[kvh, keys, :], khi_scr[kvh, keys, :]], axis=0)
            q2 = jnp.concatenate(
                [q_ref[rows, (2 * kvh + g) * LANES:(2 * kvh + g + 1) * LANES] for g in range(2)],
                axis=0)
            s_t = lax.dot_general(k2, q2, _CONTRACT_LAST, preferred_element_type=_F32)
            out.append(s_t + bias_ref[variant, kvh])
        return out

    s_next = scores(0)
    for u in range(n_units):
        rows = slice(u * ATTN_UNIT, (u + 1) * ATTN_UNIT)
        keys = slice(u * ATTN_UNIT, u * ATTN_UNIT + ATTN_KEYS)
        s_cur = s_next
        if u + 1 < n_units:
            s_next = scores(u + 1)
        mlp.step()
        chains = []
        for kvh in range(N_KV_HEADS):
            for p in range(2):
                head0 = kvh * GQA_GROUP + p
                sink = jnp.where(lane2 < ATTN_UNIT, sinks_ref[0, head0], sinks_ref[0, head0 + 2])
                sp = s_cur[kvh][p * ATTN_KEYS:(p + 1) * ATTN_KEYS, :]
                m = jnp.maximum(jnp.max(sp, axis=0, keepdims=True), sink)
                chains.append((head0, kvh, jnp.exp(sp - m).astype(_BF16), jnp.exp(sink - m)))
        results = [(head0, _dot(vt_scr[kvh, :, keys], e), sink_e)
                   for head0, kvh, e, sink_e in chains]
        mlp.step()
        for head0, r, sink_e in results:
            den = r[HEAD_DIM:HEAD_DIM + 1, :] + sink_e
            o = r[:HEAD_DIM, :] * (1.0 / den)
            for g in range(2):
                head = head0 + 2 * g
                ot_scr[head * HEAD_DIM:(head + 1) * HEAD_DIM, rows] = (
                    o[:, g * ATTN_UNIT:(g + 1) * ATTN_UNIT].astype(_BF16))

    out_ref[...] = mlp.finish()
    h_scr[...] = x_ref[...] + _dot(ot_scr[...].T, wo_ref[...])


def _attn_ffn_prompt(sinks, x, q, k, v, w_o, p, g_ffn, w1, w2, g_ple, w_gate, w_proj, seq,
                     cast_weights=()):
    tb = TOKEN_BLOCK
    layer = 0
    n_blocks = x.shape[0] // tb
    casts = _WeightCasts(cast_weights, n_blocks)
    nt = seq // tb
    halo_per_block = tb // WINDOW
    attn_block = lambda s: jnp.minimum(s, n_blocks - 1)
    mlp_block = lambda s: jnp.maximum(s - 1, 0)
    row = lambda width: pl.BlockSpec((tb, width), lambda s: (attn_block(s), 0))
    halo = pl.BlockSpec(
        (WINDOW, KV_DIM), lambda s: (jnp.maximum(attn_block(s) * halo_per_block - 1, 0), 0))
    per_seq = pl.BlockSpec((KV_DIM, WINDOW), lambda s: (attn_block(s) // nt, 0))
    return pl.pallas_call(
        functools.partial(_attn_ffn_prompt_kernel, blocks_per_seq=nt, n_casts=len(casts)),
        grid=(n_blocks + 1,),
        in_specs=[pl.BlockSpec(memory_space=pltpu.SMEM),
                  row(D_MODEL), row(D_MODEL), row(KV_DIM), halo, row(KV_DIM), halo,
                  _resident((D_MODEL, D_MODEL)),
                  pl.BlockSpec((None, tb, PLE_DIM), lambda s: (layer, mlp_block(s), 0)),
                  _resident((1, D_MODEL), layer),
                  _resident((D_MODEL, D_FF)), _resident((D_FF, D_MODEL)),
                  _resident((1, D_MODEL), layer), _resident((D_MODEL, D_MODEL)),
                  _resident((PLE_DIM, D_MODEL))] + casts.in_specs,
        out_specs=[pl.BlockSpec((tb, D_MODEL), lambda s: (mlp_block(s), 0)), per_seq,
                   per_seq] + casts.out_specs,
        out_shape=[jax.ShapeDtypeStruct(x.shape, _F32),
                   jax.ShapeDtypeStruct((x.shape[0] // seq * KV_DIM, WINDOW), _F32),
                   jax.ShapeDtypeStruct((x.shape[0] // seq * KV_DIM, WINDOW), _F32)]
        + casts.out_shape,
        scratch_shapes=[
            pltpu.VMEM((2, N_KV_HEADS, 2 * ATTN_KEYS, 2 * ATTN_UNIT), _F32),
            pltpu.VMEM((N_KV_HEADS, WINDOW + tb, LANES), _BF16),
            pltpu.VMEM((N_KV_HEADS, WINDOW + tb, LANES), _BF16),
            pltpu.VMEM((N_KV_HEADS, PV_ROWS, WINDOW + tb), _BF16),
            pltpu.VMEM((D_MODEL, tb), _BF16),
            pltpu.VMEM((tb, D_MODEL), _F32),
        ],
        compiler_params=_params(1),
        name="attn_ffn_prompt",
    )(sinks, x, q, k, k, v, v, w_o, p, g_ffn, w1, w2, g_ple, w_gate, w_proj, *casts.arrays)


def _fill_bias(bias_new_ref, bias_old_ref, q_rows):
    t = lax.broadcasted_iota(jnp.int32, (q_rows, WINDOW), 0)
    s = lax.broadcasted_iota(jnp.int32, (q_rows, WINDOW), 1)
    dist_new = jnp.abs(t + WINDOW - (s + q_rows)).astype(_F32)
    dist_old = jnp.abs(t + WINDOW - s).astype(_F32)
    for kvh in range(N_KV_HEADS):
        for g in range(2):
            for p in range(2):
                slope = _alibi_slope(kvh * GQA_GROUP + 2 * g + p)
                rows = slice(g * q_rows, (g + 1) * q_rows)
                cols = slice(p * WINDOW, (p + 1) * WINDOW)
                bias_new_ref[kvh, rows, cols] = -slope * dist_new
                bias_old_ref[kvh, rows, cols] = jnp.where(s < q_rows, -slope * dist_old, NEG_INF)


def _block_diag2(a):
    z = jnp.zeros_like(a)
    return jnp.concatenate([jnp.concatenate([a, z], axis=1), jnp.concatenate([z, a], axis=1)],
                           axis=0)


def _attn_sample_kernel(sinks_ref, x_ref, q_ref, k_ref, ckt_ref, v_ref, cvt_ref, wo_ref,
                        out_ref, nkt_ref, nvt_ref, bias_new_ref, bias_old_ref, o_scr, *, dec_seq):
    kept = WINDOW - dec_seq

    @pl.when(pl.program_id(0) == 0)
    def _():
        _fill_bias(bias_new_ref, bias_old_ref, dec_seq)

    keep_lane = lax.broadcasted_iota(jnp.int32, (HEAD_DIM, WINDOW), 1) < kept
    pad = jnp.zeros((kept, LANES), _F32)
    ones_col = lax.broadcasted_iota(jnp.int32, (2 * BF16_SUBLANES, 2 * WINDOW), 1) // WINDOW
    ones_row = lax.broadcasted_iota(jnp.int32, (2 * BF16_SUBLANES, 2 * WINDOW), 0) // BF16_SUBLANES
    sum_rows = jnp.where(ones_col == ones_row, 1.0, 0.0).astype(_BF16)
    lane2 = lax.broadcasted_iota(jnp.int32, (2 * dec_seq, 2 * WINDOW), 1)
    lane1 = lax.broadcasted_iota(jnp.int32, (2 * dec_seq, LANES), 1)
    row2 = lax.broadcasted_iota(jnp.int32, (2 * dec_seq, 1), 0)

    def new_cols_t(rows_f32):
        return jnp.concatenate([pad, rows_f32], axis=0).T

    def one_batch(bb, carry):
        rows = pl.ds(pl.multiple_of(bb * dec_seq, dec_seq), dec_seq)
        k_new = k_ref[rows, :]
        v_new = v_ref[rows, :]
        kt_new = [new_cols_t(k_new[:, g * LANES:(g + 1) * LANES]) for g in range(KV_DIM // LANES)]
        vt_new = [new_cols_t(v_new[:, g * LANES:(g + 1) * LANES]) for g in range(KV_DIM // LANES)]

        staged = []
        for kvh in range(N_KV_HEADS):
            cache = pl.ds(pl.multiple_of(bb * KV_DIM + kvh * HEAD_DIM, HEAD_DIM), HEAD_DIM)
            kt_old = ckt_ref[cache, :]
            vt_old = cvt_ref[cache, :]
            half = slice((kvh % 2) * HEAD_DIM, (kvh % 2 + 1) * HEAD_DIM)
            kt = jnp.where(keep_lane, pltpu.roll(kt_old, kept, axis=1), kt_new[kvh // 2][half, :])
            vt = jnp.where(keep_lane, pltpu.roll(vt_old, kept, axis=1), vt_new[kvh // 2][half, :])
            nkt_ref[cache, :] = kt
            nvt_ref[cache, :] = vt
            q2 = jnp.concatenate(
                [q_ref[rows, (2 * kvh + g) * LANES:(2 * kvh + g + 1) * LANES] for g in range(2)],
                axis=0)
            s_new = _dot(q2, _block_diag2(kt.astype(_BF16))) + bias_new_ref[kvh]
            s_old = _dot(q2, _block_diag2(kt_old.astype(_BF16))) + bias_old_ref[kvh]
            v2_new = jnp.concatenate([_block_diag2(vt.astype(_BF16)), sum_rows], axis=0)
            v2_old = jnp.concatenate([_block_diag2(vt_old.astype(_BF16)), sum_rows], axis=0)
            staged.append((s_new, s_old, v2_new, v2_old))

        soft = []
        for kvh, (s_new, s_old, v2_new, v2_old) in enumerate(staged):
            s_max = jnp.maximum(s_new, s_old)
            head0 = kvh * GQA_GROUP
            ms, sink_es = [], []
            for p in range(2):
                sink = jnp.where(row2 < dec_seq, sinks_ref[0, head0 + p], sinks_ref[0, head0 + 2 + p])
                m_p = jnp.maximum(
                    jnp.max(s_max[:, p * WINDOW:(p + 1) * WINDOW], axis=-1, keepdims=True), sink)
                ms.append(m_p)
                sink_es.append(jnp.exp(sink - m_p))
            m = jnp.where(lane2 < WINDOW, ms[0], ms[1])
            soft.append((jnp.exp(s_new - m).astype(_BF16), jnp.exp(s_old - m).astype(_BF16),
                         sink_es, v2_new, v2_old))

        outs = [(lax.dot_general(e_new, v2_new, _CONTRACT_LAST, preferred_element_type=_F32)
                 + lax.dot_general(e_old, v2_old, _CONTRACT_LAST, preferred_element_type=_F32),
                 sink_es) for e_new, e_old, sink_es, v2_new, v2_old in soft]
        for kvh, (r, sink_es) in enumerate(outs):
            dens = [r[:, LANES + p * BF16_SUBLANES:LANES + p * BF16_SUBLANES + 1] + sink_es[p]
                    for p in range(2)]
            inv = jnp.where(lane1 < HEAD_DIM, 1.0 / dens[0], 1.0 / dens[1])
            o = (r[:, :LANES] * inv).astype(_BF16)
            for g in range(2):
                o_scr[rows, (2 * kvh + g) * LANES:(2 * kvh + g + 1) * LANES] = (
                    o[g * dec_seq:(g + 1) * dec_seq, :])
        return carry

    lax.fori_loop(0, TOKEN_BLOCK // dec_seq, one_batch, 0, unroll=2)
    out_ref[...] = x_ref[...] + _dot(o_scr[...], wo_ref[...])


def _attn_sample(sinks, x, q, k, v, cache_kt, cache_vt, w_o, dec_seq):
    n_tok = x.shape[0]
    tb = TOKEN_BLOCK
    batches_per_block = tb // dec_seq
    row = lambda width: pl.BlockSpec((tb, width), lambda i: (i, 0))
    cache = pl.BlockSpec((batches_per_block * KV_DIM, WINDOW), lambda i: (i, 0))
    return pl.pallas_call(
        functools.partial(_attn_sample_kernel, dec_seq=dec_seq),
        grid=(n_tok // tb,),
        in_specs=[pl.BlockSpec(memory_space=pltpu.SMEM),
                  row(D_MODEL), row(D_MODEL), row(KV_DIM), cache, row(KV_DIM), cache,
                  _resident((D_MODEL, D_MODEL))],
        out_specs=[row(D_MODEL), cache, cache],
        out_shape=[jax.ShapeDtypeStruct(x.shape, _F32),
                   jax.ShapeDtypeStruct(cache_kt.shape, _F32),
                   jax.ShapeDtypeStruct(cache_vt.shape, _F32)],
        scratch_shapes=[
            pltpu.VMEM((N_KV_HEADS, 2 * dec_seq, 2 * WINDOW), _F32),
            pltpu.VMEM((N_KV_HEADS, 2 * dec_seq, 2 * WINDOW), _F32),
            pltpu.VMEM((tb, D_MODEL), _BF16),
        ],
        compiler_params=_params(1),
        name="attn_sample",
    )(sinks, x, q, k, cache_kt, v, cache_vt, w_o)


def _ffn_ple_kernel(h_ref, p_ref, gf_ref, w1_ref, w2_ref, gp_ref, wg_ref, wp_ref, out_ref):
    n_groups = h_ref.shape[0] // TOKEN_BLOCK
    groups = [slice(r * TOKEN_BLOCK, (r + 1) * TOKEN_BLOCK) for r in range(n_groups)]
    runs = [_Staged(_ffn_ple_steps(h_ref[rows, :], p_ref[rows, :].astype(_BF16), gf_ref, w1_ref,
                                   w2_ref, gp_ref, wg_ref, wp_ref)) for rows in groups]
    while not all(run.done for run in runs):
        for run in runs:
            run.step()
    for rows, run in zip(groups, runs):
        out_ref[rows, :] = run.result


def _ffn_ple(h, p, g_ffn, w1, w2, g_ple, w_gate, w_proj, layer):
    n_tok = h.shape[0]
    tb = PAIRED_TOKEN_BLOCK
    row = lambda width: pl.BlockSpec((tb, width), lambda i: (i, 0))
    return pl.pallas_call(
        _ffn_ple_kernel,
        grid=(n_tok // tb,),
        in_specs=[row(D_MODEL), pl.BlockSpec((None, tb, PLE_DIM), lambda i: (layer, i, 0)),
                  _resident((1, D_MODEL), layer),
                  _resident((D_MODEL, D_FF)), _resident((D_FF, D_MODEL)),
                  _resident((1, D_MODEL), layer), _resident((D_MODEL, D_MODEL)),
                  _resident((PLE_DIM, D_MODEL))],
        out_specs=row(D_MODEL),
        out_shape=jax.ShapeDtypeStruct(h.shape, _F32),
        compiler_params=_params(1),
        name="ffn_ple",
    )(h, p, g_ffn, w1, w2, g_ple, w_gate, w_proj)


def _gelu_tanh(x):
    c = math.sqrt(2.0 / math.pi)
    return x * (0.5 * (1.0 + jnp.tanh(c * (x + 0.044715 * (x * x * x)))))


def _gmlp_kernel(*refs, emit_v, n_casts):
    h_ref, g_ref, wuv_ref, vg_ref, ws_ref, bs_ref, wout_ref = refs[:7]
    n_out = 2 if emit_v else 1
    out_ref = refs[7 + n_casts]
    vout_ref = refs[8 + n_casts] if emit_v else None
    v_scr, vb_scr = refs[7 + 2 * n_casts + n_out:]
    _cast_slabs(refs[7:7 + n_casts], refs[7 + n_casts + n_out:7 + 2 * n_casts + n_out])
    tb = h_ref.shape[0]
    h = h_ref[...]
    n = _rms(h, g_ref[...]).astype(_BF16)

    ssq = jnp.zeros((tb, 1), _F32)
    for c in range(GMLP_HALF // GMLP_V_CHUNK):
        cols = slice(c * GMLP_V_CHUNK, (c + 1) * GMLP_V_CHUNK)
        w_cols = slice(GMLP_HALF + c * GMLP_V_CHUNK, GMLP_HALF + (c + 1) * GMLP_V_CHUNK)
        vc = _gelu_tanh(_dot(n, wuv_ref[:, w_cols]))
        v_scr[:, cols] = vc
        ssq = ssq + jnp.sum(vc * vc, axis=-1, keepdims=True)
    inv = lax.rsqrt(ssq * (1.0 / GMLP_HALF) + EPS)
    for c in range(GMLP_HALF // GMLP_V_CHUNK):
        cols = slice(c * GMLP_V_CHUNK, (c + 1) * GMLP_V_CHUNK)
        vn = (v_scr[:, cols] * inv) * vg_ref[:, cols]
        if emit_v:
            vout_ref[:, cols] = vn
        vb_scr[:, cols] = vn.astype(_BF16)

    groups_per_step = U_CHUNK // GMLP_GROUP_DIM
    n_chunks = GMLP_HALF // U_CHUNK
    u_proj = lambda c: _gelu_tanh(_dot(n, wuv_ref[:, c * U_CHUNK:(c + 1) * U_CHUNK]))
    u_next = u_proj(0)
    for c in range(n_chunks):
        u = u_next
        if c + 1 < n_chunks:
            u_next = u_proj(c + 1)
        gated_rows = []
        for r in range(tb // GMLP_CHUNK):
            rows = slice(r * GMLP_CHUNK, (r + 1) * GMLP_CHUNK)
            parts = []
            for gg in range(groups_per_step):
                grp = c * groups_per_step + gg
                cols = slice(grp * GMLP_GROUP_DIM, (grp + 1) * GMLP_GROUP_DIM)
                s = _dot(ws_ref[grp], vb_scr[rows, cols]) + bs_ref[:, grp:grp + 1]
                parts.append(u[rows, gg * GMLP_GROUP_DIM:(gg + 1) * GMLP_GROUP_DIM] * s)
            gated_rows.append(jnp.concatenate(parts, axis=1))
        gated = jnp.concatenate(gated_rows, axis=0).astype(_BF16)
        h = h + _dot(gated, wout_ref[c * U_CHUNK:(c + 1) * U_CHUNK, :])
    out_ref[...] = h


def _gmlp(h, g, w_uv, v_gain, ws, bs_t, w_out, emit_v, cast_weights=()):
    n_tok = h.shape[0]
    tb = TOKEN_BLOCK
    casts = _WeightCasts(cast_weights, n_tok // tb)
    row = lambda width: pl.BlockSpec((tb, width), lambda i: (i, 0))
    out_specs = [row(D_MODEL)]
    out_shape = [jax.ShapeDtypeStruct(h.shape, _F32)]
    if emit_v:
        out_specs.append(row(GMLP_HALF))
        out_shape.append(jax.ShapeDtypeStruct((n_tok, GMLP_HALF), _F32))
    return pl.pallas_call(
        functools.partial(_gmlp_kernel, emit_v=emit_v, n_casts=len(casts)),
        grid=(n_tok // tb,),
        in_specs=[row(D_MODEL), _resident((1, D_MODEL), layer=1),
                  _resident((D_MODEL, 2 * GMLP_HALF)),
                  _resident((1, GMLP_HALF)),
                  _resident((GMLP_GROUPS, GMLP_CHUNK, GMLP_CHUNK)),
                  _resident((GMLP_CHUNK, GMLP_GROUPS)),
                  _resident((GMLP_HALF, D_MODEL))] + casts.in_specs,
        out_specs=out_specs + casts.out_specs,
        out_shape=out_shape + casts.out_shape,
        scratch_shapes=[pltpu.VMEM((tb, GMLP_HALF), _F32), pltpu.VMEM((tb, GMLP_HALF), _BF16)],
        compiler_params=_params(1),
        name="gmlp_v" if emit_v else "gmlp",
    )(h, g, w_uv, v_gain, ws, bs_t, w_out, *casts.arrays)


def _gmlp_spatial_weights(w_s, b_s, length):
    tril = jnp.tril(jnp.ones((length, length), w_s.dtype))
    ws = w_s[:, :length, :length] * tril
    reps = GMLP_CHUNK // length
    if reps > 1:
        eye = jnp.eye(reps, dtype=w_s.dtype)
        ws = jnp.einsum("ab,gij->gaibj", eye, ws).reshape(GMLP_GROUPS, GMLP_CHUNK, GMLP_CHUNK)
    bs_t = jnp.tile(b_s[:, :length].T, (reps, 1))
    return ws.astype(_BF16), bs_t


def _to_cache_t(cache):
    _, b, rows, kvh, hd = cache.shape
    return jnp.transpose(cache, (0, 1, 3, 4, 2)).reshape(b * kvh * hd, rows)


def _from_cache_t(cache_t, batch):
    t = cache_t.reshape(1, batch, N_KV_HEADS, HEAD_DIM, cache_t.shape[1])
    return jnp.transpose(t, (0, 1, 4, 2, 3))


def kernel(x_prompt, x_sample, p_prompt, p_sample, cache_k, cache_v, g_mix, g_ffn, g_ple,
           attn_w_qkv, attn_q_norm, attn_k_norm, attn_sinks, attn_w_o, gmlp_w_uv, gmlp_v_norm,
           gmlp_w_s, gmlp_b_s, gmlp_w_out, ffn_w1, ffn_w2, ple_w_proj, ple_w_gate):
    batch, seq, _ = x_prompt.shape
    dec_batch, dec_seq, _ = x_sample.shape
    depth = g_mix.shape[0]
    cache_rows = cache_k.shape[2]
    assert depth == 2 and attn_w_qkv.shape[0] == 1 and gmlp_w_uv.shape[0] == 1
    assert seq % PAIRED_TOKEN_BLOCK == 0 and (dec_batch * dec_seq) % PAIRED_TOKEN_BLOCK == 0
    assert TOKEN_BLOCK % (2 * dec_seq) == 0 and GMLP_CHUNK % dec_seq == 0 and dec_seq % 16 == 0
    assert cache_rows == WINDOW and dec_seq <= WINDOW

    scale = HEAD_DIM ** -0.5
    g_mix = g_mix.reshape(depth, 1, D_MODEL)
    g_ffn = g_ffn.reshape(depth, 1, D_MODEL)
    g_ple = g_ple.reshape(depth, 1, D_MODEL)
    w_qkv = attn_w_qkv.astype(_BF16)
    qk_gain = jnp.concatenate([jnp.tile(attn_q_norm[0] * scale, N_HEADS),
                               jnp.tile(attn_k_norm[0], N_KV_HEADS)]).reshape(1, QK_DIM)
    sinks = attn_sinks[0].reshape(1, N_HEADS)
    v_gain = gmlp_v_norm[0].reshape(1, GMLP_HALF)
    ws_p, bs_p = _gmlp_spatial_weights(gmlp_w_s[0], gmlp_b_s[0], min(seq, GMLP_CHUNK))
    ws_s, bs_s = _gmlp_spatial_weights(gmlp_w_s[0], gmlp_b_s[0], min(dec_seq, GMLP_CHUNK))

    n_prompt = batch * seq
    n_sample = dec_batch * dec_seq
    x_p = x_prompt.reshape(n_prompt, D_MODEL)
    x_s = x_sample.reshape(n_sample, D_MODEL)
    p_p = p_prompt.reshape(depth, n_prompt, PLE_DIM)
    p_s = p_sample.reshape(depth, n_sample, PLE_DIM)

    layer_weights = lambda layer: [(ffn_w1, layer), (ffn_w2, layer), (ple_w_gate, layer),
                                   (ple_w_proj, layer)]
    q, k, v, w_o, *mlp0 = _qkv_proj(x_p, g_mix, w_qkv, qk_gain,
                                    cast_weights=[(attn_w_o, 0)] + layer_weights(0))
    h_p, nkt_p, nvt_p, w_uv, w_out = _attn_ffn_prompt(
        sinks, x_p, q, k, v, w_o, p_p, g_ffn, mlp0[0], mlp0[1], g_ple, mlp0[2], mlp0[3], seq,
        cast_weights=[(gmlp_w_uv, 0), (gmlp_w_out, 0)])
    h_p, *mlp1 = _gmlp(h_p, g_mix, w_uv, v_gain, ws_p, bs_p, w_out, False,
                       cast_weights=layer_weights(1))
    y_p = _ffn_ple(h_p, p_p, g_ffn, mlp1[0], mlp1[1], g_ple, mlp1[2], mlp1[3], 1)

    q, k, v = _qkv_proj(x_s, g_mix, w_qkv, qk_gain)
    h_s, nkt_s, nvt_s = _attn_sample(sinks, x_s, q, k, v, _to_cache_t(cache_k),
                                     _to_cache_t(cache_v), w_o, dec_seq)
    h_s = _ffn_ple(h_s, p_s, g_ffn, mlp0[0], mlp0[1], g_ple, mlp0[2], mlp0[3], 0)
    h_s, vrows = _gmlp(h_s, g_mix, w_uv, v_gain, ws_s, bs_s, w_out, True)
    y_s = _ffn_ple(h_s, p_s, g_ffn, mlp1[0], mlp1[1], g_ple, mlp1[2], mlp1[3], 1)
    return (y_p.reshape(batch, seq, D_MODEL), y_s.reshape(dec_batch, dec_seq, D_MODEL),
            _from_cache_t(nkt_p, batch), _from_cache_t(nvt_p, batch),
            _from_cache_t(nkt_s, dec_batch), _from_cache_t(nvt_s, dec_batch),
            vrows.reshape(1, dec_batch, dec_seq, GMLP_HALF))
```

```python
import functools
import math

import jax
import jax.numpy as jnp
from jax import lax
from jax.experimental import pallas as pl
from jax.experimental.pallas import tpu as pltpu

D_MODEL = 1024
HEAD_DIM = 64
N_HEADS = 16
N_KV_HEADS = 4
GQA_GROUP = N_HEADS // N_KV_HEADS
KV_DIM = N_KV_HEADS * HEAD_DIM
QK_DIM = D_MODEL + KV_DIM
QKV_DIM = QK_DIM + KV_DIM
CHUNK = 64
WINDOW = 128
GMLP_CHUNK = 128
GMLP_HALF = 3 * D_MODEL
GMLP_GROUPS = 8
GMLP_GROUP_DIM = GMLP_HALF // GMLP_GROUPS
D_FF = 4 * D_MODEL
PLE_DIM = 256
EPS = 1e-6
NEG_INF = -1e30

LANES = 128
MXU_DIM_V7X = 256
BF16_SUBLANES = 16
VMEM_LIMIT_BYTES_V7X = 56 * 1024 * 1024

TOKEN_BLOCK = 512
PAIRED_TOKEN_BLOCK = 2 * TOKEN_BLOCK
ATTN_UNIT = 2 * CHUNK
ATTN_KEYS = 2 * WINDOW
PV_ROWS = HEAD_DIM + BF16_SUBLANES
QKV_CHUNK = 512
FF_CHUNK = 1024
GMLP_V_CHUNK = 1024
U_CHUNK = 2 * GMLP_GROUP_DIM

_BF16 = jnp.bfloat16
_F32 = jnp.float32
_CONTRACT_LAST = (((1,), (1,)), ((), ()))


def _resident(shape, layer=None):
    if layer is None:
        return pl.BlockSpec(shape, lambda *_: (0,) * len(shape), pipeline_mode=pl.Buffered(1))
    return pl.BlockSpec((None,) + tuple(shape), lambda *_: (layer,) + (0,) * len(shape),
                        pipeline_mode=pl.Buffered(1))


def _params(n_axes):
    return pltpu.CompilerParams(
        dimension_semantics=("arbitrary",) * n_axes,
        vmem_limit_bytes=VMEM_LIMIT_BYTES_V7X,
    )


def _rms(x, g):
    ms = jnp.mean(x * x, axis=-1, keepdims=True)
    return (x * lax.rsqrt(ms + EPS)) * g


def _dot(a, b):
    return jnp.dot(a, b, preferred_element_type=_F32)


class _WeightCasts:
    def __init__(self, weights, n_slabs):
        self.arrays = [arr for arr, _ in weights]
        self.in_specs, self.out_specs, self.out_shape = [], [], []
        for arr, layer in weights:
            rows, cols = arr.shape[-2:]
            slab = rows // n_slabs
            if rows % n_slabs or slab % BF16_SUBLANES:
                slab, n_slabs_w = rows, 1
            else:
                n_slabs_w = n_slabs
            row_block = lambda s, last=n_slabs_w - 1: jnp.minimum(s, last)
            self.in_specs.append(pl.BlockSpec(
                (None, slab, cols), lambda s, f=row_block, l=layer: (l, f(s), 0)))
            self.out_specs.append(pl.BlockSpec((slab, cols), lambda s, f=row_block: (f(s), 0)))
            self.out_shape.append(jax.ShapeDtypeStruct((rows, cols), _BF16))

    def __len__(self):
        return len(self.arrays)


def _cast_slabs(in_refs, out_refs):
    for src, dst in zip(in_refs, out_refs):
        dst[...] = src[...].astype(_BF16)


class _Staged:
    def __init__(self, steps):
        self._steps = steps
        self.done = False
        self.result = None

    def step(self):
        if not self.done:
            try:
                next(self._steps)
            except StopIteration as stop:
                self.done, self.result = True, stop.value

    def finish(self):
        while not self.done:
            self.step()
        return self.result


def _run_interleaved(runs):
    while not all(run.done for run in runs):
        for run in runs:
            run.step()


def _qkv_steps(x, g_ref, w_ref, qkg_ref, q_dst, k_dst, v_dst):
    n = _rms(x, g_ref[...]).astype(_BF16)
    parts = []
    for c in range(QKV_DIM // QKV_CHUNK):
        parts.append(_dot(n, w_ref[:, c * QKV_CHUNK:(c + 1) * QKV_CHUNK]))
        yield
    qkv = jnp.concatenate(parts, axis=1)
    r = lax.broadcasted_iota(jnp.int32, (MXU_DIM_V7X, MXU_DIM_V7X), 0) // HEAD_DIM
    c = lax.broadcasted_iota(jnp.int32, (MXU_DIM_V7X, MXU_DIM_V7X), 1) // HEAD_DIM
    seg = jnp.where(r == c, 1.0, 0.0).astype(_BF16)
    n_q = D_MODEL // MXU_DIM_V7X
    for j in range(QK_DIM // MXU_DIM_V7X):
        cols = slice(j * MXU_DIM_V7X, (j + 1) * MXU_DIM_V7X)
        t = qkv[:, cols]
        ss = _dot((t * t).astype(_BF16), seg)
        yield
        inv = lax.rsqrt(ss * (1.0 / HEAD_DIM) + EPS)
        normed = (t * inv) * qkg_ref[:, cols]
        if j < n_q:
            q_dst[:, cols] = normed.astype(_BF16)
        else:
            k_dst[...] = normed
    v_dst[...] = qkv[:, QK_DIM:]


def _qkv_kernel(x_ref, g_ref, w_ref, qkg_ref, q_ref, k_ref, v_ref):
    n_groups = x_ref.shape[0] // TOKEN_BLOCK
    groups = [pl.ds(r * TOKEN_BLOCK, TOKEN_BLOCK) for r in range(n_groups)]
    _run_interleaved([
        _Staged(_qkv_steps(x_ref[rows, :], g_ref, w_ref, qkg_ref, q_ref.at[rows], k_ref.at[rows],
                           v_ref.at[rows])) for rows in groups])


def _qkv_proj(x, g, w_qkv, qk_gain):
    n_tok = x.shape[0]
    tb = PAIRED_TOKEN_BLOCK
    row = lambda width: pl.BlockSpec((tb, width), lambda i: (i, 0))
    return pl.pallas_call(
        _qkv_kernel,
        grid=(n_tok // tb,),
        in_specs=[row(D_MODEL), _resident((1, D_MODEL), layer=0),
                  _resident((D_MODEL, QKV_DIM)), _resident((1, QK_DIM))],
        out_specs=[row(D_MODEL), row(KV_DIM), row(KV_DIM)],
        out_shape=[jax.ShapeDtypeStruct((n_tok, D_MODEL), _BF16),
                   jax.ShapeDtypeStruct((n_tok, KV_DIM), _F32),
                   jax.ShapeDtypeStruct((n_tok, KV_DIM), _F32)],
        compiler_params=_params(1),
        name="qkv_proj",
    )(x, g, w_qkv, qk_gain)


def _alibi_slope(head):
    return 2.0 ** (-8.0 * (head + 1) / N_HEADS)


def _fill_bias_t(bias_ref, variant, valid_fn):
    s = lax.broadcasted_iota(jnp.int32, (ATTN_KEYS, ATTN_UNIT), 0)
    t = lax.broadcasted_iota(jnp.int32, (ATTN_KEYS, ATTN_UNIT), 1)
    dist = jnp.abs(t + WINDOW - s).astype(_F32)
    valid = valid_fn(t, s)
    for kvh in range(N_KV_HEADS):
        for p in range(2):
            for g in range(2):
                slope = _alibi_slope(kvh * GQA_GROUP + 2 * g + p)
                bias_ref[variant, kvh, p * ATTN_KEYS:(p + 1) * ATTN_KEYS,
                         g * ATTN_UNIT:(g + 1) * ATTN_UNIT] = jnp.where(valid, -slope * dist, NEG_INF)


def _ffn_ple_steps(h, p_bf16, gf_ref, w1_ref, w2_ref, gp_ref, wg_ref, wp_ref):
    n = _rms(h, gf_ref[...]).astype(_BF16)
    for c in range(D_FF // FF_CHUNK):
        cols = slice(c * FF_CHUNK, (c + 1) * FF_CHUNK)
        a = jnp.maximum(_dot(n, w1_ref[:, cols]), 0.0)
        yield
        h = h + _dot((a * a).astype(_BF16), w2_ref[cols, :])
        yield
    z = _dot(_rms(h, gp_ref[...]).astype(_BF16), wg_ref[...])
    yield
    e = _dot(p_bf16, wp_ref[...])
    yield
    gate = 1.0 / (1.0 + jnp.exp(-z))
    return h + gate * e


def _qkv_attn_prompt_kernel(*refs, blocks_per_seq, n_casts):
    sinks_ref, xn_ref, xc_ref, g_ref, wqkv_ref, qkg_ref, wo_ref = refs[:7]
    out_ref, nkt_ref, nvt_ref = refs[7 + n_casts:10 + n_casts]
    (bias_ref, klo_scr, khi_scr, vt_scr, ot_scr, q_scr, k_scr, v_scr) = refs[10 + 2 * n_casts:]
    _cast_slabs(refs[7:7 + n_casts], refs[10 + n_casts:10 + 2 * n_casts])
    step = pl.program_id(0)
    tb = TOKEN_BLOCK
    new = step % 2
    cur = 1 - new
    block = jnp.maximum(step - 1, 0)

    @pl.when(step == 0)
    def _():
        in_window = lambda t, s: ((t < CHUNK) & (s < WINDOW + CHUNK)) | ((t >= CHUNK) & (s >= CHUNK))
        _fill_bias_t(bias_ref, 0, in_window)
        _fill_bias_t(bias_ref, 1, lambda t, s: in_window(t, s) & (s >= WINDOW))
        vt_scr[:, HEAD_DIM:, :] = jnp.ones((N_KV_HEADS, PV_ROWS - HEAD_DIM, WINDOW + tb), _BF16)
        q_scr[...] = jnp.zeros_like(q_scr)
        k_scr[...] = jnp.zeros_like(k_scr)
        v_scr[...] = jnp.zeros_like(v_scr)

    proj = _Staged(_qkv_steps(xn_ref[...], g_ref, wqkv_ref, qkg_ref, q_scr.at[new], k_scr.at[new],
                              v_scr.at[new]))
    proj.step()

    k_cur = k_scr[cur]
    kd = jnp.concatenate([k_scr[new, tb - WINDOW:, :], k_cur], axis=0)
    low_half = lax.broadcasted_iota(jnp.int32, (WINDOW + tb, LANES), 1) < HEAD_DIM
    for pair in range(N_KV_HEADS // 2):
        kk = kd[:, pair * LANES:(pair + 1) * LANES]
        swapped = pltpu.roll(kk, HEAD_DIM, axis=1)
        klo_scr[2 * pair] = jnp.where(low_half, kk, 0.0).astype(_BF16)
        khi_scr[2 * pair] = jnp.where(low_half, 0.0, swapped).astype(_BF16)
        klo_scr[2 * pair + 1] = jnp.where(low_half, swapped, 0.0).astype(_BF16)
        khi_scr[2 * pair + 1] = jnp.where(low_half, 0.0, kk).astype(_BF16)
    vt = jnp.concatenate([v_scr[new, tb - WINDOW:, :], v_scr[cur]], axis=0).T
    for kvh in range(N_KV_HEADS):
        vt_scr[kvh, :HEAD_DIM, :] = vt[kvh * HEAD_DIM:(kvh + 1) * HEAD_DIM, :].astype(_BF16)

    nkt_ref[...] = k_cur[tb - WINDOW:, :].T
    nvt_ref[...] = vt[:, tb:]

    first_in_seq = jnp.where(block % blocks_per_seq == 0, 1, 0)
    lane2 = lax.broadcasted_iota(jnp.int32, (1, 2 * ATTN_UNIT), 1)
    n_units = tb // ATTN_UNIT

    def scores(u):
        rows = slice(u * ATTN_UNIT, (u + 1) * ATTN_UNIT)
        keys = slice(u * ATTN_UNIT, u * ATTN_UNIT + ATTN_KEYS)
        variant = first_in_seq if u == 0 else 0
        out = []
        for kvh in range(N_KV_HEADS):
            k2 = jnp.concatenate([klo_scr[kvh, keys, :], khi_scr[kvh, keys, :]], axis=0)
            q2 = jnp.concatenate(
                [q_scr[cur, rows, (2 * kvh + g) * LANES:(2 * kvh + g + 1) * LANES]
                 for g in range(2)], axis=0)
            s_t = lax.dot_general(k2, q2, _CONTRACT_LAST, preferred_element_type=_F32)
            out.append(s_t + bias_ref[variant, kvh])
        return out

    s_next = scores(0)
    for u in range(n_units):
        rows = slice(u * ATTN_UNIT, (u + 1) * ATTN_UNIT)
        keys = slice(u * ATTN_UNIT, u * ATTN_UNIT + ATTN_KEYS)
        s_cur = s_next
        if u + 1 < n_units:
            s_next = scores(u + 1)
        proj.step()
        chains = []
        for kvh in range(N_KV_HEADS):
            for p in range(2):
                head0 = kvh * GQA_GROUP + p
                sink = jnp.where(lane2 < ATTN_UNIT, sinks_ref[0, head0], sinks_ref[0, head0 + 2])
                sp = s_cur[kvh][p * ATTN_KEYS:(p + 1) * ATTN_KEYS, :]
                m = jnp.maximum(jnp.max(sp, axis=0, keepdims=True), sink)
                chains.append((head0, kvh, jnp.exp(sp - m).astype(_BF16), jnp.exp(sink - m)))
        results = [(head0, _dot(vt_scr[kvh, :, keys], e), sink_e)
                   for head0, kvh, e, sink_e in chains]
        proj.step()
        for head0, r, sink_e in results:
            den = r[HEAD_DIM:HEAD_DIM + 1, :] + sink_e
            o = r[:HEAD_DIM, :] * (1.0 / den)
            for g in range(2):
                head = head0 + 2 * g
                ot_scr[head * HEAD_DIM:(head + 1) * HEAD_DIM, rows] = (
                    o[:, g * ATTN_UNIT:(g + 1) * ATTN_UNIT].astype(_BF16))

    out_ref[...] = xc_ref[...] + _dot(ot_scr[...].T, wo_ref[...])
    proj.finish()


def _qkv_attn_prompt(sinks, x, g_mix, w_qkv, qk_gain, w_o, seq, cast_weights=()):
    tb = TOKEN_BLOCK
    n_blocks = x.shape[0] // tb
    casts = _WeightCasts(cast_weights, n_blocks)
    nt = seq // tb
    proj_block = lambda s: jnp.minimum(s, n_blocks - 1)
    attn_block = lambda s: jnp.maximum(s - 1, 0)
    per_seq = pl.BlockSpec((KV_DIM, WINDOW), lambda s: (attn_block(s) // nt, 0))
    return pl.pallas_call(
        functools.partial(_qkv_attn_prompt_kernel, blocks_per_seq=nt, n_casts=len(casts)),
        grid=(n_blocks + 1,),
        in_specs=[pl.BlockSpec(memory_space=pltpu.SMEM),
                  pl.BlockSpec((tb, D_MODEL), lambda s: (proj_block(s), 0)),
                  pl.BlockSpec((tb, D_MODEL), lambda s: (attn_block(s), 0)),
                  _resident((1, D_MODEL), layer=0), _resident((D_MODEL, QKV_DIM)),
                  _resident((1, QK_DIM)), _resident((D_MODEL, D_MODEL))] + casts.in_specs,
        out_specs=[pl.BlockSpec((tb, D_MODEL), lambda s: (attn_block(s), 0)), per_seq,
                   per_seq] + casts.out_specs,
        out_shape=[jax.ShapeDtypeStruct(x.shape, _F32),
                   jax.ShapeDtypeStruct((x.shape[0] // seq * KV_DIM, WINDOW), _F32),
                   jax.ShapeDtypeStruct((x.shape[0] // seq * KV_DIM, WINDOW), _F32)]
        + casts.out_shape,
        scratch_shapes=[
            pltpu.VMEM((2, N_KV_HEADS, 2 * ATTN_KEYS, 2 * ATTN_UNIT), _F32),
            pltpu.VMEM((N_KV_HEADS, WINDOW + tb, LANES), _BF16),
            pltpu.VMEM((N_KV_HEADS, WINDOW + tb, LANES), _BF16),
            pltpu.VMEM((N_KV_HEADS, PV_ROWS, WINDOW + tb), _BF16),
            pltpu.VMEM((D_MODEL, tb), _BF16),
            pltpu.VMEM((2, tb, D_MODEL), _BF16),
            pltpu.VMEM((2, tb, KV_DIM), _F32),
            pltpu.VMEM((2, tb, KV_DIM), _F32),
        ],
        compiler_params=_params(1),
        name="qkv_attn_prompt",
    )(sinks, x, x, g_mix, w_qkv, qk_gain, w_o, *casts.arrays)


def _fill_bias(bias_new_ref, bias_old_ref, q_rows):
    t = lax.broadcasted_iota(jnp.int32, (q_rows, WINDOW), 0)
    s = lax.broadcasted_iota(jnp.int32, (q_rows, WINDOW), 1)
    dist_new = jnp.abs(t + WINDOW - (s + q_rows)).astype(_F32)
    dist_old = jnp.abs(t + WINDOW - s).astype(_F32)
    for kvh in range(N_KV_HEADS):
        for g in range(2):
            for p in range(2):
                slope = _alibi_slope(kvh * GQA_GROUP + 2 * g + p)
                rows = slice(g * q_rows, (g + 1) * q_rows)
                cols = slice(p * WINDOW, (p + 1) * WINDOW)
                bias_new_ref[kvh, rows, cols] = -slope * dist_new
                bias_old_ref[kvh, rows, cols] = jnp.where(s < q_rows, -slope * dist_old, NEG_INF)


def _block_diag2(a):
    z = jnp.zeros_like(a)
    return jnp.concatenate([jnp.concatenate([a, z], axis=1), jnp.concatenate([z, a], axis=1)],
                           axis=0)


def _attn_sample_kernel(sinks_ref, x_ref, q_ref, k_ref, ckt_ref, v_ref, cvt_ref, wo_ref,
                        out_ref, nkt_ref, nvt_ref, bias_new_ref, bias_old_ref, o_scr, *, dec_seq):
    kept = WINDOW - dec_seq

    @pl.when(pl.program_id(0) == 0)
    def _():
        _fill_bias(bias_new_ref, bias_old_ref, dec_seq)

    keep_lane = lax.broadcasted_iota(jnp.int32, (HEAD_DIM, WINDOW), 1) < kept
    pad = jnp.zeros((kept, LANES), _F32)
    ones_col = lax.broadcasted_iota(jnp.int32, (2 * BF16_SUBLANES, 2 * WINDOW), 1) // WINDOW
    ones_row = lax.broadcasted_iota(jnp.int32, (2 * BF16_SUBLANES, 2 * WINDOW), 0) // BF16_SUBLANES
    sum_rows = jnp.where(ones_col == ones_row, 1.0, 0.0).astype(_BF16)
    lane2 = lax.broadcasted_iota(jnp.int32, (2 * dec_seq, 2 * WINDOW), 1)
    lane1 = lax.broadcasted_iota(jnp.int32, (2 * dec_seq, LANES), 1)
    row2 = lax.broadcasted_iota(jnp.int32, (2 * dec_seq, 1), 0)

    def new_cols_t(rows_f32):
        return jnp.concatenate([pad, rows_f32], axis=0).T

    def one_batch(bb, carry):
        rows = pl.ds(pl.multiple_of(bb * dec_seq, dec_seq), dec_seq)
        k_new = k_ref[rows, :]
        v_new = v_ref[rows, :]
        kt_new = [new_cols_t(k_new[:, g * LANES:(g + 1) * LANES]) for g in range(KV_DIM // LANES)]
        vt_new = [new_cols_t(v_new[:, g * LANES:(g + 1) * LANES]) for g in range(KV_DIM // LANES)]

        staged = []
        for kvh in range(N_KV_HEADS):
            cache = pl.ds(pl.multiple_of(bb * KV_DIM + kvh * HEAD_DIM, HEAD_DIM), HEAD_DIM)
            kt_old = ckt_ref[cache, :]
            vt_old = cvt_ref[cache, :]
            half = slice((kvh % 2) * HEAD_DIM, (kvh % 2 + 1) * HEAD_DIM)
            kt = jnp.where(keep_lane, pltpu.roll(kt_old, kept, axis=1), kt_new[kvh // 2][half, :])
            vt = jnp.where(keep_lane, pltpu.roll(vt_old, kept, axis=1), vt_new[kvh // 2][half, :])
            nkt_ref[cache, :] = kt
            nvt_ref[cache, :] = vt
            q2 = jnp.concatenate(
                [q_ref[rows, (2 * kvh + g) * LANES:(2 * kvh + g + 1) * LANES] for g in range(2)],
                axis=0)
            s_new = _dot(q2, _block_diag2(kt.astype(_BF16))) + bias_new_ref[kvh]
            s_old = _dot(q2, _block_diag2(kt_old.astype(_BF16))) + bias_old_ref[kvh]
            v2_new = jnp.concatenate([_block_diag2(vt.astype(_BF16)), sum_rows], axis=0)
            v2_old = jnp.concatenate([_block_diag2(vt_old.astype(_BF16)), sum_rows], axis=0)
            staged.append((s_new, s_old, v2_new, v2_old))

        soft = []
        for kvh, (s_new, s_old, v2_new, v2_old) in enumerate(staged):
            s_max = jnp.maximum(s_new, s_old)
            head0 = kvh * GQA_GROUP
            ms, sink_es = [], []
            for p in range(2):
                sink = jnp.where(row2 < dec_seq, sinks_ref[0, head0 + p], sinks_ref[0, head0 + 2 + p])
                m_p = jnp.maximum(
                    jnp.max(s_max[:, p * WINDOW:(p + 1) * WINDOW], axis=-1, keepdims=True), sink)
                ms.append(m_p)
                sink_es.append(jnp.exp(sink - m_p))
            m = jnp.where(lane2 < WINDOW, ms[0], ms[1])
            soft.append((jnp.exp(s_new - m).astype(_BF16), jnp.exp(s_old - m).astype(_BF16),
                         sink_es, v2_new, v2_old))

        outs = [(lax.dot_general(e_new, v2_new, _CONTRACT_LAST, preferred_element_type=_F32)
                 + lax.dot_general(e_old, v2_old, _CONTRACT_LAST, preferred_element_type=_F32),
                 sink_es) for e_new, e_old, sink_es, v2_new, v2_old in soft]
        for kvh, (r, sink_es) in enumerate(outs):
            dens = [r[:, LANES + p * BF16_SUBLANES:LANES + p * BF16_SUBLANES + 1] + sink_es[p]
                    for p in range(2)]
            inv = jnp.where(lane1 < HEAD_DIM, 1.0 / dens[0], 1.0 / dens[1])
            o = (r[:, :LANES] * inv).astype(_BF16)
            for g in range(2):
                o_scr[rows, (2 * kvh + g) * LANES:(2 * kvh + g + 1) * LANES] = (
                    o[g * dec_seq:(g + 1) * dec_seq, :])
        return carry

    lax.fori_loop(0, TOKEN_BLOCK // dec_seq, one_batch, 0, unroll=2)
    out_ref[...] = x_ref[...] + _dot(o_scr[...], wo_ref[...])


def _attn_sample(sinks, x, q, k, v, cache_kt, cache_vt, w_o, dec_seq):
    n_tok = x.shape[0]
    tb = TOKEN_BLOCK
    batches_per_block = tb // dec_seq
    row = lambda width: pl.BlockSpec((tb, width), lambda i: (i, 0))
    cache = pl.BlockSpec((batches_per_block * KV_DIM, WINDOW), lambda i: (i, 0))
    return pl.pallas_call(
        functools.partial(_attn_sample_kernel, dec_seq=dec_seq),
        grid=(n_tok // tb,),
        in_specs=[pl.BlockSpec(memory_space=pltpu.SMEM),
                  row(D_MODEL), row(D_MODEL), row(KV_DIM), cache, row(KV_DIM), cache,
                  _resident((D_MODEL, D_MODEL))],
        out_specs=[row(D_MODEL), cache, cache],
        out_shape=[jax.ShapeDtypeStruct(x.shape, _F32),
                   jax.ShapeDtypeStruct(cache_kt.shape, _F32),
                   jax.ShapeDtypeStruct(cache_vt.shape, _F32)],
        scratch_shapes=[
            pltpu.VMEM((N_KV_HEADS, 2 * dec_seq, 2 * WINDOW), _F32),
            pltpu.VMEM((N_KV_HEADS, 2 * dec_seq, 2 * WINDOW), _F32),
            pltpu.VMEM((tb, D_MODEL), _BF16),
        ],
        compiler_params=_params(1),
        name="attn_sample",
    )(sinks, x, q, k, cache_kt, v, cache_vt, w_o)


def _ffn_ple_kernel(*refs, n_casts):
    h_ref, p_ref, gf_ref, w1_ref, w2_ref, gp_ref, wg_ref, wp_ref = refs[:8]
    out_ref = refs[8 + n_casts]
    _cast_slabs(refs[8:8 + n_casts], refs[9 + n_casts:])
    n_groups = h_ref.shape[0] // TOKEN_BLOCK
    groups = [slice(r * TOKEN_BLOCK, (r + 1) * TOKEN_BLOCK) for r in range(n_groups)]
    runs = [_Staged(_ffn_ple_steps(h_ref[rows, :], p_ref[rows, :].astype(_BF16), gf_ref, w1_ref,
                                   w2_ref, gp_ref, wg_ref, wp_ref)) for rows in groups]
    _run_interleaved(runs)
    for rows, run in zip(groups, runs):
        out_ref[rows, :] = run.result


def _ffn_ple(h, p, g_ffn, w1, w2, g_ple, w_gate, w_proj, layer, cast_weights=()):
    n_tok = h.shape[0]
    tb = PAIRED_TOKEN_BLOCK
    casts = _WeightCasts(cast_weights, n_tok // tb)
    row = lambda width: pl.BlockSpec((tb, width), lambda i: (i, 0))
    out = pl.pallas_call(
        functools.partial(_ffn_ple_kernel, n_casts=len(casts)),
        grid=(n_tok // tb,),
        in_specs=[row(D_MODEL), pl.BlockSpec((None, tb, PLE_DIM), lambda i: (layer, i, 0)),
                  _resident((1, D_MODEL), layer),
                  _resident((D_MODEL, D_FF)), _resident((D_FF, D_MODEL)),
                  _resident((1, D_MODEL), layer), _resident((D_MODEL, D_MODEL)),
                  _resident((PLE_DIM, D_MODEL))] + casts.in_specs,
        out_specs=[row(D_MODEL)] + casts.out_specs,
        out_shape=[jax.ShapeDtypeStruct(h.shape, _F32)] + casts.out_shape,
        compiler_params=_params(1),
        name="ffn_ple",
    )(h, p, g_ffn, w1, w2, g_ple, w_gate, w_proj, *casts.arrays)
    return out if cast_weights else out[0]


def _gelu_tanh(x):
    c = math.sqrt(2.0 / math.pi)
    return x * (0.5 * (1.0 + jnp.tanh(c * (x + 0.044715 * (x * x * x)))))


def _gmlp_kernel(*refs, emit_v, n_casts):
    h_ref, g_ref, wuv_ref, vg_ref, ws_ref, bs_ref, wout_ref = refs[:7]
    n_out = 2 if emit_v else 1
    out_ref = refs[7 + n_casts]
    vout_ref = refs[8 + n_casts] if emit_v else None
    v_scr, vb_scr = refs[7 + 2 * n_casts + n_out:]
    _cast_slabs(refs[7:7 + n_casts], refs[7 + n_casts + n_out:7 + 2 * n_casts + n_out])
    tb = h_ref.shape[0]
    h = h_ref[...]
    n = _rms(h, g_ref[...]).astype(_BF16)

    ssq = jnp.zeros((tb, 1), _F32)
    for c in range(GMLP_HALF // GMLP_V_CHUNK):
        cols = slice(c * GMLP_V_CHUNK, (c + 1) * GMLP_V_CHUNK)
        w_cols = slice(GMLP_HALF + c * GMLP_V_CHUNK, GMLP_HALF + (c + 1) * GMLP_V_CHUNK)
        vc = _gelu_tanh(_dot(n, wuv_ref[:, w_cols]))
        v_scr[:, cols] = vc
        ssq = ssq + jnp.sum(vc * vc, axis=-1, keepdims=True)
    inv = lax.rsqrt(ssq * (1.0 / GMLP_HALF) + EPS)
    for c in range(GMLP_HALF // GMLP_V_CHUNK):
        cols = slice(c * GMLP_V_CHUNK, (c + 1) * GMLP_V_CHUNK)
        vn = (v_scr[:, cols] * inv) * vg_ref[:, cols]
        if emit_v:
            vout_ref[:, cols] = vn
        vb_scr[:, cols] = vn.astype(_BF16)

    groups_per_step = U_CHUNK // GMLP_GROUP_DIM
    n_chunks = GMLP_HALF // U_CHUNK
    u_proj = lambda c: _gelu_tanh(_dot(n, wuv_ref[:, c * U_CHUNK:(c + 1) * U_CHUNK]))
    u_next = u_proj(0)
    for c in range(n_chunks):
        u = u_next
        if c + 1 < n_chunks:
            u_next = u_proj(c + 1)
        gated_rows = []
        for r in range(tb // GMLP_CHUNK):
            rows = slice(r * GMLP_CHUNK, (r + 1) * GMLP_CHUNK)
            parts = []
            for gg in range(groups_per_step):
                grp = c * groups_per_step + gg
                cols = slice(grp * GMLP_GROUP_DIM, (grp + 1) * GMLP_GROUP_DIM)
                s = _dot(ws_ref[grp], vb_scr[rows, cols]) + bs_ref[:, grp:grp + 1]
                parts.append(u[rows, gg * GMLP_GROUP_DIM:(gg + 1) * GMLP_GROUP_DIM] * s)
            gated_rows.append(jnp.concatenate(parts, axis=1))
        gated = jnp.concatenate(gated_rows, axis=0).astype(_BF16)
        h = h + _dot(gated, wout_ref[c * U_CHUNK:(c + 1) * U_CHUNK, :])
    out_ref[...] = h


def _gmlp(h, g, w_uv, v_gain, ws, bs_t, w_out, emit_v, cast_weights=()):
    n_tok = h.shape[0]
    tb = TOKEN_BLOCK
    casts = _WeightCasts(cast_weights, n_tok // tb)
    row = lambda width: pl.BlockSpec((tb, width), lambda i: (i, 0))
    out_specs = [row(D_MODEL)]
    out_shape = [jax.ShapeDtypeStruct(h.shape, _F32)]
    if emit_v:
        out_specs.append(row(GMLP_HALF))
        out_shape.append(jax.ShapeDtypeStruct((n_tok, GMLP_HALF), _F32))
    return pl.pallas_call(
        functools.partial(_gmlp_kernel, emit_v=emit_v, n_casts=len(casts)),
        grid=(n_tok // tb,),
        in_specs=[row(D_MODEL), _resident((1, D_MODEL), layer=1),
                  _resident((D_MODEL, 2 * GMLP_HALF)),
                  _resident((1, GMLP_HALF)),
                  _resident((GMLP_GROUPS, GMLP_CHUNK, GMLP_CHUNK)),
                  _resident((GMLP_CHUNK, GMLP_GROUPS)),
                  _resident((GMLP_HALF, D_MODEL))] + casts.in_specs,
        out_specs=out_specs + casts.out_specs,
        out_shape=out_shape + casts.out_shape,
        scratch_shapes=[pltpu.VMEM((tb, GMLP_HALF), _F32), pltpu.VMEM((tb, GMLP_HALF), _BF16)],
        compiler_params=_params(1),
        name="gmlp_v" if emit_v else "gmlp",
    )(h, g, w_uv, v_gain, ws, bs_t, w_out, *casts.arrays)


def _gmlp_spatial_weights(w_s, b_s, length):
    tril = jnp.tril(jnp.ones((length, length), w_s.dtype))
    ws = w_s[:, :length, :length] * tril
    reps = GMLP_CHUNK // length
    if reps > 1:
        eye = jnp.eye(reps, dtype=w_s.dtype)
        ws = jnp.einsum("ab,gij->gaibj", eye, ws).reshape(GMLP_GROUPS, GMLP_CHUNK, GMLP_CHUNK)
    bs_t = jnp.tile(b_s[:, :length].T, (reps, 1))
    return ws.astype(_BF16), bs_t


def _to_cache_t(cache):
    _, b, rows, kvh, hd = cache.shape
    return jnp.transpose(cache, (0, 1, 3, 4, 2)).reshape(b * kvh * hd, rows)


def _from_cache_t(cache_t, batch):
    t = cache_t.reshape(1, batch, N_KV_HEADS, HEAD_DIM, cache_t.shape[1])
    return jnp.transpose(t, (0, 1, 4, 2, 3))


def kernel(x_prompt, x_sample, p_prompt, p_sample, cache_k, cache_v, g_mix, g_ffn, g_ple,
           attn_w_qkv, attn_q_norm, attn_k_norm, attn_sinks, attn_w_o, gmlp_w_uv, gmlp_v_norm,
           gmlp_w_s, gmlp_b_s, gmlp_w_out, ffn_w1, ffn_w2, ple_w_proj, ple_w_gate):
    batch, seq, _ = x_prompt.shape
    dec_batch, dec_seq, _ = x_sample.shape
    depth = g_mix.shape[0]
    cache_rows = cache_k.shape[2]
    assert depth == 2 and attn_w_qkv.shape[0] == 1 and gmlp_w_uv.shape[0] == 1
    assert seq % PAIRED_TOKEN_BLOCK == 0 and (dec_batch * dec_seq) % PAIRED_TOKEN_BLOCK == 0
    assert TOKEN_BLOCK % (2 * dec_seq) == 0 and GMLP_CHUNK % dec_seq == 0 and dec_seq % 16 == 0
    assert cache_rows == WINDOW and dec_seq <= WINDOW

    scale = HEAD_DIM ** -0.5
    g_mix = g_mix.reshape(depth, 1, D_MODEL)
    g_ffn = g_ffn.reshape(depth, 1, D_MODEL)
    g_ple = g_ple.reshape(depth, 1, D_MODEL)
    w_qkv = attn_w_qkv[0].astype(_BF16)
    w_o = attn_w_o[0].astype(_BF16)
    qk_gain = jnp.concatenate([jnp.tile(attn_q_norm[0] * scale, N_HEADS),
                               jnp.tile(attn_k_norm[0], N_KV_HEADS)]).reshape(1, QK_DIM)
    sinks = attn_sinks[0].reshape(1, N_HEADS)
    v_gain = gmlp_v_norm[0].reshape(1, GMLP_HALF)
    ws_p, bs_p = _gmlp_spatial_weights(gmlp_w_s[0], gmlp_b_s[0], min(seq, GMLP_CHUNK))
    ws_s, bs_s = _gmlp_spatial_weights(gmlp_w_s[0], gmlp_b_s[0], min(dec_seq, GMLP_CHUNK))

    n_prompt = batch * seq
    n_sample = dec_batch * dec_seq
    x_p = x_prompt.reshape(n_prompt, D_MODEL)
    x_s = x_sample.reshape(n_sample, D_MODEL)
    p_p = p_prompt.reshape(depth, n_prompt, PLE_DIM)
    p_s = p_sample.reshape(depth, n_sample, PLE_DIM)

    layer_weights = lambda layer: [(ffn_w1, layer), (ffn_w2, layer), (ple_w_gate, layer),
                                   (ple_w_proj, layer)]
    h_p, nkt_p, nvt_p, *mlp0 = _qkv_attn_prompt(sinks, x_p, g_mix, w_qkv, qk_gain, w_o, seq,
                                                cast_weights=layer_weights(0))
    h_p, w_uv, w_out = _ffn_ple(h_p, p_p, g_ffn, mlp0[0], mlp0[1], g_ple, mlp0[2], mlp0[3], 0,
                                cast_weights=[(gmlp_w_uv, 0), (gmlp_w_out, 0)])
    h_p, *mlp1 = _gmlp(h_p, g_mix, w_uv, v_gain, ws_p, bs_p, w_out, False,
                       cast_weights=layer_weights(1))
    y_p = _ffn_ple(h_p, p_p, g_ffn, mlp1[0], mlp1[1], g_ple, mlp1[2], mlp1[3], 1)

    q, k, v = _qkv_proj(x_s, g_mix, w_qkv, qk_gain)
    h_s, nkt_s, nvt_s = _attn_sample(sinks, x_s, q, k, v, _to_cache_t(cache_k),
                                     _to_cache_t(cache_v), w_o, dec_seq)
    h_s = _ffn_ple(h_s, p_s, g_ffn, mlp0[0], mlp0[1], g_ple, mlp0[2], mlp0[3], 0)
    h_s, vrows = _gmlp(h_s, g_mix, w_uv, v_gain, ws_s, bs_s, w_out, True)
    y_s = _ffn_ple(h_s, p_s, g_ffn, mlp1[0], mlp1[1], g_ple, mlp1[2], mlp1[3], 1)
    return (y_p.reshape(batch, seq, D_MODEL), y_s.reshape(dec_batch, dec_seq, D_MODEL),
            _from_cache_t(nkt_p, batch), _from_cache_t(nvt_p, batch),
            _from_cache_t(nkt_s, dec_batch), _from_cache_t(nvt_s, dec_batch),
            vrows.reshape(1, dec_batch, dec_seq, GMLP_HALF))
```
